```python
import math
import jax
import jax.numpy as jnp
from jax import lax
import numpy as np

D_MODEL = 1024
BATCH = 32
SEQ = 2048
DEPTH = 4
DEC_BATCH = 8
DEC_SEQ = 4096
PAST_LEN = 128

HEAD_DIM = 64
GROUP_WIDTH = D_MODEL // 4
SSM_GROUP_CH = 16
SSM_GROUPS = GROUP_WIDTH // SSM_GROUP_CH
SSM_STATE = 64
GQA_HEADS = GROUP_WIDTH // HEAD_DIM
GQA_KV_HEADS = GQA_HEADS // 2
GDN_HEADS = GROUP_WIDTH // HEAD_DIM
GDN_CONV = 5
GDN_CHUNK = 64
DIFF_HEADS = GROUP_WIDTH // HEAD_DIM
DIFF_SUB_DIM = HEAD_DIM // 2
PARTIAL_ROPE_DIMS = DIFF_SUB_DIM // 4
AXIAL_THETA = 10000.0
ROPE_THETA = 500000.0
GRID_W = 64
Q_BLOCK = 128
N_EXPERTS = 32
TOP_K = 4
D_EXPERT = D_MODEL
SWIGLU_ALPHA = 1.702
SWIGLU_LIMIT = 7.0
EXPERT_BLOCK = 256
NORM_EPS = 1e-6
LN_EPS = 1e-5
IN_SPLITS = (GROUP_WIDTH,
             GQA_HEADS * HEAD_DIM, GQA_KV_HEADS * HEAD_DIM, GQA_KV_HEADS * HEAD_DIM,
             GDN_HEADS * HEAD_DIM, GDN_HEADS * HEAD_DIM, GDN_HEADS * HEAD_DIM, GDN_HEADS * HEAD_DIM,
             2 * GDN_HEADS, 2 * GDN_HEADS,
             DIFF_HEADS * 2 * DIFF_SUB_DIM, DIFF_HEADS * 2 * DIFF_SUB_DIM, DIFF_HEADS * HEAD_DIM)
IN_WIDTH = sum(IN_SPLITS)

kernel_name = 'hybrid_parallel_head_encoder'


def _split_points():
    return [int(s) for s in np.cumsum(IN_SPLITS)[:-1]]


def _layer_norm(x, g, b):
    xf = x.astype(jnp.float32)
    mu = jnp.mean(xf, axis=-1, keepdims=True)
    var = jnp.mean(jnp.square(xf - mu), axis=-1, keepdims=True)
    return ((xf - mu) * lax.rsqrt(var + LN_EPS) * g + b).astype(x.dtype)


def _rms_norm(x, g):
    xf = x.astype(jnp.float32)
    return (xf * lax.rsqrt(jnp.mean(xf * xf, axis=-1, keepdims=True) + NORM_EPS) * g).astype(x.dtype)


def _l2norm(x):
    xf = x.astype(jnp.float32)
    return xf * lax.rsqrt(jnp.sum(xf * xf, axis=-1, keepdims=True) + NORM_EPS)


def _rope_table(pos, dim, theta):
    inv = theta ** (-jnp.arange(0, dim, 2, dtype=jnp.float32) / dim)
    ang = pos.astype(jnp.float32)[:, None] * inv[None, :]
    return jnp.cos(ang), jnp.sin(ang)


def _apply_rope(x, cos, sin):
    half = x.shape[-1] // 2
    xf = x.astype(jnp.float32)
    x1, x2 = xf[..., :half], xf[..., half:]
    c = cos[None, :, None, :]
    s = sin[None, :, None, :]
    return jnp.concatenate([x1 * c - x2 * s, x2 * c + x1 * s], axis=-1).astype(x.dtype)


def _apply_axial(x, axial):
    row_cos, row_sin, col_cos, col_sin = axial
    half = x.shape[-1] // 2
    return jnp.concatenate([_apply_rope(x[..., :half], row_cos, row_sin),
                            _apply_rope(x[..., half:], col_cos, col_sin)], axis=-1)


def _query_blocks(t):
    bn, l = t.shape[:2]
    return jnp.moveaxis(t.reshape((bn, l // Q_BLOCK, Q_BLOCK) + t.shape[2:]), 1, 0)


def _merge_blocks(o):
    o = jnp.moveaxis(o, 0, 1)
    return o.reshape((o.shape[0], o.shape[1] * o.shape[2]) + o.shape[3:])


def _centred_conv(x, w):
    k = w.shape[0]
    return lax.conv_general_dilated(x, w[:, None, :], window_strides=(1,),
                                    padding=[(k // 2, k // 2)],
                                    dimension_numbers=('NWC', 'WIO', 'NWC'),
                                    feature_group_count=x.shape[-1])


def _complex_affine_combine(e1, e2):
    a1r, a1i, b1r, b1i = e1
    a2r, a2i, b2r, b2i = e2
    return (a2r * a1r - a2i * a1i,
            a2r * a1i + a2i * a1r,
            a2r * b1r - a2i * b1i + b2r,
            a2r * b1i + a2i * b1r + b2i)


def _s5_mixer(u, a_re, a_im, log_dt, b_re, b_im, c_re, c_im, d, glu_w, glu_b):
    f32 = jnp.float32
    bn, l, _ = u.shape
    uf = u.astype(f32).reshape(bn, l, SSM_GROUPS, SSM_GROUP_CH)
    y = uf * d.astype(f32)
    for direction, rev in ((0, False), (1, True)):
        lam_re = a_re[direction].astype(f32)
        lam_im = a_im[direction].astype(f32)
        dt = jnp.exp(log_dt[direction].astype(f32))[:, None]
        mag = jnp.exp(lam_re * dt)
        abar_re = mag * jnp.cos(lam_im * dt)
        abar_im = mag * jnp.sin(lam_im * dt)
        den = lam_re * lam_re + lam_im * lam_im
        coef_re = ((abar_re - 1.0) * lam_re + abar_im * lam_im) / den
        coef_im = (abar_im * lam_re - (abar_re - 1.0) * lam_im) / den
        br = b_re[direction].astype(f32)
        bi = b_im[direction].astype(f32)
        bbar_re = coef_re[..., None] * br - coef_im[..., None] * bi
        bbar_im = coef_re[..., None] * bi + coef_im[..., None] * br
        bu_re = jnp.einsum('blgh,gph->blgp', uf, bbar_re)
        bu_im = jnp.einsum('blgh,gph->blgp', uf, bbar_im)
        elems = (jnp.broadcast_to(abar_re, bu_re.shape), jnp.broadcast_to(abar_im, bu_re.shape), bu_re, bu_im)
        _, _, s_re, s_im = lax.associative_scan(_complex_affine_combine, elems, reverse=rev, axis=1)
        y = (y + jnp.einsum('blgp,ghp->blgh', s_re, c_re[direction].astype(f32))
             - jnp.einsum('blgp,ghp->blgh', s_im, c_im[direction].astype(f32)))
    y = jax.nn.gelu(y.reshape(bn, l, GROUP_WIDTH))
    out = y * jax.nn.sigmoid(y @ glu_w.astype(f32) + glu_b.astype(f32))
    return out.astype(u.dtype)


def _gqa_mixer(q, k, v, q_norm, k_norm, axial):
    bn, l, _ = q.shape
    q = _apply_axial(_rms_norm(q.reshape(bn, l, GQA_HEADS, HEAD_DIM), q_norm), axial)
    k = _apply_axial(_rms_norm(k.reshape(bn, l, GQA_KV_HEADS, HEAD_DIM), k_norm), axial)
    v = v.reshape(bn, l, GQA_KV_HEADS, HEAD_DIM)
    q = q.reshape(bn, l, GQA_KV_HEADS, GQA_HEADS // GQA_KV_HEADS, HEAD_DIM)
    scale = HEAD_DIM ** -0.5

    def block(qb):
        s = jnp.einsum('bqhgd,bkhd->bhgqk', qb, k).astype(jnp.float32) * scale
        p = jax.nn.softmax(s, axis=-1).astype(v.dtype)
        return jnp.einsum('bhgqk,bkhd->bqhgd', p, v)

    o = _merge_blocks(lax.map(block, _query_blocks(q)))
    return o.reshape(bn, l, GQA_HEADS * HEAD_DIM)


def _gated_delta_chunked(q, k, v, g, beta):
    bn, l, h, dk = q.shape
    dv = v.shape[-1]
    c = GDN_CHUNK
    n = l // c

    def chunks(t):
        return jnp.transpose(t.reshape(bn, n, c, h, t.shape[-1]), (0, 3, 1, 2, 4))

    def chunks_h(t):
        return jnp.transpose(t.reshape(bn, n, c, h), (0, 3, 1, 2))

    q = chunks(q) * (dk ** -0.5)
    k = chunks(k)
    v = chunks(v)
    beta = chunks_h(beta)
    g = jnp.cumsum(chunks_h(g), axis=-1)
    causal = jnp.tril(jnp.ones((c, c), dtype=bool))
    strict = jnp.tril(jnp.ones((c, c), dtype=bool), k=-1)
    decay = jnp.exp(jnp.where(causal, g[..., :, None] - g[..., None, :], -jnp.inf))
    k_beta = k * beta[..., None]
    lower = jnp.where(strict, jnp.einsum('bhncd,bhnsd->bhncs', k_beta, k) * decay, 0.0)
    t_mat = lower + jnp.eye(c, dtype=lower.dtype)
    u = lax.linalg.triangular_solve(t_mat, v * beta[..., None], left_side=True, lower=True, unit_diagonal=True)
    w = lax.linalg.triangular_solve(t_mat, k_beta * jnp.exp(g)[..., None], left_side=True, lower=True, unit_diagonal=True)
    intra = jnp.einsum('bhncd,bhnsd->bhncs', q, k) * decay

    def step(s, xs):
        q_c, k_c, u_c, w_c, g_c, a_c = xs
        v_new = u_c - jnp.einsum('bhck,bhkv->bhcv', w_c, s)
        o_c = (jnp.einsum('bhck,bhkv->bhcv', q_c * jnp.exp(g_c)[..., None], s)
               + jnp.einsum('bhcs,bhsv->bhcv', a_c, v_new))
        g_last = g_c[..., -1:]
        s = (s * jnp.exp(g_last)[..., None]
             + jnp.einsum('bhck,bhcv->bhkv', k_c * jnp.exp(g_last - g_c)[..., None], v_new))
        return s, o_c

    xs = tuple(jnp.moveaxis(t, 2, 0) for t in (q, k, u, w, g, intra))
    s0 = jnp.zeros((bn, h, dk, dv), jnp.float32)
    _, o = lax.scan(step, s0, xs)
    return jnp.transpose(o, (1, 0, 3, 2, 4)).reshape(bn, l, h, dv)


def _gdn_mixer(cq, ck, cv, cz, ca, cb, conv_w, a_log, dt_bias, o_norm):
    f32 = jnp.float32
    bn, l, _ = cq.shape
    qkv = jax.nn.silu(_centred_conv(jnp.concatenate([cq, ck, cv], axis=-1), conv_w))
    q, k, v = jnp.split(qkv, 3, axis=-1)
    q = _l2norm(q.reshape(bn, l, GDN_HEADS, HEAD_DIM))
    k = _l2norm(k.reshape(bn, l, GDN_HEADS, HEAD_DIM))
    v = v.reshape(bn, l, GDN_HEADS, HEAD_DIM).astype(f32)
    ca = ca.astype(f32).reshape(bn, l, 2, GDN_HEADS)
    cb = cb.astype(f32).reshape(bn, l, 2, GDN_HEADS)
    g = -jnp.exp(a_log.astype(f32)) * jax.nn.softplus(ca + dt_bias.astype(f32))
    beta = jax.nn.sigmoid(cb)
    flip = lambda t: t[:, ::-1]
    o_fwd = _gated_delta_chunked(q, k, v, g[:, :, 0], beta[:, :, 0])
    o_bwd = flip(_gated_delta_chunked(flip(q), flip(k), flip(v), flip(g[:, :, 1]), flip(beta[:, :, 1])))
    gate = jax.nn.silu(cz.astype(f32).reshape(bn, l, GDN_HEADS, HEAD_DIM))
    o = _rms_norm(o_fwd + o_bwd, o_norm) * gate
    return o.reshape(bn, l, GDN_HEADS * HEAD_DIM).astype(cq.dtype)


def _diff_mixer(q, k, v, lq1, lk1, lq2, lk2, subln, rope, lambda_init):
    bn, l, _ = q.shape
    cos, sin = rope
    q = q.reshape(bn, l, DIFF_HEADS, 2, DIFF_SUB_DIM)
    k = k.reshape(bn, l, DIFF_HEADS, 2, DIFF_SUB_DIM)
    v = v.reshape(bn, l, DIFF_HEADS, HEAD_DIM)

    def partial_rope(t):
        return jnp.concatenate([_apply_rope(t[..., :PARTIAL_ROPE_DIMS], cos, sin), t[..., PARTIAL_ROPE_DIMS:]], axis=-1)

    q1, q2 = partial_rope(q[..., 0, :]), partial_rope(q[..., 1, :])
    k1, k2 = partial_rope(k[..., 0, :]), partial_rope(k[..., 1, :])
    f32 = jnp.float32
    lam = (jnp.exp(jnp.sum(lq1.astype(f32) * lk1.astype(f32)))
           - jnp.exp(jnp.sum(lq2.astype(f32) * lk2.astype(f32))) + lambda_init)
    scale = DIFF_SUB_DIM ** -0.5

    def block(qs):
        qb1, qb2 = qs
        p1 = jax.nn.softmax(jnp.einsum('bqhd,bkhd->bhqk', qb1, k1).astype(f32) * scale, axis=-1)
        p2 = jax.nn.softmax(jnp.einsum('bqhd,bkhd->bhqk', qb2, k2).astype(f32) * scale, axis=-1)
        p = (p1 - lam * p2).astype(v.dtype)
        return jnp.einsum('bhqk,bkhd->bqhd', p, v)

    o = _merge_blocks(lax.map(block, (_query_blocks(q1), _query_blocks(q2))))
    o = _rms_norm(o, subln) * (1.0 - lambda_init)
    return o.reshape(bn, l, DIFF_HEADS * HEAD_DIM).astype(q.dtype)


def _moe(x, router_w, router_b, w1, b1, w2, b2):
    bn, l, dm = x.shape
    t = bn * l
    m = t * TOP_K
    xt = x.reshape(t, dm)
    logits = (xt @ router_w + router_b).astype(jnp.float32)
    top_vals, top_idx = lax.top_k(logits, TOP_K)
    gates = jax.nn.softmax(top_vals, axis=-1).reshape(-1)
    flat_e = top_idx.reshape(-1)
    order = jnp.argsort(flat_e)
    sorted_e = flat_e[order]
    counts = jnp.bincount(flat_e, length=N_EXPERTS)
    padded = (counts + EXPERT_BLOCK - 1) // EXPERT_BLOCK * EXPERT_BLOCK
    group_start = jnp.cumsum(counts) - counts
    padded_end = jnp.cumsum(padded)
    padded_start = padded_end - padded
    dest = padded_start[sorted_e] + jnp.arange(m) - group_start[sorted_e]
    n_blocks = -(-m // EXPERT_BLOCK) + N_EXPERTS
    n_slots = n_blocks * EXPERT_BLOCK
    slot_token = jnp.full((n_slots,), t, jnp.int32).at[dest].set((order // TOP_K).astype(jnp.int32))
    slot_gate = jnp.zeros((n_slots,), jnp.float32).at[dest].set(gates[order])
    block_expert = jnp.minimum(jnp.searchsorted(padded_end, jnp.arange(n_blocks) * EXPERT_BLOCK, side='right'), N_EXPERTS - 1)
    x_pad = jnp.concatenate([xt, jnp.zeros((1, dm), xt.dtype)], axis=0)
    xb = x_pad[slot_token].reshape(n_blocks, EXPERT_BLOCK, dm)

    def expert_block(args):
        xe, e = args
        hdn = xe @ w1[e] + b1[e]
        x_glu = jnp.minimum(hdn[:, ::2], SWIGLU_LIMIT)
        x_lin = jnp.clip(hdn[:, 1::2], -SWIGLU_LIMIT, SWIGLU_LIMIT)
        act = x_glu * jax.nn.sigmoid(SWIGLU_ALPHA * x_glu) * (x_lin + 1.0)
        return act @ w2[e] + b2[e]

    yb = lax.map(expert_block, (xb, block_expert)).reshape(n_slots, dm)
    yb = yb * slot_gate[:, None].astype(yb.dtype)
    out = jnp.zeros((t + 1, dm), yb.dtype).at[slot_token].add(yb)[:t]
    return out.reshape(bn, l, dm).astype(x.dtype)


def _layer(x, pl, axial, rope, lambda_init, alpha):
    proj = x @ pl['w_in']
    (u, gq, gk, gv, cq, ck, cv, cz, ca, cb, dq, dk, dv) = jnp.split(proj, _split_points(), axis=-1)
    a_out = _s5_mixer(u, pl['ssm_a_re'], pl['ssm_a_im'], pl['ssm_log_dt'], pl['ssm_b_re'], pl['ssm_b_im'],
                      pl['ssm_c_re'], pl['ssm_c_im'], pl['ssm_d'], pl['ssm_glu_w'], pl['ssm_glu_b'])
    b_out = _gqa_mixer(gq, gk, gv, pl['gqa_q_norm'], pl['gqa_k_norm'], axial)
    c_out = _gdn_mixer(cq, ck, cv, cz, ca, cb, pl['gdn_conv_w'], pl['gdn_a_log'], pl['gdn_dt_bias'], pl['gdn_o_norm'])
    d_out = _diff_mixer(dq, dk, dv, pl['diff_lambda_q1'], pl['diff_lambda_k1'], pl['diff_lambda_q2'],
                        pl['diff_lambda_k2'], pl['diff_subln'], rope, lambda_init)
    mixed = jnp.concatenate([a_out, b_out, c_out, d_out], axis=-1) @ pl['w_out']
    x = _layer_norm(alpha * x + mixed, pl['ln1_g'], pl['ln1_b'])
    ffn = _moe(x, pl['router_w'], pl['router_b'], pl['moe_w1'], pl['moe_b1'], pl['moe_w2'], pl['moe_b2'])
    return _layer_norm(alpha * x + ffn, pl['ln2_g'], pl['ln2_b'])


def _trunk(x, params):
    l = x.shape[1]
    rows = l // GRID_W
    row_pos = jnp.broadcast_to(jnp.arange(rows)[:, None], (rows, GRID_W)).reshape(-1)
    col_pos = jnp.broadcast_to(jnp.arange(GRID_W)[None, :], (rows, GRID_W)).reshape(-1)
    row_cos, row_sin = _rope_table(row_pos, HEAD_DIM // 2, AXIAL_THETA)
    col_cos, col_sin = _rope_table(col_pos, HEAD_DIM // 2, AXIAL_THETA)
    axial = (row_cos, row_sin, col_cos, col_sin)
    rope = _rope_table(jnp.arange(l), PARTIAL_ROPE_DIMS, ROPE_THETA)
    alpha = (2.0 * DEPTH) ** 0.25
    for layer in range(DEPTH):
        pl = {name: arr[layer] for name, arr in params.items()}
        lambda_init = 0.8 - 0.6 * math.exp(-0.3 * layer)
        x = _layer(x, pl, axial, rope, lambda_init, alpha)
    return x


def setup_inputs(seed: int = 0) -> dict:
    key = jax.random.key(seed)
    ks = iter(jax.random.split(key, 48))
    f32 = jnp.float32

    def nrm(shape, scale):
        return jax.random.normal(next(ks), shape, f32) * scale

    def unif(shape, lo, hi):
        return jax.random.uniform(next(ks), shape, f32, lo, hi)

    out_scale = (8.0 * DEPTH) ** -0.25
    nl = DEPTH
    dt_gdn = jnp.exp(unif((nl, 2, GDN_HEADS), math.log(1e-3), math.log(1e-1)))
    return {
        'x_prompt': nrm((BATCH, SEQ, D_MODEL), 1.0),
        'x_sample': nrm((DEC_BATCH, DEC_SEQ, D_MODEL), 1.0),
        'w_in': nrm((nl, D_MODEL, IN_WIDTH), D_MODEL ** -0.5),
        'w_out': nrm((nl, D_MODEL, D_MODEL), D_MODEL ** -0.5 * out_scale),
        'ssm_a_re': -0.5 + nrm((nl, 2, SSM_GROUPS, SSM_STATE), 0.01),
        'ssm_a_im': jnp.pi * jnp.arange(SSM_STATE, dtype=f32) + nrm((nl, 2, SSM_GROUPS, SSM_STATE), 0.01),
        'ssm_log_dt': unif((nl, 2, SSM_GROUPS), math.log(1e-3), math.log(1e-1)),
        'ssm_b_re': nrm((nl, 2, SSM_GROUPS, SSM_STATE, SSM_GROUP_CH), (2.0 * SSM_GROUP_CH) ** -0.5),
        'ssm_b_im': nrm((nl, 2, SSM_GROUPS, SSM_STATE, SSM_GROUP_CH), (2.0 * SSM_GROUP_CH) ** -0.5),
        'ssm_c_re': nrm((nl, 2, SSM_GROUPS, SSM_GROUP_CH, SSM_STATE), (2.0 * SSM_STATE) ** -0.5),
        'ssm_c_im': nrm((nl, 2, SSM_GROUPS, SSM_GROUP_CH, SSM_STATE), (2.0 * SSM_STATE) ** -0.5),
        'ssm_d': nrm((nl, SSM_GROUPS, SSM_GROUP_CH), 1.0),
        'ssm_glu_w': nrm((nl, GROUP_WIDTH, GROUP_WIDTH), GROUP_WIDTH ** -0.5),
        'ssm_glu_b': nrm((nl, GROUP_WIDTH), 0.01),
        'gqa_q_norm': 1.0 + nrm((nl, HEAD_DIM), 0.01),
        'gqa_k_norm': 1.0 + nrm((nl, HEAD_DIM), 0.01),
        'gdn_conv_w': nrm((nl, GDN_CONV, 3 * GDN_HEADS * HEAD_DIM), GDN_CONV ** -0.5),
        'gdn_a_log': jnp.log(unif((nl, 2, GDN_HEADS), 1.0, 16.0)),
        'gdn_dt_bias': dt_gdn + jnp.log(-jnp.expm1(-dt_gdn)),
        'gdn_o_norm': 1.0 + nrm((nl, HEAD_DIM), 0.01),
        'diff_lambda_q1': nrm((nl, DIFF_SUB_DIM), 0.1),
        'diff_lambda_k1': nrm((nl, DIFF_SUB_DIM), 0.1),
        'diff_lambda_q2': nrm((nl, DIFF_SUB_DIM), 0.1),
        'diff_lambda_k2': nrm((nl, DIFF_SUB_DIM), 0.1),
        'diff_subln': 1.0 + nrm((nl, HEAD_DIM), 0.01),
        'router_w': nrm((nl, D_MODEL, N_EXPERTS), D_MODEL ** -0.5),
        'router_b': nrm((nl, N_EXPERTS), 0.01),
        'moe_w1': nrm((nl, N_EXPERTS, D_MODEL, 2 * D_EXPERT), D_MODEL ** -0.5),
        'moe_b1': nrm((nl, N_EXPERTS, 2 * D_EXPERT), 0.01),
        'moe_w2': nrm((nl, N_EXPERTS, D_EXPERT, D_MODEL), D_EXPERT ** -0.5 * out_scale),
        'moe_b2': nrm((nl, N_EXPERTS, D_MODEL), 0.01),
        'ln1_g': 1.0 + nrm((nl, D_MODEL), 0.01),
        'ln1_b': nrm((nl, D_MODEL), 0.01),
        'ln2_g': 1.0 + nrm((nl, D_MODEL), 0.01),
        'ln2_b': nrm((nl, D_MODEL), 0.01),
    }


def reference(x_prompt, x_sample, w_in, w_out, ssm_a_re, ssm_a_im, ssm_log_dt, ssm_b_re, ssm_b_im,
              ssm_c_re, ssm_c_im, ssm_d, ssm_glu_w, ssm_glu_b, gqa_q_norm, gqa_k_norm, gdn_conv_w,
              gdn_a_log, gdn_dt_bias, gdn_o_norm, diff_lambda_q1, diff_lambda_k1, diff_lambda_q2,
              diff_lambda_k2, diff_subln, router_w, router_b, moe_w1, moe_b1, moe_w2, moe_b2,
              ln1_g, ln1_b, ln2_g, ln2_b):
    params = {
        'w_in': w_in, 'w_out': w_out,
        'ssm_a_re': ssm_a_re, 'ssm_a_im': ssm_a_im, 'ssm_log_dt': ssm_log_dt,
        'ssm_b_re': ssm_b_re, 'ssm_b_im': ssm_b_im, 'ssm_c_re': ssm_c_re, 'ssm_c_im': ssm_c_im,
        'ssm_d': ssm_d, 'ssm_glu_w': ssm_glu_w, 'ssm_glu_b': ssm_glu_b,
        'gqa_q_norm': gqa_q_norm, 'gqa_k_norm': gqa_k_norm,
        'gdn_conv_w': gdn_conv_w, 'gdn_a_log': gdn_a_log, 'gdn_dt_bias': gdn_dt_bias, 'gdn_o_norm': gdn_o_norm,
        'diff_lambda_q1': diff_lambda_q1, 'diff_lambda_k1': diff_lambda_k1,
        'diff_lambda_q2': diff_lambda_q2, 'diff_lambda_k2': diff_lambda_k2, 'diff_subln': diff_subln,
        'router_w': router_w, 'router_b': router_b,
        'moe_w1': moe_w1, 'moe_b1': moe_b1, 'moe_w2': moe_w2, 'moe_b2': moe_b2,
        'ln1_g': ln1_g, 'ln1_b': ln1_b, 'ln2_g': ln2_g, 'ln2_b': ln2_b,
    }
    y_prompt = _trunk(x_prompt, params)
    y_sample = _trunk(x_sample, params)
    return (y_prompt, y_sample)
```

```python
import functools
import math

import jax
import jax.numpy as jnp
import numpy as np
from jax import lax
from jax.experimental import pallas as pl
from jax.experimental.pallas import tpu as pltpu

F32 = jnp.float32
BF16 = jnp.bfloat16

D_MODEL = 1024
GROUP_WIDTH = 256
HEAD_DIM = 64
SSM_GROUPS = 16
SSM_GROUP_CH = 16
SSM_STATE = 64
GDN_HEADS = 4
GDN_CHUNK = 64
DIFF_SUB_DIM = 32
PARTIAL_ROPE_DIMS = 8
AXIAL_THETA = 10000.0
ROPE_THETA = 500000.0
GRID_W = 64
N_EXPERTS = 32
TOP_K = 4
SWIGLU_ALPHA = 1.702
SWIGLU_LIMIT = 7.0
NORM_EPS = 1e-6
LN_EPS = 1e-5

LANES = 128
VMEM_LIMIT = 56 * 1024 * 1024

COL_U, COL_GQ, COL_GK, COL_GV = 0, 256, 512, 640
COL_CQ, COL_CK, COL_CV, COL_CZ = 768, 1024, 1280, 1536
COL_DQ, COL_DK, COL_DV = 1792, 2048, 2304
MAIN_W = 2560
GATE_W = 128
ORIG_CA = 1792


def _cparams(sem):
    return pltpu.CompilerParams(dimension_semantics=sem, vmem_limit_bytes=VMEM_LIMIT)


def _dot(a, b):
    return jnp.dot(a, b, preferred_element_type=F32)


def _dot_nt(a, b):
    return lax.dot_general(a, b, (((1,), (1,)), ((), ())), preferred_element_type=F32)


def _split_dot(x, m):
    hi = x.astype(BF16)
    lo = (x - hi.astype(F32)).astype(BF16)
    return _dot(hi, m) + _dot(lo, m)


def _inproj_body(x_ref, w_ref, main_ref, gate_ref):
    x = x_ref[...].astype(BF16)
    step = 640
    for c in range(MAIN_W // step):
        main_ref[:, c * step:(c + 1) * step] = _dot(x, w_ref[:, c * step:(c + 1) * step]).astype(BF16)
    gate_ref[...] = _dot(x, w_ref[:, MAIN_W:])


def _inproj(x, w):
    t = x.shape[0]
    tm = min(512, t)
    return pl.pallas_call(
        _inproj_body,
        grid=(t // tm,),
        in_specs=[pl.BlockSpec((tm, D_MODEL), lambda i: (i, 0)),
                  pl.BlockSpec((D_MODEL, MAIN_W + GATE_W), lambda i: (0, 0))],
        out_specs=[pl.BlockSpec((tm, MAIN_W), lambda i: (i, 0)),
                   pl.BlockSpec((tm, GATE_W), lambda i: (i, 0))],
        out_shape=[jax.ShapeDtypeStruct((t, MAIN_W), BF16),
                   jax.ShapeDtypeStruct((t, GATE_W), F32)],
        compiler_params=_cparams(("parallel",)),
        name="inproj",
    )(x, w)


def _head_mean_sq(xf, bd):
    return _split_dot(xf * xf, bd)


def _rope_lanes(x, c, s, half):
    n = x.shape[-1]
    lane = lax.broadcasted_iota(jnp.int32, x.shape, 1)
    first = (lane % (2 * half)) < half
    swapped = jnp.where(first, pltpu.roll(x, n - half, 1), pltpu.roll(x, half, 1))
    return x * c + swapped * s


def _gqa_body(q_ref, k_ref, v_ref, cq_ref, sq_ref, ck_ref, sk_ref, gq_ref, gk_ref, bd_ref,
              o_ref, kdup_ref):
    i = pl.program_id(1)
    bd = bd_ref[...]

    @pl.when(i == 0)
    def _():
        kf = k_ref[...].astype(F32)
        ms = _head_mean_sq(kf, bd[:LANES, :LANES])
        kn = kf * lax.rsqrt(ms + NORM_EPS) * gk_ref[...]
        kn = _rope_lanes(kn, ck_ref[...], sk_ref[...], 16)
        lane = lax.broadcasted_iota(jnp.int32, kn.shape, 1)
        sw = pltpu.roll(kn, 64, 1)
        kdup_ref[0] = jnp.where(lane < 64, kn, sw).astype(BF16)
        kdup_ref[1] = jnp.where(lane < 64, sw, kn).astype(BF16)

    qf = q_ref[...].astype(F32)
    ms = _head_mean_sq(qf, bd)
    qn = qf * lax.rsqrt(ms + NORM_EPS) * gq_ref[...]
    cq = jnp.concatenate([cq_ref[...], cq_ref[...]], axis=1)
    sq = jnp.concatenate([sq_ref[...], sq_ref[...]], axis=1)
    qn = (_rope_lanes(qn, cq, sq, 16) * (HEAD_DIM ** -0.5)).astype(BF16)
    tq = qn.shape[0]
    v = v_ref[...]
    lane = lax.broadcasted_iota(jnp.int32, (tq, LANES), 1)
    zero = jnp.zeros((tq, LANES), BF16)
    for h in range(2):
        qh = qn[:, h * LANES:(h + 1) * LANES]
        q2 = jnp.concatenate([jnp.where(lane < 64, qh, zero), jnp.where(lane < 64, zero, qh)], axis=0)
        s = _dot_nt(q2, kdup_ref[h])
        m = jnp.max(s, axis=-1, keepdims=True)
        p = jnp.exp(s - m)
        l = jnp.sum(p, axis=-1, keepdims=True)
        o2 = _dot(p.astype(BF16), v) / l
        top, bot = o2[:tq], o2[tq:]
        if h == 0:
            oh = jnp.where(lane < 64, top, pltpu.roll(bot, 64, 1))
        else:
            oh = jnp.where(lane < 64, pltpu.roll(top, 64, 1), bot)
        o_ref[:, h * LANES:(h + 1) * LANES] = oh.astype(BF16)


def _gqa(main, tabs, gq, gk, bd, b, l):
    tq = min(256, l)
    nq = l // tq
    cq, sq = tabs
    full = lambda bi, i: (0, 0)
    return pl.pallas_call(
        _gqa_body,
        grid=(b, nq),
        in_specs=[pl.BlockSpec((tq, 256), lambda bi, i: (bi * nq + i, COL_GQ // 256)),
                  pl.BlockSpec((l, LANES), lambda bi, i: (bi, COL_GK // LANES)),
                  pl.BlockSpec((l, LANES), lambda bi, i: (bi, COL_GV // LANES)),
                  pl.BlockSpec((tq, LANES), lambda bi, i: (i, 0)),
                  pl.BlockSpec((tq, LANES), lambda bi, i: (i, 0)),
                  pl.BlockSpec((l, LANES), full),
                  pl.BlockSpec((l, LANES), full),
                  pl.BlockSpec((1, 256), full),
                  pl.BlockSpec((1, LANES), full),
                  pl.BlockSpec((256, 256), full)],
        out_specs=pl.BlockSpec((tq, 256), lambda bi, i: (bi * nq + i, 0)),
        out_shape=jax.ShapeDtypeStruct((b * l, 256), BF16),
        scratch_shapes=[pltpu.VMEM((2, l, LANES), BF16)],
        compiler_params=_cparams(("parallel", "arbitrary")),
        name="gqa",
    )(main, main, main, cq, sq, cq, sq, gq, gk, bd)


def _diff_body(lam_ref, q_ref, k_ref, v_ref, cq_ref, sq_ref, ck_ref, sk_ref, g_ref, bd_ref,
               o_ref, kr_ref):
    i = pl.program_id(1)

    @pl.when(i == 0)
    def _():
        kr_ref[...] = _rope_lanes(k_ref[...].astype(F32), ck_ref[...], sk_ref[...], 4).astype(BF16)

    lam = lam_ref[0]
    qr = _rope_lanes(q_ref[...].astype(F32), cq_ref[...], sq_ref[...], 4)
    qr = (qr * (DIFF_SUB_DIM ** -0.5)).astype(BF16)
    tq = qr.shape[0]
    lane = lax.broadcasted_iota(jnp.int32, (tq, 256), 1)
    zero = jnp.zeros((tq, 256), BF16)
    kr = kr_ref[...]
    v = v_ref[...]
    acc = jnp.zeros((tq, 256), F32)
    for h in range(4):
        q2 = jnp.concatenate([jnp.where(lane // 32 == 2 * h, qr, zero),
                              jnp.where(lane // 32 == 2 * h + 1, qr, zero)], axis=0)
        s = _dot_nt(q2, kr)
        m = jnp.max(s, axis=-1, keepdims=True)
        e = jnp.exp(s - m)
        l = jnp.sum(e, axis=-1, keepdims=True)
        l1, l2 = l[:tq], l[tq:]
        pm = (e[:tq] - (lam * l1 / l2) * e[tq:]).astype(BF16)
        o = _dot(pm, v) / l1
        acc = jnp.where(lane // 64 == h, o, acc)
    ms = _head_mean_sq(acc, bd_ref[...])
    o_ref[...] = (acc * lax.rsqrt(ms + NORM_EPS) * g_ref[...]).astype(BF16)


def _diff(main, lam, tabs, g, bd, b, l):
    tq = min(256, l)
    nq = l // tq
    c, s = tabs
    full = lambda bi, i, *_: (0, 0)
    grid_spec = pltpu.PrefetchScalarGridSpec(
        num_scalar_prefetch=1,
        grid=(b, nq),
        in_specs=[pl.BlockSpec((tq, 256), lambda bi, i, *_: (bi * nq + i, COL_DQ // 256)),
                  pl.BlockSpec((l, 256), lambda bi, i, *_: (bi, COL_DK // 256)),
                  pl.BlockSpec((l, 256), lambda bi, i, *_: (bi, COL_DV // 256)),
                  pl.BlockSpec((tq, 256), lambda bi, i, *_: (i, 0)),
                  pl.BlockSpec((tq, 256), lambda bi, i, *_: (i, 0)),
                  pl.BlockSpec((l, 256), full),
                  pl.BlockSpec((l, 256), full),
                  pl.BlockSpec((1, 256), full),
                  pl.BlockSpec((256, 256), full)],
        out_specs=pl.BlockSpec((tq, 256), lambda bi, i, *_: (bi * nq + i, 0)),
        scratch_shapes=[pltpu.VMEM((l, 256), BF16)],
    )
    return pl.pallas_call(
        _diff_body,
        grid_spec=grid_spec,
        out_shape=jax.ShapeDtypeStruct((b * l, 256), BF16),
        compiler_params=_cparams(("parallel", "arbitrary")),
        name="diffattn",
    )(lam, main, main, main, c, s, c, s, g, bd)


def _layer_norm_rows(y, g, b):
    mu = jnp.mean(y, axis=-1, keepdims=True)
    d = y - mu
    var = jnp.mean(d * d, axis=-1, keepdims=True)
    return d * lax.rsqrt(var + LN_EPS) * g + b


def _outproj_body(alpha, a_ref, b_ref, c_ref, d_ref, w_ref, x_ref, g_ref, beta_ref, rw_ref, rb_ref,
                  x1_ref, x1b_ref, idx_ref, gate_ref):
    mixed = (_dot(a_ref[...], w_ref[0:256, :]) + _dot(b_ref[...], w_ref[256:512, :])
             + _dot(c_ref[...], w_ref[512:768, :]) + _dot(d_ref[...], w_ref[768:1024, :]))
    x1 = _layer_norm_rows(alpha * x_ref[...] + mixed, g_ref[...], beta_ref[...])
    x1_ref[...] = x1
    x1b = x1.astype(BF16)
    x1b_ref[...] = x1b
    logits = _dot(x1b, rw_ref[...]) + rb_ref[...]
    lane = lax.broadcasted_iota(jnp.int32, logits.shape, 1)
    vals, idxs = [], []
    for _ in range(TOP_K):
        m = jnp.max(logits, axis=-1, keepdims=True)
        ix = jnp.min(jnp.where(logits == m, lane, LANES), axis=-1, keepdims=True)
        vals.append(m)
        idxs.append(ix)
        logits = jnp.where(lane == ix, -jnp.inf, logits)
    es = [jnp.exp(vk - vals[0]) for vk in vals]
    tot = es[0] + es[1] + es[2] + es[3]
    idx_out = jnp.zeros(logits.shape, jnp.int32)
    gate_out = jnp.zeros(logits.shape, F32)
    for k in range(TOP_K):
        idx_out = jnp.where(lane == k, idxs[k], idx_out)
        gate_out = jnp.where(lane == k, es[k] / tot, gate_out)
    idx_ref[...] = idx_out
    gate_ref[...] = gate_out


def _outproj(pieces, w, x, g, beta, rw, rb, alpha):
    t = x.shape[0]
    tm = min(512, t)
    row = lambda i: (i, 0)
    full = lambda i: (0, 0)
    return pl.pallas_call(
        functools.partial(_outproj_body, alpha),
        grid=(t // tm,),
        in_specs=[pl.BlockSpec((tm, 256), row)] * 4 + [
            pl.BlockSpec((D_MODEL, D_MODEL), full),
            pl.BlockSpec((tm, D_MODEL), row),
            pl.BlockSpec((1, D_MODEL), full),
            pl.BlockSpec((1, D_MODEL), full),
            pl.BlockSpec((D_MODEL, LANES), full),
            pl.BlockSpec((1, LANES), full)],
        out_specs=[pl.BlockSpec((tm, D_MODEL), row), pl.BlockSpec((tm, D_MODEL), row),
                   pl.BlockSpec((tm, LANES), row), pl.BlockSpec((tm, LANES), row)],
        out_shape=[jax.ShapeDtypeStruct((t, D_MODEL), F32), jax.ShapeDtypeStruct((t, D_MODEL), BF16),
                   jax.ShapeDtypeStruct((t, LANES), jnp.int32), jax.ShapeDtypeStruct((t, LANES), F32)],
        compiler_params=_cparams(("parallel",)),
        name="outproj_ln_router",
    )(*pieces, w, x, g, beta, rw, rb)


MOE_BLOCK = 512


def _ffn_body(be_ref, nb_ref, x_ref, w1_ref, b1_ref, w2_ref, b2_ref, o_ref):
    i = pl.program_id(0)

    @pl.when(i < nb_ref[0])
    def _():
        h = _dot(x_ref[...], w1_ref[...]) + b1_ref[...]
        de = h.shape[1] // 2
        glu = jnp.minimum(h[:, :de], SWIGLU_LIMIT)
        lin = jnp.clip(h[:, de:], -SWIGLU_LIMIT, SWIGLU_LIMIT)
        act = glu * jax.nn.sigmoid(SWIGLU_ALPHA * glu) * (lin + 1.0)
        o_ref[...] = (_dot(act.astype(BF16), w2_ref[...]) + b2_ref[...]).astype(o_ref.dtype)

    @pl.when(i >= nb_ref[0])
    def _():
        o_ref[...] = jnp.zeros(o_ref.shape, o_ref.dtype)


def _ffn(xb, block_expert, n_used, w1, b1, w2, b2):
    n_slots = xb.shape[0]
    n_blocks = n_slots // MOE_BLOCK
    de2 = w1.shape[2]
    grid_spec = pltpu.PrefetchScalarGridSpec(
        num_scalar_prefetch=2,
        grid=(n_blocks,),
        in_specs=[pl.BlockSpec((MOE_BLOCK, D_MODEL), lambda i, be, nb: (i, 0)),
                  pl.BlockSpec((None, D_MODEL, de2), lambda i, be, nb: (be[i], 0, 0)),
                  pl.BlockSpec((None, 1, de2), lambda i, be, nb: (be[i], 0, 0)),
                  pl.BlockSpec((None, de2 // 2, D_MODEL), lambda i, be, nb: (be[i], 0, 0)),
                  pl.BlockSpec((None, 1, D_MODEL), lambda i, be, nb: (be[i], 0, 0))],
        out_specs=pl.BlockSpec((MOE_BLOCK, D_MODEL), lambda i, be, nb: (i, 0)),
    )
    return pl.pallas_call(
        _ffn_body,
        grid_spec=grid_spec,
        out_shape=jax.ShapeDtypeStruct((n_slots, D_MODEL), BF16),
        compiler_params=_cparams(("arbitrary",)),
        name="moe_ffn",
    )(block_expert, n_used, xb, w1, b1, w2, b2)


def _ln2_body(alpha, x_ref, f_ref, g_ref, b_ref, o_ref):
    o_ref[...] = _layer_norm_rows(alpha * x_ref[...] + f_ref[...], g_ref[...], b_ref[...])


def _ln2(x, f, g, b, alpha):
    t = x.shape[0]
    tm = min(512, t)
    row = lambda i: (i, 0)
    full = lambda i: (0, 0)
    return pl.pallas_call(
        functools.partial(_ln2_body, alpha),
        grid=(t // tm,),
        in_specs=[pl.BlockSpec((tm, D_MODEL), row), pl.BlockSpec((tm, D_MODEL), row),
                  pl.BlockSpec((1, D_MODEL), full), pl.BlockSpec((1, D_MODEL), full)],
        out_specs=pl.BlockSpec((tm, D_MODEL), row),
        out_shape=jax.ShapeDtypeStruct((t, D_MODEL), F32),
        compiler_params=_cparams(("parallel",)),
        name="ln2",
    )(x, f, g, b)


def _complex_affine_combine(e1, e2):
    a1r, a1i, b1r, b1i = e1
    a2r, a2i, b2r, b2i = e2
    return (a2r * a1r - a2i * a1i, a2r * a1i + a2i * a1r,
            a2r * b1r - a2i * b1i + b2r, a2r * b1i + a2i * b1r + b2i)


def _s5_jax(u, p):
    bn, l, _ = u.shape
    uf = u.astype(F32).reshape(bn, l, SSM_GROUPS, SSM_GROUP_CH)
    y = uf * p['ssm_d']
    for direction, rev in ((0, False), (1, True)):
        lam_re = p['ssm_a_re'][direction]
        lam_im = p['ssm_a_im'][direction]
        dt = jnp.exp(p['ssm_log_dt'][direction])[:, None]
        mag = jnp.exp(lam_re * dt)
        abar_re = mag * jnp.cos(lam_im * dt)
        abar_im = mag * jnp.sin(lam_im * dt)
        den = lam_re * lam_re + lam_im * lam_im
        coef_re = ((abar_re - 1.0) * lam_re + abar_im * lam_im) / den
        coef_im = (abar_im * lam_re - (abar_re - 1.0) * lam_im) / den
        br = p['ssm_b_re'][direction]
        bi = p['ssm_b_im'][direction]
        bbar_re = coef_re[..., None] * br - coef_im[..., None] * bi
        bbar_im = coef_re[..., None] * bi + coef_im[..., None] * br
        bu_re = jnp.einsum('blgh,gph->blgp', uf, bbar_re)
        bu_im = jnp.einsum('blgh,gph->blgp', uf, bbar_im)
        elems = (jnp.broadcast_to(abar_re, bu_re.shape), jnp.broadcast_to(abar_im, bu_re.shape), bu_re, bu_im)
        _, _, s_re, s_im = lax.associative_scan(_complex_affine_combine, elems, reverse=rev, axis=1)
        y = (y + jnp.einsum('blgp,ghp->blgh', s_re, p['ssm_c_re'][direction])
             - jnp.einsum('blgp,ghp->blgh', s_im, p['ssm_c_im'][direction]))
    y = jax.nn.gelu(y.reshape(bn, l, GROUP_WIDTH))
    out = y * jax.nn.sigmoid(y @ p['ssm_glu_w'] + p['ssm_glu_b'])
    return out.astype(BF16)


def _l2norm(x):
    return x * lax.rsqrt(jnp.sum(x * x, axis=-1, keepdims=True) + NORM_EPS)


def _gated_delta_chunked(q, k, v, g, beta):
    bn, l, h, dk = q.shape
    c = GDN_CHUNK
    n = l // c

    def chunks(t):
        return jnp.transpose(t.reshape(bn, n, c, h, t.shape[-1]), (0, 3, 1, 2, 4))

    def chunks_h(t):
        return jnp.transpose(t.reshape(bn, n, c, h), (0, 3, 1, 2))

    q = chunks(q) * (dk ** -0.5)
    k = chunks(k)
    v = chunks(v)
    beta = chunks_h(beta)
    g = jnp.cumsum(chunks_h(g), axis=-1)
    causal = jnp.tril(jnp.ones((c, c), dtype=bool))
    strict = jnp.tril(jnp.ones((c, c), dtype=bool), k=-1)
    decay = jnp.exp(jnp.where(causal, g[..., :, None] - g[..., None, :], -jnp.inf))
    k_beta = k * beta[..., None]
    lower = jnp.where(strict, jnp.einsum('bhncd,bhnsd->bhncs', k_beta, k) * decay, 0.0)
    t_mat = lower + jnp.eye(c, dtype=lower.dtype)
    u = lax.linalg.triangular_solve(t_mat, v * beta[..., None], left_side=True, lower=True, unit_diagonal=True)
    w = lax.linalg.triangular_solve(t_mat, k_beta * jnp.exp(g)[..., None], left_side=True, lower=True,
                                    unit_diagonal=True)
    intra = jnp.einsum('bhncd,bhnsd->bhncs', q, k) * decay

    def step(s, xs):
        q_c, k_c, u_c, w_c, g_c, a_c = xs
        v_new = u_c - jnp.einsum('bhck,bhkv->bhcv', w_c, s)
        o_c = (jnp.einsum('bhck,bhkv->bhcv', q_c * jnp.exp(g_c)[..., None], s)
               + jnp.einsum('bhcs,bhsv->bhcv', a_c, v_new))
        g_last = g_c[..., -1:]
        s = (s * jnp.exp(g_last)[..., None]
             + jnp.einsum('bhck,bhcv->bhkv', k_c * jnp.exp(g_last - g_c)[..., None], v_new))
        return s, o_c

    xs = tuple(jnp.moveaxis(t, 2, 0) for t in (q, k, u, w, g, intra))
    s0 = jnp.zeros((bn, h, dk, v.shape[-1]), F32)
    _, o = lax.scan(step, s0, xs)
    return jnp.transpose(o, (1, 0, 3, 2, 4)).reshape(bn, l, h, v.shape[-1])


def _gdn_jax(cq, ck, cv, cz, ca, cb, p):
    bn, l, _ = cq.shape
    x = jnp.concatenate([cq, ck, cv], axis=-1).astype(F32)
    w = p['gdn_conv_w']
    kw = w.shape[0]
    conv = lax.conv_general_dilated(x, w[:, None, :], window_strides=(1,), padding=[(kw // 2, kw // 2)],
                                    dimension_numbers=('NWC', 'WIO', 'NWC'), feature_group_count=x.shape[-1])
    qkv = jax.nn.silu(conv)
    q, k, v = jnp.split(qkv, 3, axis=-1)
    q = _l2norm(q.reshape(bn, l, GDN_HEADS, HEAD_DIM))
    k = _l2norm(k.reshape(bn, l, GDN_HEADS, HEAD_DIM))
    v = v.reshape(bn, l, GDN_HEADS, HEAD_DIM)
    ca = ca.reshape(bn, l, 2, GDN_HEADS)
    cb = cb.reshape(bn, l, 2, GDN_HEADS)
    g = -jnp.exp(p['gdn_a_log']) * jax.nn.softplus(ca + p['gdn_dt_bias'])
    beta = jax.nn.sigmoid(cb)
    flip = lambda t: t[:, ::-1]
    o_fwd = _gated_delta_chunked(q, k, v, g[:, :, 0], beta[:, :, 0])
    o_bwd = flip(_gated_delta_chunked(flip(q), flip(k), flip(v), flip(g[:, :, 1]), flip(beta[:, :, 1])))
    gate = jax.nn.silu(cz.astype(F32).reshape(bn, l, GDN_HEADS, HEAD_DIM))
    o = o_fwd + o_bwd
    o = o * lax.rsqrt(jnp.mean(o * o, axis=-1, keepdims=True) + NORM_EPS) * p['gdn_o_norm'] * gate
    return o.reshape(bn, l, GDN_HEADS * HEAD_DIM).astype(BF16)


def _moe(x1, x1b, top_idx, gates, p):
    t = x1.shape[0]
    m = t * TOP_K
    flat_e = top_idx.reshape(-1)
    order = jnp.argsort(flat_e)
    sorted_e = flat_e[order]
    counts = jnp.bincount(flat_e, length=N_EXPERTS)
    padded = (counts + MOE_BLOCK - 1) // MOE_BLOCK * MOE_BLOCK
    group_start = jnp.cumsum(counts) - counts
    padded_end = jnp.cumsum(padded)
    padded_start = padded_end - padded
    dest = (padded_start[sorted_e] + jnp.arange(m) - group_start[sorted_e]).astype(jnp.int32)
    n_blocks = -(-m // MOE_BLOCK) + N_EXPERTS
    n_slots = n_blocks * MOE_BLOCK
    slot_token = jnp.full((n_slots,), t, jnp.int32).at[dest].set((order // TOP_K).astype(jnp.int32))
    block_expert = jnp.minimum(jnp.searchsorted(padded_end, jnp.arange(n_blocks) * MOE_BLOCK, side='right'),
                               N_EXPERTS - 1).astype(jnp.int32)
    n_used = (padded_end[-1] // MOE_BLOCK).astype(jnp.int32).reshape(1)
    x_pad = jnp.concatenate([x1b, jnp.zeros((1, D_MODEL), BF16)], axis=0)
    xb = x_pad[slot_token]
    yb = _ffn(xb, block_expert, n_used, p['moe_w1'], p['moe_b1'], p['moe_w2'], p['moe_b2'])
    dest_tk = jnp.zeros((m,), jnp.int32).at[order].set(dest).reshape(t, TOP_K)
    return jnp.sum(yb[dest_tk].astype(F32) * gates[..., None], axis=1)


def _block_diag_mean(width, group):
    idx = np.arange(width)
    return jnp.asarray((idx[:, None] // group == idx[None, :] // group) / group, BF16)


def _axial_tables(l):
    rows = l // GRID_W
    row_pos = np.repeat(np.arange(rows), GRID_W).astype(np.float64)
    col_pos = np.tile(np.arange(GRID_W), rows).astype(np.float64)
    lane = np.arange(LANES)
    d = lane % HEAD_DIM
    e = d % 32
    f = e % 16
    inv = (AXIAL_THETA ** (-(np.arange(0, 32, 2, dtype=np.float32)) / 32)).astype(np.float32)
    pos = np.where((d // 32)[None, :] == 0, row_pos[:, None], col_pos[:, None]).astype(np.float32)
    ang = pos * inv[f][None, :]
    sign = np.where(e < 16, -1.0, 1.0)[None, :]
    return jnp.asarray(np.cos(ang), F32), jnp.asarray(np.sin(ang) * sign, F32)


def _diff_tables(l):
    lane = np.arange(256)
    e = lane % 32
    f = e % 4
    inv = (ROPE_THETA ** (-(np.arange(0, PARTIAL_ROPE_DIMS, 2, dtype=np.float32)) / PARTIAL_ROPE_DIMS)).astype(np.float32)
    ang = np.arange(l, dtype=np.float32)[:, None] * inv[f][None, :]
    roped = (e < PARTIAL_ROPE_DIMS)[None, :]
    sign = np.where(e < 4, -1.0, 1.0)[None, :]
    c = np.where(roped, np.cos(ang), 1.0)
    s = np.where(roped, np.sin(ang) * sign, 0.0)
    return jnp.asarray(c, F32), jnp.asarray(s, F32)


def _prep_layer(params, layer):
    p = {name: arr[layer] for name, arr in params.items()}
    w_in = p['w_in']
    w_in = jnp.concatenate([w_in[:, :ORIG_CA], w_in[:, ORIG_CA + 16:], w_in[:, ORIG_CA:ORIG_CA + 16],
                            jnp.zeros((D_MODEL, GATE_W - 16), F32)], axis=1)
    q = dict(p)
    q['w_in'] = w_in.astype(BF16)
    q['w_out'] = p['w_out'].astype(BF16)
    q['router_w'] = jnp.concatenate([p['router_w'], jnp.zeros((D_MODEL, LANES - N_EXPERTS), F32)], axis=1).astype(BF16)
    q['router_b'] = jnp.concatenate([p['router_b'], jnp.full((LANES - N_EXPERTS,), -1e30, F32)])[None, :]
    w1 = p['moe_w1']
    q['moe_w1'] = jnp.concatenate([w1[:, :, 0::2], w1[:, :, 1::2]], axis=2).astype(BF16)
    b1 = p['moe_b1']
    q['moe_b1'] = jnp.concatenate([b1[:, 0::2], b1[:, 1::2]], axis=1)[:, None, :]
    q['moe_w2'] = p['moe_w2'].astype(BF16)
    q['moe_b2'] = p['moe_b2'][:, None, :]
    for name in ('ln1_g', 'ln1_b', 'ln2_g', 'ln2_b'):
        q[name] = p[name][None, :]
    q['gqa_q_norm'] = jnp.tile(p['gqa_q_norm'], 4)[None, :]
    q['gqa_k_norm'] = jnp.tile(p['gqa_k_norm'], 2)[None, :]
    lambda_init = 0.8 - 0.6 * math.exp(-0.3 * layer)
    q['diff_subln'] = (jnp.tile(p['diff_subln'], 4) * (1.0 - lambda_init))[None, :]
    lam = (jnp.exp(jnp.sum(p['diff_lambda_q1'] * p['diff_lambda_k1']))
           - jnp.exp(jnp.sum(p['diff_lambda_q2'] * p['diff_lambda_k2'])) + lambda_init)
    q['diff_lam'] = lam.reshape(1).astype(F32)
    return q


def _mixers(main, gate, q, tabs, b, l):
    t = b * l
    m3 = main.reshape(b, l, MAIN_W)
    a_out = _s5_jax(m3[..., COL_U:COL_U + 256], q).reshape(t, 256)
    b_out = _gqa(main, tabs['axial'], q['gqa_q_norm'], q['gqa_k_norm'], tabs['bd64'], b, l)
    g3 = gate.reshape(b, l, GATE_W)
    c_out = _gdn_jax(m3[..., COL_CQ:COL_CQ + 256], m3[..., COL_CK:COL_CK + 256], m3[..., COL_CV:COL_CV + 256],
                     m3[..., COL_CZ:COL_CZ + 256], g3[..., 0:8], g3[..., 8:16], q).reshape(t, 256)
    d_out = _diff(main, q['diff_lam'], tabs['diff'], q['diff_subln'], tabs['bd64'], b, l)
    return a_out, b_out, c_out, d_out


def _layer(x, q, tabs, b, l, alpha):
    main, gate = _inproj(x, q['w_in'])
    pieces = _mixers(main, gate, q, tabs, b, l)
    x1, x1b, idx, gates = _outproj(pieces, q['w_out'], x, q['ln1_g'], q['ln1_b'],
                                   q['router_w'], q['router_b'], alpha)
    ffn = _moe(x1, x1b, idx[:, :TOP_K], gates[:, :TOP_K], q)
    return _ln2(x1, ffn, q['ln2_g'], q['ln2_b'], alpha)


def _trunk(x, layers, alpha):
    b, l, _ = x.shape
    tabs = {'axial': _axial_tables(l), 'diff': _diff_tables(l), 'bd64': _block_diag_mean(256, HEAD_DIM)}
    h = x.reshape(b * l, D_MODEL)
    for q in layers:
        h = _layer(h, q, tabs, b, l, alpha)
    return h.reshape(b, l, D_MODEL)


def kernel(x_prompt, x_sample, w_in, w_out, ssm_a_re, ssm_a_im, ssm_log_dt, ssm_b_re, ssm_b_im, ssm_c_re, ssm_c_im, ssm_d, ssm_glu_w, ssm_glu_b, gqa_q_norm, gqa_k_norm, gdn_conv_w, gdn_a_log, gdn_dt_bias, gdn_o_norm, diff_lambda_q1, diff_lambda_k1, diff_lambda_q2, diff_lambda_k2, diff_subln, router_w, router_b, moe_w1, moe_b1, moe_w2, moe_b2, ln1_g, ln1_b, ln2_g, ln2_b):
    params = {
        'w_in': w_in, 'w_out': w_out,
        'ssm_a_re': ssm_a_re, 'ssm_a_im': ssm_a_im, 'ssm_log_dt': ssm_log_dt,
        'ssm_b_re': ssm_b_re, 'ssm_b_im': ssm_b_im, 'ssm_c_re': ssm_c_re, 'ssm_c_im': ssm_c_im,
        'ssm_d': ssm_d, 'ssm_glu_w': ssm_glu_w, 'ssm_glu_b': ssm_glu_b,
        'gqa_q_norm': gqa_q_norm, 'gqa_k_norm': gqa_k_norm,
        'gdn_conv_w': gdn_conv_w, 'gdn_a_log': gdn_a_log, 'gdn_dt_bias': gdn_dt_bias, 'gdn_o_norm': gdn_o_norm,
        'diff_lambda_q1': diff_lambda_q1, 'diff_lambda_k1': diff_lambda_k1,
        'diff_lambda_q2': diff_lambda_q2, 'diff_lambda_k2': diff_lambda_k2, 'diff_subln': diff_subln,
        'router_w': router_w, 'router_b': router_b,
        'moe_w1': moe_w1, 'moe_b1': moe_b1, 'moe_w2': moe_w2, 'moe_b2': moe_b2,
        'ln1_g': ln1_g, 'ln1_b': ln1_b, 'ln2_g': ln2_g, 'ln2_b': ln2_b,
    }
    depth = w_in.shape[0]
    alpha = (2.0 * depth) ** 0.25
    layers = [_prep_layer(params, layer) for layer in range(depth)]
    return (_trunk(x_prompt, layers, alpha), _trunk(x_sample, layers, alpha))
```

```python
import functools
import math

import jax
import jax.numpy as jnp
import numpy as np
from jax import lax
from jax.experimental import pallas as pl
from jax.experimental.pallas import tpu as pltpu

F32 = jnp.float32
BF16 = jnp.bfloat16

D_MODEL = 1024
GROUP_WIDTH = 256
HEAD_DIM = 64
SSM_GROUPS = 16
SSM_GROUP_CH = 16
SSM_STATE = 64
GDN_HEADS = 4
GDN_CHUNK = 64
DIFF_SUB_DIM = 32
PARTIAL_ROPE_DIMS = 8
AXIAL_THETA = 10000.0
ROPE_THETA = 500000.0
GRID_W = 64
N_EXPERTS = 32
TOP_K = 4
SWIGLU_ALPHA = 1.702
SWIGLU_LIMIT = 7.0
NORM_EPS = 1e-6
LN_EPS = 1e-5

LANES = 128
VMEM_LIMIT = 56 * 1024 * 1024

COL_U, COL_GQ, COL_GK, COL_GV = 0, 256, 512, 640
COL_CQ, COL_CK, COL_CV, COL_CZ = 768, 1024, 1280, 1536
COL_DQ, COL_DK, COL_DV = 1792, 2048, 2304
MAIN_W = 2560
GATE_W = 128
ORIG_CA = 1792


def _cparams(sem):
    return pltpu.CompilerParams(dimension_semantics=sem, vmem_limit_bytes=VMEM_LIMIT)


def _dot(a, b):
    return jnp.dot(a, b, preferred_element_type=F32)


def _dot_nt(a, b):
    return lax.dot_general(a, b, (((1,), (1,)), ((), ())), preferred_element_type=F32)


def _split_dot(x, m):
    hi = x.astype(BF16)
    lo = (x - hi.astype(F32)).astype(BF16)
    return _dot(hi, m) + _dot(lo, m)


def _inproj_body(x_ref, w_ref, main_ref, gate_ref):
    x = x_ref[...].astype(BF16)
    step = 640
    for c in range(MAIN_W // step):
        main_ref[:, c * step:(c + 1) * step] = _dot(x, w_ref[:, c * step:(c + 1) * step]).astype(BF16)
    gate_ref[...] = _dot(x, w_ref[:, MAIN_W:])


def _inproj(x, w):
    t = x.shape[0]
    tm = min(512, t)
    return pl.pallas_call(
        _inproj_body,
        grid=(t // tm,),
        in_specs=[pl.BlockSpec((tm, D_MODEL), lambda i: (i, 0)),
                  pl.BlockSpec((D_MODEL, MAIN_W + GATE_W), lambda i: (0, 0))],
        out_specs=[pl.BlockSpec((tm, MAIN_W), lambda i: (i, 0)),
                   pl.BlockSpec((tm, GATE_W), lambda i: (i, 0))],
        out_shape=[jax.ShapeDtypeStruct((t, MAIN_W), BF16),
                   jax.ShapeDtypeStruct((t, GATE_W), F32)],
        compiler_params=_cparams(("parallel",)),
        name="inproj",
    )(x, w)


def _head_mean_sq(xf, bd):
    return _split_dot(xf * xf, bd)


def _rope_lanes(x, c, s, half):
    n = x.shape[-1]
    lane = lax.broadcasted_iota(jnp.int32, x.shape, 1)
    first = (lane % (2 * half)) < half
    swapped = jnp.where(first, pltpu.roll(x, n - half, 1), pltpu.roll(x, half, 1))
    return x * c + swapped * s


def _gqa_body(q_ref, k_ref, v_ref, cq_ref, sq_ref, ck_ref, sk_ref, gq_ref, gk_ref, bd_ref,
              o_ref, kdup_ref):
    i = pl.program_id(1)
    bd = bd_ref[...]

    @pl.when(i == 0)
    def _():
        kf = k_ref[...].astype(F32)
        ms = _head_mean_sq(kf, bd[:LANES, :LANES])
        kn = kf * lax.rsqrt(ms + NORM_EPS) * gk_ref[...]
        kn = _rope_lanes(kn, ck_ref[...], sk_ref[...], 16)
        lane = lax.broadcasted_iota(jnp.int32, kn.shape, 1)
        sw = pltpu.roll(kn, 64, 1)
        kdup_ref[0] = jnp.where(lane < 64, kn, sw).astype(BF16)
        kdup_ref[1] = jnp.where(lane < 64, sw, kn).astype(BF16)

    qf = q_ref[...].astype(F32)
    ms = _head_mean_sq(qf, bd)
    qn = qf * lax.rsqrt(ms + NORM_EPS) * gq_ref[...]
    cq = jnp.concatenate([cq_ref[...], cq_ref[...]], axis=1)
    sq = jnp.concatenate([sq_ref[...], sq_ref[...]], axis=1)
    qn = (_rope_lanes(qn, cq, sq, 16) * (HEAD_DIM ** -0.5)).astype(BF16)
    tq = qn.shape[0]
    v = v_ref[...]
    lane = lax.broadcasted_iota(jnp.int32, (tq, LANES), 1)
    zero = jnp.zeros((tq, LANES), BF16)
    for h in range(2):
        qh = qn[:, h * LANES:(h + 1) * LANES]
        q2 = jnp.concatenate([jnp.where(lane < 64, qh, zero), jnp.where(lane < 64, zero, qh)], axis=0)
        s = _dot_nt(q2, kdup_ref[h])
        m = jnp.max(s, axis=-1, keepdims=True)
        p = jnp.exp(s - m)
        l = jnp.sum(p, axis=-1, keepdims=True)
        o2 = _dot(p.astype(BF16), v) / l
        top, bot = o2[:tq], o2[tq:]
        if h == 0:
            oh = jnp.where(lane < 64, top, pltpu.roll(bot, 64, 1))
        else:
            oh = jnp.where(lane < 64, pltpu.roll(top, 64, 1), bot)
        o_ref[:, h * LANES:(h + 1) * LANES] = oh.astype(BF16)


def _gqa(main, tabs, gq, gk, bd, b, l):
    tq = min(256, l)
    nq = l // tq
    cq, sq = tabs
    full = lambda bi, i: (0, 0)
    return pl.pallas_call(
        _gqa_body,
        grid=(b, nq),
        in_specs=[pl.BlockSpec((tq, 256), lambda bi, i: (bi * nq + i, COL_GQ // 256)),
                  pl.BlockSpec((l, LANES), lambda bi, i: (bi, COL_GK // LANES)),
                  pl.BlockSpec((l, LANES), lambda bi, i: (bi, COL_GV // LANES)),
                  pl.BlockSpec((tq, LANES), lambda bi, i: (i, 0)),
                  pl.BlockSpec((tq, LANES), lambda bi, i: (i, 0)),
                  pl.BlockSpec((l, LANES), full),
                  pl.BlockSpec((l, LANES), full),
                  pl.BlockSpec((1, 256), full),
                  pl.BlockSpec((1, LANES), full),
                  pl.BlockSpec((256, 256), full)],
        out_specs=pl.BlockSpec((tq, 256), lambda bi, i: (bi * nq + i, 0)),
        out_shape=jax.ShapeDtypeStruct((b * l, 256), BF16),
        scratch_shapes=[pltpu.VMEM((2, l, LANES), BF16)],
        compiler_params=_cparams(("parallel", "arbitrary")),
        name="gqa",
    )(main, main, main, cq, sq, cq, sq, gq, gk, bd)


def _diff_body(lam_ref, q_ref, k_ref, v_ref, cq_ref, sq_ref, ck_ref, sk_ref, g_ref, bd_ref,
               o_ref, kr_ref):
    i = pl.program_id(1)

    @pl.when(i == 0)
    def _():
        kr_ref[...] = _rope_lanes(k_ref[...].astype(F32), ck_ref[...], sk_ref[...], 4).astype(BF16)

    lam = lam_ref[0]
    qr = _rope_lanes(q_ref[...].astype(F32), cq_ref[...], sq_ref[...], 4)
    qr = (qr * (DIFF_SUB_DIM ** -0.5)).astype(BF16)
    tq = qr.shape[0]
    lane = lax.broadcasted_iota(jnp.int32, (tq, 256), 1)
    zero = jnp.zeros((tq, 256), BF16)
    kr = kr_ref[...]
    v = v_ref[...]
    acc = jnp.zeros((tq, 256), F32)
    for h in range(4):
        q2 = jnp.concatenate([jnp.where(lane // 32 == 2 * h, qr, zero),
                              jnp.where(lane // 32 == 2 * h + 1, qr, zero)], axis=0)
        s = _dot_nt(q2, kr)
        m = jnp.max(s, axis=-1, keepdims=True)
        e = jnp.exp(s - m)
        l = jnp.sum(e, axis=-1, keepdims=True)
        l1, l2 = l[:tq], l[tq:]
        pm = (e[:tq] - (lam * l1 / l2) * e[tq:]).astype(BF16)
        o = _dot(pm, v) / l1
        acc = jnp.where(lane // 64 == h, o, acc)
    ms = _head_mean_sq(acc, bd_ref[...])
    o_ref[...] = (acc * lax.rsqrt(ms + NORM_EPS) * g_ref[...]).astype(BF16)


def _diff(main, lam, tabs, g, bd, b, l):
    tq = min(256, l)
    nq = l // tq
    c, s = tabs
    full = lambda bi, i, *_: (0, 0)
    grid_spec = pltpu.PrefetchScalarGridSpec(
        num_scalar_prefetch=1,
        grid=(b, nq),
        in_specs=[pl.BlockSpec((tq, 256), lambda bi, i, *_: (bi * nq + i, COL_DQ // 256)),
                  pl.BlockSpec((l, 256), lambda bi, i, *_: (bi, COL_DK // 256)),
                  pl.BlockSpec((l, 256), lambda bi, i, *_: (bi, COL_DV // 256)),
                  pl.BlockSpec((tq, 256), lambda bi, i, *_: (i, 0)),
                  pl.BlockSpec((tq, 256), lambda bi, i, *_: (i, 0)),
                  pl.BlockSpec((l, 256), full),
                  pl.BlockSpec((l, 256), full),
                  pl.BlockSpec((1, 256), full),
                  pl.BlockSpec((256, 256), full)],
        out_specs=pl.BlockSpec((tq, 256), lambda bi, i, *_: (bi * nq + i, 0)),
        scratch_shapes=[pltpu.VMEM((l, 256), BF16)],
    )
    return pl.pallas_call(
        _diff_body,
        grid_spec=grid_spec,
        out_shape=jax.ShapeDtypeStruct((b * l, 256), BF16),
        compiler_params=_cparams(("parallel", "arbitrary")),
        name="diffattn",
    )(lam, main, main, main, c, s, c, s, g, bd)


def _layer_norm_rows(y, g, b):
    mu = jnp.mean(y, axis=-1, keepdims=True)
    d = y - mu
    var = jnp.mean(d * d, axis=-1, keepdims=True)
    return d * lax.rsqrt(var + LN_EPS) * g + b


def _outproj_body(alpha, a_ref, b_ref, c_ref, d_ref, w_ref, x_ref, g_ref, beta_ref, rw_ref, rb_ref,
                  x1_ref, x1b_ref, idx_ref, gate_ref):
    mixed = (_dot(a_ref[...], w_ref[0:256, :]) + _dot(b_ref[...], w_ref[256:512, :])
             + _dot(c_ref[...], w_ref[512:768, :]) + _dot(d_ref[...], w_ref[768:1024, :]))
    x1 = _layer_norm_rows(alpha * x_ref[...] + mixed, g_ref[...], beta_ref[...])
    x1_ref[...] = x1
    x1b = x1.astype(BF16)
    x1b_ref[...] = x1b
    logits = _dot(x1b, rw_ref[...]) + rb_ref[...]
    lane = lax.broadcasted_iota(jnp.int32, logits.shape, 1)
    vals, idxs = [], []
    for _ in range(TOP_K):
        m = jnp.max(logits, axis=-1, keepdims=True)
        ix = jnp.min(jnp.where(logits == m, lane, LANES), axis=-1, keepdims=True)
        vals.append(m)
        idxs.append(ix)
        logits = jnp.where(lane == ix, -jnp.inf, logits)
    es = [jnp.exp(vk - vals[0]) for vk in vals]
    tot = es[0] + es[1] + es[2] + es[3]
    idx_out = jnp.zeros(logits.shape, jnp.int32)
    gate_out = jnp.zeros(logits.shape, F32)
    for k in range(TOP_K):
        idx_out = jnp.where(lane == k, idxs[k], idx_out)
        gate_out = jnp.where(lane == k, es[k] / tot, gate_out)
    idx_ref[...] = idx_out
    gate_ref[...] = gate_out


def _outproj(pieces, w, x, g, beta, rw, rb, alpha):
    t = x.shape[0]
    tm = min(512, t)
    row = lambda i: (i, 0)
    full = lambda i: (0, 0)
    return pl.pallas_call(
        functools.partial(_outproj_body, alpha),
        grid=(t // tm,),
        in_specs=[pl.BlockSpec((tm, 256), row)] * 4 + [
            pl.BlockSpec((D_MODEL, D_MODEL), full),
            pl.BlockSpec((tm, D_MODEL), row),
            pl.BlockSpec((1, D_MODEL), full),
            pl.BlockSpec((1, D_MODEL), full),
            pl.BlockSpec((D_MODEL, LANES), full),
            pl.BlockSpec((1, LANES), full)],
        out_specs=[pl.BlockSpec((tm, D_MODEL), row), pl.BlockSpec((tm, D_MODEL), row),
                   pl.BlockSpec((tm, LANES), row), pl.BlockSpec((tm, LANES), row)],
        out_shape=[jax.ShapeDtypeStruct((t, D_MODEL), F32), jax.ShapeDtypeStruct((t, D_MODEL), BF16),
                   jax.ShapeDtypeStruct((t, LANES), jnp.int32), jax.ShapeDtypeStruct((t, LANES), F32)],
        compiler_params=_cparams(("parallel",)),
        name="outproj_ln_router",
    )(*pieces, w, x, g, beta, rw, rb)


MOE_BLOCK = 512


def _ffn_body(be_ref, nb_ref, x_ref, w1_ref, b1_ref, w2_ref, b2_ref, o_ref):
    i = pl.program_id(0)

    @pl.when(i < nb_ref[0])
    def _():
        h = _dot(x_ref[...], w1_ref[...]) + b1_ref[...]
        de = h.shape[1] // 2
        glu = jnp.minimum(h[:, :de], SWIGLU_LIMIT)
        lin = jnp.clip(h[:, de:], -SWIGLU_LIMIT, SWIGLU_LIMIT)
        act = glu * jax.nn.sigmoid(SWIGLU_ALPHA * glu) * (lin + 1.0)
        o_ref[...] = (_dot(act.astype(BF16), w2_ref[...]) + b2_ref[...]).astype(o_ref.dtype)

    @pl.when(i >= nb_ref[0])
    def _():
        o_ref[...] = jnp.zeros(o_ref.shape, o_ref.dtype)


def _ffn(xb, block_expert, n_used, w1, b1, w2, b2):
    n_slots = xb.shape[0]
    n_blocks = n_slots // MOE_BLOCK
    de2 = w1.shape[2]
    grid_spec = pltpu.PrefetchScalarGridSpec(
        num_scalar_prefetch=2,
        grid=(n_blocks,),
        in_specs=[pl.BlockSpec((MOE_BLOCK, D_MODEL), lambda i, be, nb: (i, 0)),
                  pl.BlockSpec((None, D_MODEL, de2), lambda i, be, nb: (be[i], 0, 0)),
                  pl.BlockSpec((None, 1, de2), lambda i, be, nb: (be[i], 0, 0)),
                  pl.BlockSpec((None, de2 // 2, D_MODEL), lambda i, be, nb: (be[i], 0, 0)),
                  pl.BlockSpec((None, 1, D_MODEL), lambda i, be, nb: (be[i], 0, 0))],
        out_specs=pl.BlockSpec((MOE_BLOCK, D_MODEL), lambda i, be, nb: (i, 0)),
    )
    return pl.pallas_call(
        _ffn_body,
        grid_spec=grid_spec,
        out_shape=jax.ShapeDtypeStruct((n_slots, D_MODEL), BF16),
        compiler_params=_cparams(("arbitrary",)),
        name="moe_ffn",
    )(block_expert, n_used, xb, w1, b1, w2, b2)


def _ln2_body(alpha, x_ref, f_ref, g_ref, b_ref, o_ref):
    o_ref[...] = _layer_norm_rows(alpha * x_ref[...] + f_ref[...], g_ref[...], b_ref[...])


def _ln2(x, f, g, b, alpha):
    t = x.shape[0]
    tm = min(512, t)
    row = lambda i: (i, 0)
    full = lambda i: (0, 0)
    return pl.pallas_call(
        functools.partial(_ln2_body, alpha),
        grid=(t // tm,),
        in_specs=[pl.BlockSpec((tm, D_MODEL), row), pl.BlockSpec((tm, D_MODEL), row),
                  pl.BlockSpec((1, D_MODEL), full), pl.BlockSpec((1, D_MODEL), full)],
        out_specs=pl.BlockSpec((tm, D_MODEL), row),
        out_shape=jax.ShapeDtypeStruct((t, D_MODEL), F32),
        compiler_params=_cparams(("parallel",)),
        name="ln2",
    )(x, f, g, b)


S5_CHUNK = 16


def _s5_body(nb, n1, u_ref, t_ref, w_ref, v_ref, lam_ref, o_ref, h_ref, e_ref):
    u0 = u_ref[0]
    u1 = u_ref[1]
    h_ref[...] = _dot(jnp.concatenate([u0, u1], axis=1), w_ref[...])
    lam = lam_ref[...]
    zero = jnp.zeros((nb, LANES), F32)

    def scan(base, lr, li, reverse):
        def step(k, carry):
            er, ei = carry
            n = (n1 - 1 - k) if reverse else k
            r0 = n * nb
            if nb % 8 == 0:
                r0 = pl.multiple_of(r0, 8)
            rows = pl.ds(r0, nb)
            e_ref[rows, base:base + LANES] = er
            e_ref[rows, base + LANES:base + 2 * LANES] = ei
            hr = h_ref[rows, base:base + LANES]
            hi = h_ref[rows, base + LANES:base + 2 * LANES]
            return lr * er - li * ei + hr, lr * ei + li * er + hi
        lax.fori_loop(0, n1, step, (zero, zero))

    scan(0, lam[0:1], lam[1:2], False)
    scan(2 * LANES, lam[2:3], lam[3:4], True)
    yi = _dot(e_ref[...].astype(BF16), v_ref[...])
    o_ref[0] = (_dot(u0, t_ref[0]) + yi[:, :256]).astype(BF16)
    o_ref[1] = (_dot(u1, t_ref[1]) + yi[:, 256:]).astype(BF16)


def _s5_core(u, sp, nb, n1):
    g, r, _ = u.shape
    pair = lambda i: (i, 0, 0)
    return pl.pallas_call(
        functools.partial(_s5_body, nb, n1),
        grid=(g // 2,),
        in_specs=[pl.BlockSpec((2, r, 256), pair),
                  pl.BlockSpec((2, 256, 256), pair),
                  pl.BlockSpec((None, 512, 512), pair),
                  pl.BlockSpec((None, 512, 512), pair),
                  pl.BlockSpec((None, 8, LANES), pair)],
        out_specs=pl.BlockSpec((2, r, 256), pair),
        out_shape=jax.ShapeDtypeStruct((g, r, 256), BF16),
        scratch_shapes=[pltpu.VMEM((r, 512), F32), pltpu.VMEM((r, 512), F32)],
        compiler_params=_cparams(("parallel",)),
        name="s5_core",
    )(u, sp['t'], sp['w'], sp['v'], sp['lam'])


def _s5_out_body(y_ref, w_ref, b_ref, o_ref):
    y = jax.nn.gelu(y_ref[...].astype(F32))
    z = _dot(y.astype(BF16), w_ref[...]) + b_ref[...]
    o_ref[...] = (y * jax.nn.sigmoid(z)).astype(BF16)


def _s5_out(y, w, bias):
    t = y.shape[0]
    tm = min(1024, t)
    row = lambda i: (i, 0)
    full = lambda i: (0, 0)
    return pl.pallas_call(
        _s5_out_body,
        grid=(t // tm,),
        in_specs=[pl.BlockSpec((tm, 256), row), pl.BlockSpec((256, 256), full), pl.BlockSpec((1, 256), full)],
        out_specs=pl.BlockSpec((tm, 256), row),
        out_shape=jax.ShapeDtypeStruct((t, 256), BF16),
        compiler_params=_cparams(("parallel",)),
        name="s5_out",
    )(y, w, bias)


def _s5(main, q, b, l):
    n1 = l // S5_CHUNK
    u = main[:, COL_U:COL_U + 256].reshape(b, n1, S5_CHUNK, SSM_GROUPS, SSM_GROUP_CH)
    u = u.transpose(3, 1, 0, 2, 4).reshape(SSM_GROUPS, n1 * b, 256)
    y = _s5_core(u, q['s5'], b, n1)
    y = y.reshape(SSM_GROUPS, n1, b, S5_CHUNK, SSM_GROUP_CH).transpose(2, 1, 3, 0, 4).reshape(b * l, 256)
    return _s5_out(y, q['ssm_glu_w'], q['ssm_glu_b'])


def _s5_prep(p):
    hp = lax.Precision.HIGHEST
    c = S5_CHUNK
    tau = jnp.arange(c + 1, dtype=F32)[:, None, None]
    ks, ws, vs, lams = [], [], [], []
    for d in (0, 1):
        lam_re = p['ssm_a_re'][d]
        lam_im = p['ssm_a_im'][d]
        dt = jnp.exp(p['ssm_log_dt'][d])[:, None]
        mag = jnp.exp(lam_re * dt)
        abar_re = mag * jnp.cos(lam_im * dt)
        abar_im = mag * jnp.sin(lam_im * dt)
        den = lam_re * lam_re + lam_im * lam_im
        coef_re = ((abar_re - 1.0) * lam_re + abar_im * lam_im) / den
        coef_im = (abar_im * lam_re - (abar_re - 1.0) * lam_im) / den
        br, bi = p['ssm_b_re'][d], p['ssm_b_im'][d]
        bbar_re = coef_re[..., None] * br - coef_im[..., None] * bi
        bbar_im = coef_re[..., None] * bi + coef_im[..., None] * br
        cr, ci = p['ssm_c_re'][d], p['ssm_c_im'][d]
        pm = jnp.exp(tau * (lam_re * dt)[None])
        pr = pm * jnp.cos(tau * (lam_im * dt)[None])
        pi = pm * jnp.sin(tau * (lam_im * dt)[None])
        m_re = cr[None] * pr[:, :, None, :] - ci[None] * pi[:, :, None, :]
        m_im = cr[None] * pi[:, :, None, :] + ci[None] * pr[:, :, None, :]
        k = (jnp.einsum('tghp,gpk->gtkh', m_re[:c], bbar_re, precision=hp)
             - jnp.einsum('tghp,gpk->gtkh', m_im[:c], bbar_im, precision=hp))
        ks.append(k)
        pw = jnp.arange(c - 1, -1, -1) if d == 0 else jnp.arange(c)
        w_re = pr[pw][:, :, None, :] * jnp.swapaxes(bbar_re, 1, 2)[None] - pi[pw][:, :, None, :] * jnp.swapaxes(bbar_im, 1, 2)[None]
        w_im = pr[pw][:, :, None, :] * jnp.swapaxes(bbar_im, 1, 2)[None] + pi[pw][:, :, None, :] * jnp.swapaxes(bbar_re, 1, 2)[None]
        ws.append((jnp.transpose(w_re, (1, 0, 2, 3)).reshape(SSM_GROUPS, 256, SSM_STATE),
                   jnp.transpose(w_im, (1, 0, 2, 3)).reshape(SSM_GROUPS, 256, SSM_STATE)))
        po = jnp.arange(1, c + 1) if d == 0 else jnp.arange(c, 0, -1)
        v_re = m_re[po]
        v_im = m_im[po]
        vs.append((jnp.transpose(v_re, (1, 3, 0, 2)).reshape(SSM_GROUPS, SSM_STATE, 256),
                   jnp.transpose(-v_im, (1, 3, 0, 2)).reshape(SSM_GROUPS, SSM_STATE, 256)))
        lams.append((pr[c], pi[c]))
    j = jnp.arange(c)[:, None]
    t = jnp.arange(c)[None, :]
    k0 = ks[0][:, jnp.clip(t - j, 0, c - 1)] * (t >= j)[None, :, :, None, None]
    k1 = ks[1][:, jnp.clip(j - t, 0, c - 1)] * (j >= t)[None, :, :, None, None]
    tm = jnp.transpose(k0 + k1, (0, 1, 3, 2, 4)).reshape(SSM_GROUPS, 256, 256)
    dd = jnp.tile(p['ssm_d'], (1, c))
    tm = tm + jnp.eye(256, dtype=F32)[None] * dd[:, None, :]
    npair = SSM_GROUPS // 2

    def pair_diag(x):
        g, r, cc = x.shape
        x = x.reshape(npair, 2, r, cc)
        z = jnp.zeros((npair, 2, r, 2, cc), F32)
        z = z.at[:, 0, :, 0, :].set(x[:, 0]).at[:, 1, :, 1, :].set(x[:, 1])
        return z.reshape(npair, 2 * r, 2 * cc)

    w = jnp.concatenate([pair_diag(ws[0][0]), pair_diag(ws[0][1]), pair_diag(ws[1][0]), pair_diag(ws[1][1])], axis=2)
    v = jnp.concatenate([pair_diag(vs[0][0]), pair_diag(vs[0][1]), pair_diag(vs[1][0]), pair_diag(vs[1][1])], axis=1)
    lam = jnp.stack([lams[0][0], lams[0][1], lams[1][0], lams[1][1]], axis=1)
    lam = lam.reshape(npair, 2, 4, SSM_STATE).transpose(0, 2, 1, 3).reshape(npair, 4, LANES)
    lam = jnp.concatenate([lam, jnp.zeros((npair, 4, LANES), F32)], axis=1)
    return {'t': tm.astype(BF16), 'w': w.astype(BF16), 'v': v.astype(BF16), 'lam': lam}


GDN_HALO = 16


def _gdn_prep_body(nt, q_ref, qp_ref, qn_ref, k_ref, kp_ref, kn_ref, v_ref, vp_ref, vn_ref,
                   gate_ref, cw_ref, al_ref, dtb_ref, bd_ref, qo_ref, ko_ref, vo_ref, gb_ref):
    i = pl.program_id(1)
    cw = cw_ref[...]
    tl = q_ref.shape[0]

    def conv(cur_ref, prev_ref, next_ref, col):
        prev = jnp.where(i > 0, prev_ref[...].astype(F32), 0.0)
        nxt = jnp.where(i < nt - 1, next_ref[...].astype(F32), 0.0)
        xe = jnp.concatenate([prev, cur_ref[...].astype(F32), nxt], axis=0)
        n = xe.shape[0]
        acc = jnp.zeros((tl, 256), F32)
        for tap in range(5):
            s = tap - 2
            sh = xe if s == 0 else pltpu.roll(xe, (-s) % n, 0)
            acc = acc + sh[GDN_HALO:GDN_HALO + tl] * cw[tap:tap + 1, col:col + 256]
        return acc * jax.nn.sigmoid(acc)

    bd = bd_ref[...]

    def l2n(x):
        return x * lax.rsqrt(_head_mean_sq(x, bd) * HEAD_DIM + NORM_EPS)

    qo_ref[...] = l2n(conv(q_ref, qp_ref, qn_ref, 0)).astype(BF16)
    ko_ref[...] = l2n(conv(k_ref, kp_ref, kn_ref, 256)).astype(BF16)
    vo_ref[...] = conv(v_ref, vp_ref, vn_ref, 512).astype(BF16)
    gt = gate_ref[...]
    lane = lax.broadcasted_iota(jnp.int32, gt.shape, 1)
    x = gt + dtb_ref[...]
    softplus = jnp.maximum(x, 0.0) + jnp.log(1.0 + jnp.exp(-jnp.abs(x)))
    gb_ref[...] = jnp.where(lane < 8, -al_ref[...] * softplus, jnp.where(lane < 16, jax.nn.sigmoid(gt), 0.0))


def _gdn_prep(main, gate, q, bd, b, l):
    tl = min(256, l)
    nt = l // tl
    hb = tl // GDN_HALO
    nh = l // GDN_HALO

    def cur(col):
        return pl.BlockSpec((tl, 256), lambda bi, i: (bi * nt + i, col // 256))

    def prev(col):
        return pl.BlockSpec((GDN_HALO, 256), lambda bi, i: (bi * nh + jnp.maximum(i * hb - 1, 0), col // 256))

    def nxt(col):
        return pl.BlockSpec((GDN_HALO, 256), lambda bi, i: (bi * nh + jnp.minimum((i + 1) * hb, nh - 1), col // 256))

    full = lambda bi, i: (0, 0)
    row = lambda bi, i: (bi * nt + i, 0)
    specs = []
    for col in (COL_CQ, COL_CK, COL_CV):
        specs += [cur(col), prev(col), nxt(col)]
    specs += [pl.BlockSpec((tl, GATE_W), row), pl.BlockSpec((8, 768), full), pl.BlockSpec((1, LANES), full),
              pl.BlockSpec((1, LANES), full), pl.BlockSpec((256, 256), full)]
    return pl.pallas_call(
        functools.partial(_gdn_prep_body, nt),
        grid=(b, nt),
        in_specs=specs,
        out_specs=[pl.BlockSpec((tl, 256), row)] * 3 + [pl.BlockSpec((tl, GATE_W), row)],
        out_shape=[jax.ShapeDtypeStruct((b * l, 256), BF16)] * 3 + [jax.ShapeDtypeStruct((b * l, GATE_W), F32)],
        compiler_params=_cparams(("parallel", "parallel")),
        name="gdn_prep",
    )(*([main] * 9), gate, q['gdn_conv_w'], q['gdn_al'], q['gdn_dtb'], bd)


def _gdn_chunk_dir(dirn, qb, kb16, vb, gb, s_ref):
    c = GDN_CHUNK
    rev = dirn == 1
    q = qb.astype(F32)
    k = kb16.astype(F32)
    v = vb.astype(F32)
    r256 = lax.broadcasted_iota(jnp.int32, (256, 256), 0)
    c256 = lax.broadcasted_iota(jnp.int32, (256, 256), 1)
    blockmask = (r256 // c) == (c256 // c)
    rl = lax.broadcasted_iota(jnp.int32, (LANES, 256), 0)
    cl = lax.broadcasted_iota(jnp.int32, (LANES, 256), 1)
    e_g = (rl == dirn * 4 + cl // c).astype(BF16)
    e_b = (rl == 8 + dirn * 4 + cl // c).astype(BF16)
    g_full = _split_dot(gb, e_g)
    beta = _split_dot(gb, e_b)
    ri = lax.broadcasted_iota(jnp.int32, (c, c), 0)
    ci = lax.broadcasted_iota(jnp.int32, (c, c), 1)
    tri = ((ci >= ri) if rev else (ci <= ri)).astype(BF16)
    ghi = g_full.astype(BF16)
    glo = (g_full - ghi.astype(F32)).astype(BF16)
    gc = _dot(tri, ghi) + _dot(tri, glo)
    i_s = lax.broadcasted_iota(jnp.int32, (c, 256), 0)
    j_s = lax.broadcasted_iota(jnp.int32, (c, 256), 1) % c
    z = jnp.where(i_s == j_s, gc, 0.0)
    ones = jnp.ones((c, c), BF16)
    zhi = z.astype(BF16)
    zlo = (z - zhi.astype(F32)).astype(BF16)
    gct = _dot(ones, zhi) + _dot(ones, zlo)
    allowed = (j_s >= i_s) if rev else (j_s <= i_s)
    strict = (j_s > i_s) if rev else (j_s < i_s)
    decay = jnp.exp(jnp.where(allowed, gc - gct, -jnp.inf))
    eg = jnp.exp(gc)
    kbeta = k * beta

    def bdv(y):
        return jnp.where(blockmask, jnp.concatenate([y, y, y, y], axis=0), 0.0).astype(BF16)

    kk_qk = _dot_nt(jnp.concatenate([kbeta, q * (HEAD_DIM ** -0.5)], axis=0).astype(BF16), bdv(k))
    a = jnp.where(strict, kk_qk[:c] * decay, 0.0)
    intra = jnp.where(allowed, kk_qk[c:] * decay, 0.0)
    eye = (i_s == j_s).astype(F32)
    p = -a
    t = eye + p
    for _ in range(5):
        p = _dot(p.astype(BF16), bdv(p))
        t = t + _dot(t.astype(BF16), bdv(p))
    t16 = t.astype(BF16)
    u = _dot(t16, bdv(v * beta))
    w = _dot(t16, bdv(kbeta * eg))
    s = s_ref[dirn]
    s16 = s.astype(BF16)
    ws_qs = _dot(jnp.concatenate([w, q * (HEAD_DIM ** -0.5) * eg], axis=0).astype(BF16), s16)
    v_new = u - ws_qs[:c]
    o = ws_qs[c:] + _dot(intra.astype(BF16), bdv(v_new))
    last = 0 if rev else c - 1
    g_last = gc[last:last + 1, :]
    kg = (k * jnp.exp(g_last - gc)).astype(BF16)
    upd = lax.dot_general(kg, v_new.astype(BF16), (((0,), (0,)), ((), ())), preferred_element_type=F32)
    s_ref[dirn] = s * jnp.exp(g_last) + jnp.where(blockmask, upd, 0.0)
    return o


def _gdn_chunk_body(qf_ref, kf_ref, vf_ref, gf_ref, qb_ref, kb_ref, vb_ref, gb_ref, of_ref, ob_ref, s_ref):
    @pl.when(pl.program_id(1) == 0)
    def _():
        s_ref[...] = jnp.zeros(s_ref.shape, F32)

    of_ref[...] = _gdn_chunk_dir(0, qf_ref[...], kf_ref[...], vf_ref[...], gf_ref[...], s_ref).astype(BF16)
    ob_ref[...] = _gdn_chunk_dir(1, qb_ref[...], kb_ref[...], vb_ref[...], gb_ref[...], s_ref).astype(BF16)


def _gdn_chunks(qn, kn, vs, gb, b, l):
    c = GDN_CHUNK
    n = l // c
    fwd = lambda bi, i: (bi * n + i, 0)
    bwd = lambda bi, i: (bi * n + n - 1 - i, 0)
    blk = lambda m: pl.BlockSpec((c, 256), m)
    gblk = lambda m: pl.BlockSpec((c, GATE_W), m)
    return pl.pallas_call(
        _gdn_chunk_body,
        grid=(b, n),
        in_specs=[blk(fwd), blk(fwd), blk(fwd), gblk(fwd), blk(bwd), blk(bwd), blk(bwd), gblk(bwd)],
        out_specs=[blk(fwd), blk(bwd)],
        out_shape=[jax.ShapeDtypeStruct((b * l, 256), BF16)] * 2,
        scratch_shapes=[pltpu.VMEM((2, 256, 256), F32)],
        compiler_params=_cparams(("parallel", "arbitrary")),
        name="gdn_chunks",
    )(qn, kn, vs, gb, qn, kn, vs, gb)


def _gdn_out_body(of_ref, ob_ref, z_ref, g_ref, bd_ref, o_ref):
    o = of_ref[...].astype(F32) + ob_ref[...].astype(F32)
    z = z_ref[...].astype(F32)
    ms = _head_mean_sq(o, bd_ref[...])
    o_ref[...] = (o * lax.rsqrt(ms + NORM_EPS) * g_ref[...] * (z * jax.nn.sigmoid(z))).astype(BF16)


def _gdn_out(of, ob, main, g, bd):
    t = of.shape[0]
    tm = min(1024, t)
    row = lambda i: (i, 0)
    full = lambda i: (0, 0)
    return pl.pallas_call(
        _gdn_out_body,
        grid=(t // tm,),
        in_specs=[pl.BlockSpec((tm, 256), row), pl.BlockSpec((tm, 256), row),
                  pl.BlockSpec((tm, 256), lambda i: (i, COL_CZ // 256)),
                  pl.BlockSpec((1, 256), full), pl.BlockSpec((256, 256), full)],
        out_specs=pl.BlockSpec((tm, 256), row),
        out_shape=jax.ShapeDtypeStruct((t, 256), BF16),
        compiler_params=_cparams(("parallel",)),
        name="gdn_out",
    )(of, ob, main, g, bd)


def _gdn(main, gate, q, bd, b, l):
    qn, kn, vs, gb = _gdn_prep(main, gate, q, bd, b, l)
    of, ob = _gdn_chunks(qn, kn, vs, gb, b, l)
    return _gdn_out(of, ob, main, q['gdn_o_norm'], bd)


def _moe(x1, x1b, top_idx, gates, p):
    t = x1.shape[0]
    m = t * TOP_K
    flat_e = top_idx.reshape(-1)
    order = jnp.argsort(flat_e)
    sorted_e = flat_e[order]
    counts = jnp.bincount(flat_e, length=N_EXPERTS)
    padded = (counts + MOE_BLOCK - 1) // MOE_BLOCK * MOE_BLOCK
    group_start = jnp.cumsum(counts) - counts
    padded_end = jnp.cumsum(padded)
    padded_start = padded_end - padded
    dest = (padded_start[sorted_e] + jnp.arange(m) - group_start[sorted_e]).astype(jnp.int32)
    n_blocks = -(-m // MOE_BLOCK) + N_EXPERTS
    n_slots = n_blocks * MOE_BLOCK
    slot_token = jnp.full((n_slots,), t, jnp.int32).at[dest].set((order // TOP_K).astype(jnp.int32))
    block_expert = jnp.minimum(jnp.searchsorted(padded_end, jnp.arange(n_blocks) * MOE_BLOCK, side='right'),
                               N_EXPERTS - 1).astype(jnp.int32)
    n_used = (padded_end[-1] // MOE_BLOCK).astype(jnp.int32).reshape(1)
    x_pad = jnp.concatenate([x1b, jnp.zeros((1, D_MODEL), BF16)], axis=0)
    xb = x_pad[slot_token]
    yb = _ffn(xb, block_expert, n_used, p['moe_w1'], p['moe_b1'], p['moe_w2'], p['moe_b2'])
    dest_tk = jnp.zeros((m,), jnp.int32).at[order].set(dest).reshape(t, TOP_K)
    return jnp.sum(yb[dest_tk].astype(F32) * gates[..., None], axis=1)


def _block_diag_mean(width, group):
    idx = np.arange(width)
    return jnp.asarray((idx[:, None] // group == idx[None, :] // group) / group, BF16)


def _axial_tables(l):
    rows = l // GRID_W
    row_pos = np.repeat(np.arange(rows), GRID_W).astype(np.float64)
    col_pos = np.tile(np.arange(GRID_W), rows).astype(np.float64)
    lane = np.arange(LANES)
    d = lane % HEAD_DIM
    e = d % 32
    f = e % 16
    inv = (AXIAL_THETA ** (-(np.arange(0, 32, 2, dtype=np.float32)) / 32)).astype(np.float32)
    pos = np.where((d // 32)[None, :] == 0, row_pos[:, None], col_pos[:, None]).astype(np.float32)
    ang = pos * inv[f][None, :]
    sign = np.where(e < 16, -1.0, 1.0)[None, :]
    return jnp.asarray(np.cos(ang), F32), jnp.asarray(np.sin(ang) * sign, F32)


def _diff_tables(l):
    lane = np.arange(256)
    e = lane % 32
    f = e % 4
    inv = (ROPE_THETA ** (-(np.arange(0, PARTIAL_ROPE_DIMS, 2, dtype=np.float32)) / PARTIAL_ROPE_DIMS)).astype(np.float32)
    ang = np.arange(l, dtype=np.float32)[:, None] * inv[f][None, :]
    roped = (e < PARTIAL_ROPE_DIMS)[None, :]
    sign = np.where(e < 4, -1.0, 1.0)[None, :]
    c = np.where(roped, np.cos(ang), 1.0)
    s = np.where(roped, np.sin(ang) * sign, 0.0)
    return jnp.asarray(c, F32), jnp.asarray(s, F32)


def _prep_layer(params, layer):
    p = {name: arr[layer] for name, arr in params.items()}
    w_in = p['w_in']
    w_in = jnp.concatenate([w_in[:, :ORIG_CA], w_in[:, ORIG_CA + 16:], w_in[:, ORIG_CA:ORIG_CA + 16],
                            jnp.zeros((D_MODEL, GATE_W - 16), F32)], axis=1)
    q = dict(p)
    q['w_in'] = w_in.astype(BF16)
    q['w_out'] = p['w_out'].astype(BF16)
    q['router_w'] = jnp.concatenate([p['router_w'], jnp.zeros((D_MODEL, LANES - N_EXPERTS), F32)], axis=1).astype(BF16)
    q['router_b'] = jnp.concatenate([p['router_b'], jnp.full((LANES - N_EXPERTS,), -1e30, F32)])[None, :]
    w1 = p['moe_w1']
    q['moe_w1'] = jnp.concatenate([w1[:, :, 0::2], w1[:, :, 1::2]], axis=2).astype(BF16)
    b1 = p['moe_b1']
    q['moe_b1'] = jnp.concatenate([b1[:, 0::2], b1[:, 1::2]], axis=1)[:, None, :]
    q['moe_w2'] = p['moe_w2'].astype(BF16)
    q['moe_b2'] = p['moe_b2'][:, None, :]
    for name in ('ln1_g', 'ln1_b', 'ln2_g', 'ln2_b'):
        q[name] = p[name][None, :]
    q['gqa_q_norm'] = jnp.tile(p['gqa_q_norm'], 4)[None, :]
    q['gqa_k_norm'] = jnp.tile(p['gqa_k_norm'], 2)[None, :]
    lambda_init = 0.8 - 0.6 * math.exp(-0.3 * layer)
    q['diff_subln'] = (jnp.tile(p['diff_subln'], 4) * (1.0 - lambda_init))[None, :]
    lam = (jnp.exp(jnp.sum(p['diff_lambda_q1'] * p['diff_lambda_k1']))
           - jnp.exp(jnp.sum(p['diff_lambda_q2'] * p['diff_lambda_k2'])) + lambda_init)
    q['diff_lam'] = lam.reshape(1).astype(F32)
    q['s5'] = _s5_prep(p)
    q['ssm_glu_w'] = p['ssm_glu_w'].astype(BF16)
    q['ssm_glu_b'] = p['ssm_glu_b'][None, :]
    q['gdn_conv_w'] = jnp.concatenate([p['gdn_conv_w'], jnp.zeros((3, 768), F32)], axis=0)
    pad8 = lambda x: jnp.concatenate([x.reshape(-1), jnp.zeros((LANES - 8,), F32)])[None, :]
    q['gdn_al'] = pad8(jnp.exp(p['gdn_a_log']))
    q['gdn_dtb'] = pad8(p['gdn_dt_bias'])
    q['gdn_o_norm'] = jnp.tile(p['gdn_o_norm'], 4)[None, :]
    return q


def _mixers(main, gate, q, tabs, b, l):
    a_out = _s5(main, q, b, l)
    b_out = _gqa(main, tabs['axial'], q['gqa_q_norm'], q['gqa_k_norm'], tabs['bd64'], b, l)
    c_out = _gdn(main, gate, q, tabs['bd64'], b, l)
    d_out = _diff(main, q['diff_lam'], tabs['diff'], q['diff_subln'], tabs['bd64'], b, l)
    return a_out, b_out, c_out, d_out


def _layer(x, q, tabs, b, l, alpha):
    main, gate = _inproj(x, q['w_in'])
    pieces = _mixers(main, gate, q, tabs, b, l)
    x1, x1b, idx, gates = _outproj(pieces, q['w_out'], x, q['ln1_g'], q['ln1_b'],
                                   q['router_w'], q['router_b'], alpha)
    ffn = _moe(x1, x1b, idx[:, :TOP_K], gates[:, :TOP_K], q)
    return _ln2(x1, ffn, q['ln2_g'], q['ln2_b'], alpha)


def _trunk(x, layers, alpha):
    b, l, _ = x.shape
    tabs = {'axial': _axial_tables(l), 'diff': _diff_tables(l), 'bd64': _block_diag_mean(256, HEAD_DIM)}
    h = x.reshape(b * l, D_MODEL)
    for q in layers:
        h = _layer(h, q, tabs, b, l, alpha)
    return h.reshape(b, l, D_MODEL)


def kernel(x_prompt, x_sample, w_in, w_out, ssm_a_re, ssm_a_im, ssm_log_dt, ssm_b_re, ssm_b_im, ssm_c_re, ssm_c_im, ssm_d, ssm_glu_w, ssm_glu_b, gqa_q_norm, gqa_k_norm, gdn_conv_w, gdn_a_log, gdn_dt_bias, gdn_o_norm, diff_lambda_q1, diff_lambda_k1, diff_lambda_q2, diff_lambda_k2, diff_subln, router_w, router_b, moe_w1, moe_b1, moe_w2, moe_b2, ln1_g, ln1_b, ln2_g, ln2_b):
    params = {
        'w_in': w_in, 'w_out': w_out,
        'ssm_a_re': ssm_a_re, 'ssm_a_im': ssm_a_im, 'ssm_log_dt': ssm_log_dt,
        'ssm_b_re': ssm_b_re, 'ssm_b_im': ssm_b_im, 'ssm_c_re': ssm_c_re, 'ssm_c_im': ssm_c_im,
        'ssm_d': ssm_d, 'ssm_glu_w': ssm_glu_w, 'ssm_glu_b': ssm_glu_b,
        'gqa_q_norm': gqa_q_norm, 'gqa_k_norm': gqa_k_norm,
        'gdn_conv_w': gdn_conv_w, 'gdn_a_log': gdn_a_log, 'gdn_dt_bias': gdn_dt_bias, 'gdn_o_norm': gdn_o_norm,
        'diff_lambda_q1': diff_lambda_q1, 'diff_lambda_k1': diff_lambda_k1,
        'diff_lambda_q2': diff_lambda_q2, 'diff_lambda_k2': diff_lambda_k2, 'diff_subln': diff_subln,
        'router_w': router_w, 'router_b': router_b,
        'moe_w1': moe_w1, 'moe_b1': moe_b1, 'moe_w2': moe_w2, 'moe_b2': moe_b2,
        'ln1_g': ln1_g, 'ln1_b': ln1_b, 'ln2_g': ln2_g, 'ln2_b': ln2_b,
    }
    depth = w_in.shape[0]
    alpha = (2.0 * depth) ** 0.25
    layers = [_prep_layer(params, layer) for layer in range(depth)]
    return (_trunk(x_prompt, layers, alpha), _trunk(x_sample, layers, alpha))
```

```python
import functools
import math

import jax
import jax.numpy as jnp
import numpy as np
from jax import lax
from jax.experimental import pallas as pl
from jax.experimental.pallas import tpu as pltpu

F32 = jnp.float32
BF16 = jnp.bfloat16

D_MODEL = 1024
GROUP_WIDTH = 256
HEAD_DIM = 64
SSM_GROUPS = 16
SSM_GROUP_CH = 16
SSM_STATE = 64
GDN_HEADS = 4
GDN_CHUNK = 64
DIFF_SUB_DIM = 32
PARTIAL_ROPE_DIMS = 8
AXIAL_THETA = 10000.0
ROPE_THETA = 500000.0
GRID_W = 64
N_EXPERTS = 32
TOP_K = 4
SWIGLU_ALPHA = 1.702
SWIGLU_LIMIT = 7.0
NORM_EPS = 1e-6
LN_EPS = 1e-5

LANES = 128
VMEM_LIMIT = 56 * 1024 * 1024

COL_U, COL_GQ, COL_GK, COL_GV = 0, 256, 512, 640
COL_CQ, COL_CK, COL_CV, COL_CZ = 768, 1024, 1280, 1536
COL_DQ, COL_DK, COL_DV = 1792, 2048, 2304
MAIN_W = 2560
GATE_W = 128
ORIG_CA = 1792


def _cparams(sem):
    return pltpu.CompilerParams(dimension_semantics=sem, vmem_limit_bytes=VMEM_LIMIT)


def _dot(a, b):
    return jnp.dot(a, b, preferred_element_type=F32)


def _dot_nt(a, b):
    return lax.dot_general(a, b, (((1,), (1,)), ((), ())), preferred_element_type=F32)


def _split_dot(x, m):
    hi = x.astype(BF16)
    lo = (x - hi.astype(F32)).astype(BF16)
    return _dot(hi, m) + _dot(lo, m)


def _inproj_body(x_ref, w_ref, main_ref, gate_ref):
    x = x_ref[...].astype(BF16)
    step = 640
    for c in range(MAIN_W // step):
        main_ref[:, c * step:(c + 1) * step] = _dot(x, w_ref[:, c * step:(c + 1) * step]).astype(BF16)
    gate_ref[...] = _dot(x, w_ref[:, MAIN_W:])


def _inproj(x, w):
    t = x.shape[0]
    tm = min(512, t)
    return pl.pallas_call(
        _inproj_body,
        grid=(t // tm,),
        in_specs=[pl.BlockSpec((tm, D_MODEL), lambda i: (i, 0)),
                  pl.BlockSpec((D_MODEL, MAIN_W + GATE_W), lambda i: (0, 0))],
        out_specs=[pl.BlockSpec((tm, MAIN_W), lambda i: (i, 0)),
                   pl.BlockSpec((tm, GATE_W), lambda i: (i, 0))],
        out_shape=[jax.ShapeDtypeStruct((t, MAIN_W), BF16),
                   jax.ShapeDtypeStruct((t, GATE_W), F32)],
        compiler_params=_cparams(("parallel",)),
        name="inproj",
    )(x, w)


def _head_mean_sq(xf, bd):
    return _split_dot(xf * xf, bd)


def _rope_lanes(x, c, s, half):
    n = x.shape[-1]
    lane = lax.broadcasted_iota(jnp.int32, x.shape, 1)
    first = (lane % (2 * half)) < half
    swapped = jnp.where(first, pltpu.roll(x, n - half, 1), pltpu.roll(x, half, 1))
    return x * c + swapped * s


def _gqa_body(q_ref, k_ref, v_ref, cq_ref, sq_ref, ck_ref, sk_ref, gq_ref, gk_ref, bd_ref,
              o_ref, kdup_ref):
    i = pl.program_id(1)
    bd = bd_ref[...]

    @pl.when(i == 0)
    def _():
        kf = k_ref[...].astype(F32)
        ms = _head_mean_sq(kf, bd[:LANES, :LANES])
        kn = kf * lax.rsqrt(ms + NORM_EPS) * gk_ref[...]
        kn = _rope_lanes(kn, ck_ref[...], sk_ref[...], 16)
        lane = lax.broadcasted_iota(jnp.int32, kn.shape, 1)
        sw = pltpu.roll(kn, 64, 1)
        kdup_ref[0] = jnp.where(lane < 64, kn, sw).astype(BF16)
        kdup_ref[1] = jnp.where(lane < 64, sw, kn).astype(BF16)

    qf = q_ref[...].astype(F32)
    ms = _head_mean_sq(qf, bd)
    qn = qf * lax.rsqrt(ms + NORM_EPS) * gq_ref[...]
    cq = jnp.concatenate([cq_ref[...], cq_ref[...]], axis=1)
    sq = jnp.concatenate([sq_ref[...], sq_ref[...]], axis=1)
    qn = (_rope_lanes(qn, cq, sq, 16) * (HEAD_DIM ** -0.5)).astype(BF16)
    tq = qn.shape[0]
    v = v_ref[...]
    lane = lax.broadcasted_iota(jnp.int32, (tq, LANES), 1)
    zero = jnp.zeros((tq, LANES), BF16)
    for h in range(2):
        qh = qn[:, h * LANES:(h + 1) * LANES]
        q2 = jnp.concatenate([jnp.where(lane < 64, qh, zero), jnp.where(lane < 64, zero, qh)], axis=0)
        s = _dot_nt(q2, kdup_ref[h])
        m = jnp.max(s, axis=-1, keepdims=True)
        p = jnp.exp(s - m)
        l = jnp.sum(p, axis=-1, keepdims=True)
        o2 = _dot(p.astype(BF16), v) / l
        top, bot = o2[:tq], o2[tq:]
        if h == 0:
            oh = jnp.where(lane < 64, top, pltpu.roll(bot, 64, 1))
        else:
            oh = jnp.where(lane < 64, pltpu.roll(top, 64, 1), bot)
        o_ref[:, h * LANES:(h + 1) * LANES] = oh.astype(BF16)


def _gqa(main, tabs, gq, gk, bd, b, l):
    tq = min(256, l)
    nq = l // tq
    cq, sq = tabs
    full = lambda bi, i: (0, 0)
    return pl.pallas_call(
        _gqa_body,
        grid=(b, nq),
        in_specs=[pl.BlockSpec((tq, 256), lambda bi, i: (bi * nq + i, COL_GQ // 256)),
                  pl.BlockSpec((l, LANES), lambda bi, i: (bi, COL_GK // LANES)),
                  pl.BlockSpec((l, LANES), lambda bi, i: (bi, COL_GV // LANES)),
                  pl.BlockSpec((tq, LANES), lambda bi, i: (i, 0)),
                  pl.BlockSpec((tq, LANES), lambda bi, i: (i, 0)),
                  pl.BlockSpec((l, LANES), full),
                  pl.BlockSpec((l, LANES), full),
                  pl.BlockSpec((1, 256), full),
                  pl.BlockSpec((1, LANES), full),
                  pl.BlockSpec((256, 256), full)],
        out_specs=pl.BlockSpec((tq, 256), lambda bi, i: (bi * nq + i, 0)),
        out_shape=jax.ShapeDtypeStruct((b * l, 256), BF16),
        scratch_shapes=[pltpu.VMEM((2, l, LANES), BF16)],
        compiler_params=_cparams(("parallel", "arbitrary")),
        name="gqa",
    )(main, main, main, cq, sq, cq, sq, gq, gk, bd)


def _diff_body(lam_ref, q_ref, k_ref, v_ref, cq_ref, sq_ref, ck_ref, sk_ref, g_ref, bd_ref,
               o_ref, kr_ref):
    i = pl.program_id(1)

    @pl.when(i == 0)
    def _():
        kr_ref[...] = _rope_lanes(k_ref[...].astype(F32), ck_ref[...], sk_ref[...], 4).astype(BF16)

    lam = lam_ref[0]
    qr = _rope_lanes(q_ref[...].astype(F32), cq_ref[...], sq_ref[...], 4)
    qr = (qr * (DIFF_SUB_DIM ** -0.5)).astype(BF16)
    tq = qr.shape[0]
    lane = lax.broadcasted_iota(jnp.int32, (tq, 256), 1)
    zero = jnp.zeros((tq, 256), BF16)
    kr = kr_ref[...]
    v = v_ref[...]
    acc = jnp.zeros((tq, 256), F32)
    for h in range(4):
        q2 = jnp.concatenate([jnp.where(lane // 32 == 2 * h, qr, zero),
                              jnp.where(lane // 32 == 2 * h + 1, qr, zero)], axis=0)
        s = _dot_nt(q2, kr)
        m = jnp.max(s, axis=-1, keepdims=True)
        e = jnp.exp(s - m)
        l = jnp.sum(e, axis=-1, keepdims=True)
        l1, l2 = l[:tq], l[tq:]
        pm = (e[:tq] - (lam * l1 / l2) * e[tq:]).astype(BF16)
        o = _dot(pm, v) / l1
        acc = jnp.where(lane // 64 == h, o, acc)
    ms = _head_mean_sq(acc, bd_ref[...])
    o_ref[...] = (acc * lax.rsqrt(ms + NORM_EPS) * g_ref[...]).astype(BF16)


def _diff(main, lam, tabs, g, bd, b, l):
    tq = min(256, l)
    nq = l // tq
    c, s = tabs
    full = lambda bi, i, *_: (0, 0)
    grid_spec = pltpu.PrefetchScalarGridSpec(
        num_scalar_prefetch=1,
        grid=(b, nq),
        in_specs=[pl.BlockSpec((tq, 256), lambda bi, i, *_: (bi * nq + i, COL_DQ // 256)),
                  pl.BlockSpec((l, 256), lambda bi, i, *_: (bi, COL_DK // 256)),
                  pl.BlockSpec((l, 256), lambda bi, i, *_: (bi, COL_DV // 256)),
                  pl.BlockSpec((tq, 256), lambda bi, i, *_: (i, 0)),
                  pl.BlockSpec((tq, 256), lambda bi, i, *_: (i, 0)),
                  pl.BlockSpec((l, 256), full),
                  pl.BlockSpec((l, 256), full),
                  pl.BlockSpec((1, 256), full),
                  pl.BlockSpec((256, 256), full)],
        out_specs=pl.BlockSpec((tq, 256), lambda bi, i, *_: (bi * nq + i, 0)),
        scratch_shapes=[pltpu.VMEM((l, 256), BF16)],
    )
    return pl.pallas_call(
        _diff_body,
        grid_spec=grid_spec,
        out_shape=jax.ShapeDtypeStruct((b * l, 256), BF16),
        compiler_params=_cparams(("parallel", "arbitrary")),
        name="diffattn",
    )(lam, main, main, main, c, s, c, s, g, bd)


def _layer_norm_rows(y, g, b):
    mu = jnp.mean(y, axis=-1, keepdims=True)
    d = y - mu
    var = jnp.mean(d * d, axis=-1, keepdims=True)
    return d * lax.rsqrt(var + LN_EPS) * g + b


def _outproj_body(alpha, a_ref, b_ref, c_ref, d_ref, w_ref, x_ref, g_ref, beta_ref, rw_ref, rb_ref,
                  x1_ref, x1b_ref, idx_ref, gate_ref):
    mixed = (_dot(a_ref[...], w_ref[0:256, :]) + _dot(b_ref[...], w_ref[256:512, :])
             + _dot(c_ref[...], w_ref[512:768, :]) + _dot(d_ref[...], w_ref[768:1024, :]))
    x1 = _layer_norm_rows(alpha * x_ref[...] + mixed, g_ref[...], beta_ref[...])
    x1_ref[...] = x1
    x1b = x1.astype(BF16)
    x1b_ref[...] = x1b
    logits = _dot(x1b, rw_ref[...]) + rb_ref[...]
    lane = lax.broadcasted_iota(jnp.int32, logits.shape, 1)
    vals, idxs = [], []
    for _ in range(TOP_K):
        m = jnp.max(logits, axis=-1, keepdims=True)
        ix = jnp.min(jnp.where(logits == m, lane, LANES), axis=-1, keepdims=True)
        vals.append(m)
        idxs.append(ix)
        logits = jnp.where(lane == ix, -jnp.inf, logits)
    es = [jnp.exp(vk - vals[0]) for vk in vals]
    tot = es[0] + es[1] + es[2] + es[3]
    idx_out = jnp.zeros(logits.shape, jnp.int32)
    gate_out = jnp.zeros(logits.shape, F32)
    for k in range(TOP_K):
        idx_out = jnp.where(lane == k, idxs[k], idx_out)
        gate_out = jnp.where(lane == k, es[k] / tot, gate_out)
    idx_ref[...] = idx_out
    gate_ref[...] = gate_out


def _outproj(pieces, w, x, g, beta, rw, rb, alpha):
    t = x.shape[0]
    tm = min(512, t)
    row = lambda i: (i, 0)
    full = lambda i: (0, 0)
    return pl.pallas_call(
        functools.partial(_outproj_body, alpha),
        grid=(t // tm,),
        in_specs=[pl.BlockSpec((tm, 256), row)] * 4 + [
            pl.BlockSpec((D_MODEL, D_MODEL), full),
            pl.BlockSpec((tm, D_MODEL), row),
            pl.BlockSpec((1, D_MODEL), full),
            pl.BlockSpec((1, D_MODEL), full),
            pl.BlockSpec((D_MODEL, LANES), full),
            pl.BlockSpec((1, LANES), full)],
        out_specs=[pl.BlockSpec((tm, D_MODEL), row), pl.BlockSpec((tm, D_MODEL), row),
                   pl.BlockSpec((tm, LANES), row), pl.BlockSpec((tm, LANES), row)],
        out_shape=[jax.ShapeDtypeStruct((t, D_MODEL), F32), jax.ShapeDtypeStruct((t, D_MODEL), BF16),
                   jax.ShapeDtypeStruct((t, LANES), jnp.int32), jax.ShapeDtypeStruct((t, LANES), F32)],
        compiler_params=_cparams(("parallel",)),
        name="outproj_ln_router",
    )(*pieces, w, x, g, beta, rw, rb)


MOE_BLOCK = 512


def _ffn_body(be_ref, nb_ref, x_ref, w1_ref, b1_ref, w2_ref, b2_ref, o_ref):
    i = pl.program_id(0)

    @pl.when(i < nb_ref[0])
    def _():
        h = _dot(x_ref[...], w1_ref[...]) + b1_ref[...]
        acts = []
        for c in range(h.shape[1] // 256):
            glu = jnp.minimum(h[:, c * 256:c * 256 + LANES], SWIGLU_LIMIT)
            lin = jnp.clip(h[:, c * 256 + LANES:(c + 1) * 256], -SWIGLU_LIMIT, SWIGLU_LIMIT)
            acts.append((glu * jax.nn.sigmoid(SWIGLU_ALPHA * glu) * (lin + 1.0)).astype(BF16))
        act = jnp.concatenate(acts, axis=1)
        o_ref[...] = (_dot(act, w2_ref[...]) + b2_ref[...]).astype(o_ref.dtype)

    @pl.when(i >= nb_ref[0])
    def _():
        o_ref[...] = jnp.zeros(o_ref.shape, o_ref.dtype)


def _ffn(xb, block_expert, n_used, w1, b1, w2, b2):
    n_slots = xb.shape[0]
    n_blocks = n_slots // MOE_BLOCK
    de2 = w1.shape[2]
    grid_spec = pltpu.PrefetchScalarGridSpec(
        num_scalar_prefetch=2,
        grid=(n_blocks,),
        in_specs=[pl.BlockSpec((MOE_BLOCK, D_MODEL), lambda i, be, nb: (i, 0)),
                  pl.BlockSpec((None, D_MODEL, de2), lambda i, be, nb: (be[i], 0, 0)),
                  pl.BlockSpec((None, 1, de2), lambda i, be, nb: (be[i], 0, 0)),
                  pl.BlockSpec((None, de2 // 2, D_MODEL), lambda i, be, nb: (be[i], 0, 0)),
                  pl.BlockSpec((None, 1, D_MODEL), lambda i, be, nb: (be[i], 0, 0))],
        out_specs=pl.BlockSpec((MOE_BLOCK, D_MODEL), lambda i, be, nb: (i, 0)),
    )
    return pl.pallas_call(
        _ffn_body,
        grid_spec=grid_spec,
        out_shape=jax.ShapeDtypeStruct((n_slots, D_MODEL), BF16),
        compiler_params=_cparams(("arbitrary",)),
        name="moe_ffn",
    )(block_expert, n_used, xb, w1, b1, w2, b2)


def _w1_prep_body(w_ref, p_ref, o_ref):
    p = p_ref[...]
    for c in range(w_ref.shape[1] // 256):
        o_ref[:, c * 256:(c + 1) * 256] = _dot(w_ref[:, c * 256:(c + 1) * 256].astype(BF16), p).astype(BF16)


def _w1_prep(w1):
    ne, dm, de2 = w1.shape
    tn = 512
    j = np.arange(256)
    src = np.where(j < LANES, 2 * j, 2 * (j - LANES) + 1)
    perm = jnp.asarray(np.arange(256)[:, None] == src[None, :], BF16)
    return pl.pallas_call(
        _w1_prep_body,
        grid=(ne, de2 // tn),
        in_specs=[pl.BlockSpec((None, dm, tn), lambda e, c: (e, 0, c)),
                  pl.BlockSpec((256, 256), lambda e, c: (0, 0))],
        out_specs=pl.BlockSpec((None, dm, tn), lambda e, c: (e, 0, c)),
        out_shape=jax.ShapeDtypeStruct((ne, dm, de2), BF16),
        compiler_params=_cparams(("parallel", "parallel")),
        name="w1_prep",
    )(w1, perm)


def _ln2_body(alpha, x_ref, f_ref, g_ref, b_ref, o_ref):
    o_ref[...] = _layer_norm_rows(alpha * x_ref[...] + f_ref[...], g_ref[...], b_ref[...])


def _ln2(x, f, g, b, alpha):
    t = x.shape[0]
    tm = min(512, t)
    row = lambda i: (i, 0)
    full = lambda i: (0, 0)
    return pl.pallas_call(
        functools.partial(_ln2_body, alpha),
        grid=(t // tm,),
        in_specs=[pl.BlockSpec((tm, D_MODEL), row), pl.BlockSpec((tm, D_MODEL), row),
                  pl.BlockSpec((1, D_MODEL), full), pl.BlockSpec((1, D_MODEL), full)],
        out_specs=pl.BlockSpec((tm, D_MODEL), row),
        out_shape=jax.ShapeDtypeStruct((t, D_MODEL), F32),
        compiler_params=_cparams(("parallel",)),
        name="ln2",
    )(x, f, g, b)


S5_CHUNK = 16


def _s5_body(nb, n1, u_ref, t_ref, w_ref, v_ref, lam_ref, o_ref, h_ref, e_ref):
    u0 = u_ref[0]
    u1 = u_ref[1]
    h_ref[...] = _dot(jnp.concatenate([u0, u1], axis=1), w_ref[...])
    lam = lam_ref[...]
    zero = jnp.zeros((nb, LANES), F32)

    def scan(base, lr, li, reverse):
        def step(k, carry):
            er, ei = carry
            n = (n1 - 1 - k) if reverse else k
            r0 = n * nb
            if nb % 8 == 0:
                r0 = pl.multiple_of(r0, 8)
            rows = pl.ds(r0, nb)
            e_ref[rows, base:base + LANES] = er
            e_ref[rows, base + LANES:base + 2 * LANES] = ei
            hr = h_ref[rows, base:base + LANES]
            hi = h_ref[rows, base + LANES:base + 2 * LANES]
            return lr * er - li * ei + hr, lr * ei + li * er + hi
        lax.fori_loop(0, n1, step, (zero, zero))

    scan(0, lam[0:1], lam[1:2], False)
    scan(2 * LANES, lam[2:3], lam[3:4], True)
    yi = _dot(e_ref[...].astype(BF16), v_ref[...])
    o_ref[0] = (_dot(u0, t_ref[0]) + yi[:, :256]).astype(BF16)
    o_ref[1] = (_dot(u1, t_ref[1]) + yi[:, 256:]).astype(BF16)


def _s5_core(u, sp, nb, n1):
    g, r, _ = u.shape
    pair = lambda i: (i, 0, 0)
    return pl.pallas_call(
        functools.partial(_s5_body, nb, n1),
        grid=(g // 2,),
        in_specs=[pl.BlockSpec((2, r, 256), pair),
                  pl.BlockSpec((2, 256, 256), pair),
                  pl.BlockSpec((None, 512, 512), pair),
                  pl.BlockSpec((None, 512, 512), pair),
                  pl.BlockSpec((None, 8, LANES), pair)],
        out_specs=pl.BlockSpec((2, r, 256), pair),
        out_shape=jax.ShapeDtypeStruct((g, r, 256), BF16),
        scratch_shapes=[pltpu.VMEM((r, 512), F32), pltpu.VMEM((r, 512), F32)],
        compiler_params=_cparams(("parallel",)),
        name="s5_core",
    )(u, sp['t'], sp['w'], sp['v'], sp['lam'])


def _s5_out_body(y_ref, w_ref, b_ref, o_ref):
    y = jax.nn.gelu(y_ref[...].astype(F32))
    z = _dot(y.astype(BF16), w_ref[...]) + b_ref[...]
    o_ref[...] = (y * jax.nn.sigmoid(z)).astype(BF16)


def _s5_out(y, w, bias):
    t = y.shape[0]
    tm = min(1024, t)
    row = lambda i: (i, 0)
    full = lambda i: (0, 0)
    return pl.pallas_call(
        _s5_out_body,
        grid=(t // tm,),
        in_specs=[pl.BlockSpec((tm, 256), row), pl.BlockSpec((256, 256), full), pl.BlockSpec((1, 256), full)],
        out_specs=pl.BlockSpec((tm, 256), row),
        out_shape=jax.ShapeDtypeStruct((t, 256), BF16),
        compiler_params=_cparams(("parallel",)),
        name="s5_out",
    )(y, w, bias)


def _s5(main, q, b, l):
    n1 = l // S5_CHUNK
    u = main[:, COL_U:COL_U + 256].reshape(b, n1, S5_CHUNK, SSM_GROUPS, SSM_GROUP_CH)
    u = u.transpose(3, 1, 0, 2, 4).reshape(SSM_GROUPS, n1 * b, 256)
    y = _s5_core(u, q['s5'], b, n1)
    y = y.reshape(SSM_GROUPS, n1, b, S5_CHUNK, SSM_GROUP_CH).transpose(2, 1, 3, 0, 4).reshape(b * l, 256)
    return _s5_out(y, q['ssm_glu_w'], q['ssm_glu_b'])


def _s5_prep(p):
    hp = lax.Precision.HIGHEST
    c = S5_CHUNK
    tau = jnp.arange(c + 1, dtype=F32)[:, None, None]
    ks, ws, vs, lams = [], [], [], []
    for d in (0, 1):
        lam_re = p['ssm_a_re'][d]
        lam_im = p['ssm_a_im'][d]
        dt = jnp.exp(p['ssm_log_dt'][d])[:, None]
        mag = jnp.exp(lam_re * dt)
        abar_re = mag * jnp.cos(lam_im * dt)
        abar_im = mag * jnp.sin(lam_im * dt)
        den = lam_re * lam_re + lam_im * lam_im
        coef_re = ((abar_re - 1.0) * lam_re + abar_im * lam_im) / den
        coef_im = (abar_im * lam_re - (abar_re - 1.0) * lam_im) / den
        br, bi = p['ssm_b_re'][d], p['ssm_b_im'][d]
        bbar_re = coef_re[..., None] * br - coef_im[..., None] * bi
        bbar_im = coef_re[..., None] * bi + coef_im[..., None] * br
        cr, ci = p['ssm_c_re'][d], p['ssm_c_im'][d]
        pm = jnp.exp(tau * (lam_re * dt)[None])
        pr = pm * jnp.cos(tau * (lam_im * dt)[None])
        pi = pm * jnp.sin(tau * (lam_im * dt)[None])
        m_re = cr[None] * pr[:, :, None, :] - ci[None] * pi[:, :, None, :]
        m_im = cr[None] * pi[:, :, None, :] + ci[None] * pr[:, :, None, :]
        k = (jnp.einsum('tghp,gpk->gtkh', m_re[:c], bbar_re, precision=hp)
             - jnp.einsum('tghp,gpk->gtkh', m_im[:c], bbar_im, precision=hp))
        ks.append(k)
        pw = jnp.arange(c - 1, -1, -1) if d == 0 else jnp.arange(c)
        w_re = pr[pw][:, :, None, :] * jnp.swapaxes(bbar_re, 1, 2)[None] - pi[pw][:, :, None, :] * jnp.swapaxes(bbar_im, 1, 2)[None]
        w_im = pr[pw][:, :, None, :] * jnp.swapaxes(bbar_im, 1, 2)[None] + pi[pw][:, :, None, :] * jnp.swapaxes(bbar_re, 1, 2)[None]
        ws.append((jnp.transpose(w_re, (1, 0, 2, 3)).reshape(SSM_GROUPS, 256, SSM_STATE),
                   jnp.transpose(w_im, (1, 0, 2, 3)).reshape(SSM_GROUPS, 256, SSM_STATE)))
        po = jnp.arange(1, c + 1) if d == 0 else jnp.arange(c, 0, -1)
        v_re = m_re[po]
        v_im = m_im[po]
        vs.append((jnp.transpose(v_re, (1, 3, 0, 2)).reshape(SSM_GROUPS, SSM_STATE, 256),
                   jnp.transpose(-v_im, (1, 3, 0, 2)).reshape(SSM_GROUPS, SSM_STATE, 256)))
        lams.append((pr[c], pi[c]))
    j = jnp.arange(c)[:, None]
    t = jnp.arange(c)[None, :]
    k0 = ks[0][:, jnp.clip(t - j, 0, c - 1)] * (t >= j)[None, :, :, None, None]
    k1 = ks[1][:, jnp.clip(j - t, 0, c - 1)] * (j >= t)[None, :, :, None, None]
    tm = jnp.transpose(k0 + k1, (0, 1, 3, 2, 4)).reshape(SSM_GROUPS, 256, 256)
    dd = jnp.tile(p['ssm_d'], (1, c))
    tm = tm + jnp.eye(256, dtype=F32)[None] * dd[:, None, :]
    npair = SSM_GROUPS // 2

    def pair_diag(x):
        g, r, cc = x.shape
        x = x.reshape(npair, 2, r, cc)
        z = jnp.zeros((npair, 2, r, 2, cc), F32)
        z = z.at[:, 0, :, 0, :].set(x[:, 0]).at[:, 1, :, 1, :].set(x[:, 1])
        return z.reshape(npair, 2 * r, 2 * cc)

    w = jnp.concatenate([pair_diag(ws[0][0]), pair_diag(ws[0][1]), pair_diag(ws[1][0]), pair_diag(ws[1][1])], axis=2)
    v = jnp.concatenate([pair_diag(vs[0][0]), pair_diag(vs[0][1]), pair_diag(vs[1][0]), pair_diag(vs[1][1])], axis=1)
    lam = jnp.stack([lams[0][0], lams[0][1], lams[1][0], lams[1][1]], axis=1)
    lam = lam.reshape(npair, 2, 4, SSM_STATE).transpose(0, 2, 1, 3).reshape(npair, 4, LANES)
    lam = jnp.concatenate([lam, jnp.zeros((npair, 4, LANES), F32)], axis=1)
    return {'t': tm.astype(BF16), 'w': w.astype(BF16), 'v': v.astype(BF16), 'lam': lam}


GDN_HALO = 16


def _gdn_prep_body(nt, q_ref, qp_ref, qn_ref, k_ref, kp_ref, kn_ref, v_ref, vp_ref, vn_ref,
                   gate_ref, cw_ref, al_ref, dtb_ref, bd_ref, qo_ref, ko_ref, vo_ref, gb_ref):
    i = pl.program_id(1)
    cw = cw_ref[...]
    tl = q_ref.shape[0]

    def conv(cur_ref, prev_ref, next_ref, col):
        prev = jnp.where(i > 0, prev_ref[...].astype(F32), 0.0)
        nxt = jnp.where(i < nt - 1, next_ref[...].astype(F32), 0.0)
        xe = jnp.concatenate([prev, cur_ref[...].astype(F32), nxt], axis=0)
        n = xe.shape[0]
        acc = jnp.zeros((tl, 256), F32)
        for tap in range(5):
            s = tap - 2
            sh = xe if s == 0 else pltpu.roll(xe, (-s) % n, 0)
            acc = acc + sh[GDN_HALO:GDN_HALO + tl] * cw[tap:tap + 1, col:col + 256]
        return acc * jax.nn.sigmoid(acc)

    bd = bd_ref[...]

    def l2n(x):
        return x * lax.rsqrt(_head_mean_sq(x, bd) * HEAD_DIM + NORM_EPS)

    qo_ref[...] = l2n(conv(q_ref, qp_ref, qn_ref, 0)).astype(BF16)
    ko_ref[...] = l2n(conv(k_ref, kp_ref, kn_ref, 256)).astype(BF16)
    vo_ref[...] = conv(v_ref, vp_ref, vn_ref, 512).astype(BF16)
    gt = gate_ref[...]
    lane = lax.broadcasted_iota(jnp.int32, gt.shape, 1)
    x = gt + dtb_ref[...]
    softplus = jnp.maximum(x, 0.0) + jnp.log(1.0 + jnp.exp(-jnp.abs(x)))
    gb_ref[...] = jnp.where(lane < 8, -al_ref[...] * softplus, jnp.where(lane < 16, jax.nn.sigmoid(gt), 0.0))


def _gdn_prep(main, gate, q, bd, b, l):
    tl = min(256, l)
    nt = l // tl
    hb = tl // GDN_HALO
    nh = l // GDN_HALO

    def cur(col):
        return pl.BlockSpec((tl, 256), lambda bi, i: (bi * nt + i, col // 256))

    def prev(col):
        return pl.BlockSpec((GDN_HALO, 256), lambda bi, i: (bi * nh + jnp.maximum(i * hb - 1, 0), col // 256))

    def nxt(col):
        return pl.BlockSpec((GDN_HALO, 256), lambda bi, i: (bi * nh + jnp.minimum((i + 1) * hb, nh - 1), col // 256))

    full = lambda bi, i: (0, 0)
    row = lambda bi, i: (bi * nt + i, 0)
    specs = []
    for col in (COL_CQ, COL_CK, COL_CV):
        specs += [cur(col), prev(col), nxt(col)]
    specs += [pl.BlockSpec((tl, GATE_W), row), pl.BlockSpec((8, 768), full), pl.BlockSpec((1, LANES), full),
              pl.BlockSpec((1, LANES), full), pl.BlockSpec((256, 256), full)]
    return pl.pallas_call(
        functools.partial(_gdn_prep_body, nt),
        grid=(b, nt),
        in_specs=specs,
        out_specs=[pl.BlockSpec((tl, 256), row)] * 3 + [pl.BlockSpec((tl, GATE_W), row)],
        out_shape=[jax.ShapeDtypeStruct((b * l, 256), BF16)] * 3 + [jax.ShapeDtypeStruct((b * l, GATE_W), F32)],
        compiler_params=_cparams(("parallel", "parallel")),
        name="gdn_prep",
    )(*([main] * 9), gate, q['gdn_conv_w'], q['gdn_al'], q['gdn_dtb'], bd)


def _gdn_chunk_step(chains):
    c = GDN_CHUNK
    r256 = lax.broadcasted_iota(jnp.int32, (256, 256), 0)
    c256 = lax.broadcasted_iota(jnp.int32, (256, 256), 1)
    blockmask = (r256 // c) == (c256 // c)
    rl = lax.broadcasted_iota(jnp.int32, (LANES, 256), 0)
    cl = lax.broadcasted_iota(jnp.int32, (LANES, 256), 1)
    ri = lax.broadcasted_iota(jnp.int32, (c, c), 0)
    ci = lax.broadcasted_iota(jnp.int32, (c, c), 1)
    i_s = lax.broadcasted_iota(jnp.int32, (c, 256), 0)
    j_s = lax.broadcasted_iota(jnp.int32, (c, 256), 1) % c
    ones = jnp.ones((c, c), BF16)
    eye = (i_s == j_s).astype(F32)
    scale = HEAD_DIM ** -0.5

    def bdv(y):
        return jnp.where(blockmask, jnp.concatenate([y, y, y, y], axis=0), 0.0).astype(BF16)

    def hilo(x):
        hi = x.astype(BF16)
        return hi, (x - hi.astype(F32)).astype(BF16)

    st = []
    for dirn, qb, kb16, vb, gb, s_ref in chains:
        rev = dirn == 1
        e_g = (rl == dirn * 4 + cl // c).astype(BF16)
        e_b = (rl == 8 + dirn * 4 + cl // c).astype(BF16)
        ghi, glo = hilo(gb)
        st.append(dict(rev=rev, q=qb.astype(F32), k=kb16.astype(F32), v=vb.astype(F32), s_ref=s_ref,
                       g=_dot(ghi, e_g) + _dot(glo, e_g), beta=_dot(ghi, e_b) + _dot(glo, e_b),
                       tri=((ci >= ri) if rev else (ci <= ri)).astype(BF16),
                       allowed=(j_s >= i_s) if rev else (j_s <= i_s),
                       strict=(j_s > i_s) if rev else (j_s < i_s)))
    for d in st:
        ghi, glo = hilo(d['g'])
        d['gc'] = _dot(d['tri'], ghi) + _dot(d['tri'], glo)
    for d in st:
        zhi, zlo = hilo(jnp.where(i_s == j_s, d['gc'], 0.0))
        d['gct'] = _dot(ones, zhi) + _dot(ones, zlo)
    for d in st:
        d['decay'] = jnp.exp(jnp.where(d['allowed'], d['gc'] - d['gct'], -jnp.inf))
        d['eg'] = jnp.exp(d['gc'])
        d['kbeta'] = d['k'] * d['beta']
    for d in st:
        kk_qk = _dot_nt(jnp.concatenate([d['kbeta'], d['q'] * scale], axis=0).astype(BF16), bdv(d['k']))
        d['p'] = -jnp.where(d['strict'], kk_qk[:c] * d['decay'], 0.0)
        d['intra'] = jnp.where(d['allowed'], kk_qk[c:] * d['decay'], 0.0)
        d['t'] = eye + d['p']
    for _ in range(5):
        for d in st:
            d['p'] = _dot(d['p'].astype(BF16), bdv(d['p']))
        for d in st:
            d['t'] = d['t'] + _dot(d['t'].astype(BF16), bdv(d['p']))
    for d in st:
        t16 = d['t'].astype(BF16)
        d['u'] = _dot(t16, bdv(d['v'] * d['beta']))
        d['w'] = _dot(t16, bdv(d['kbeta'] * d['eg']))
    for d in st:
        d['s'] = d['s_ref'][...]
        d['ws_qs'] = _dot(jnp.concatenate([d['w'], d['q'] * scale * d['eg']], axis=0).astype(BF16),
                          d['s'].astype(BF16))
    for d in st:
        d['v_new'] = d['u'] - d['ws_qs'][:c]
        d['o'] = d['ws_qs'][c:] + _dot(d['intra'].astype(BF16), bdv(d['v_new']))
    for d in st:
        last = 0 if d['rev'] else c - 1
        g_last = d['gc'][last:last + 1, :]
        kg = (d['k'] * jnp.exp(g_last - d['gc'])).astype(BF16)
        upd = lax.dot_general(kg, d['v_new'].astype(BF16), (((0,), (0,)), ((), ())), preferred_element_type=F32)
        d['s_ref'][...] = d['s'] * jnp.exp(g_last) + jnp.where(blockmask, upd, 0.0)
    return [d['o'] for d in st]


GDN_SEQS = 2


def _gdn_chunk_body(qf_ref, kf_ref, vf_ref, gf_ref, qb_ref, kb_ref, vb_ref, gb_ref, of_ref, ob_ref, s_ref):
    @pl.when(pl.program_id(1) == 0)
    def _():
        s_ref[...] = jnp.zeros(s_ref.shape, F32)

    chains = []
    for j in range(qf_ref.shape[0]):
        chains.append((0, qf_ref[j], kf_ref[j], vf_ref[j], gf_ref[j], s_ref.at[j, 0]))
        chains.append((1, qb_ref[j], kb_ref[j], vb_ref[j], gb_ref[j], s_ref.at[j, 1]))
    outs = _gdn_chunk_step(chains)
    for j in range(qf_ref.shape[0]):
        of_ref[j] = outs[2 * j].astype(BF16)
        ob_ref[j] = outs[2 * j + 1].astype(BF16)


def _gdn_chunks(qn, kn, vs, gb, b, l):
    c = GDN_CHUNK
    n = l // c
    nseq = GDN_SEQS if b % GDN_SEQS == 0 else 1
    fwd = lambda bi, i: (bi, i, 0)
    bwd = lambda bi, i: (bi, n - 1 - i, 0)
    blk = lambda m: pl.BlockSpec((nseq, c, 256), m)
    gblk = lambda m: pl.BlockSpec((nseq, c, GATE_W), m)
    qn, kn, vs = (a.reshape(b, l, 256) for a in (qn, kn, vs))
    gb = gb.reshape(b, l, GATE_W)
    of, ob = pl.pallas_call(
        _gdn_chunk_body,
        grid=(b // nseq, n),
        in_specs=[blk(fwd), blk(fwd), blk(fwd), gblk(fwd), blk(bwd), blk(bwd), blk(bwd), gblk(bwd)],
        out_specs=[blk(fwd), blk(bwd)],
        out_shape=[jax.ShapeDtypeStruct((b, l, 256), BF16)] * 2,
        scratch_shapes=[pltpu.VMEM((nseq, 2, 256, 256), F32)],
        compiler_params=_cparams(("parallel", "arbitrary")),
        name="gdn_chunks",
    )(qn, kn, vs, gb, qn, kn, vs, gb)
    return of.reshape(b * l, 256), ob.reshape(b * l, 256)


def _gdn_out_body(of_ref, ob_ref, z_ref, g_ref, bd_ref, o_ref):
    o = of_ref[...].astype(F32) + ob_ref[...].astype(F32)
    z = z_ref[...].astype(F32)
    ms = _head_mean_sq(o, bd_ref[...])
    o_ref[...] = (o * lax.rsqrt(ms + NORM_EPS) * g_ref[...] * (z * jax.nn.sigmoid(z))).astype(BF16)


def _gdn_out(of, ob, main, g, bd):
    t = of.shape[0]
    tm = min(1024, t)
    row = lambda i: (i, 0)
    full = lambda i: (0, 0)
    return pl.pallas_call(
        _gdn_out_body,
        grid=(t // tm,),
        in_specs=[pl.BlockSpec((tm, 256), row), pl.BlockSpec((tm, 256), row),
                  pl.BlockSpec((tm, 256), lambda i: (i, COL_CZ // 256)),
                  pl.BlockSpec((1, 256), full), pl.BlockSpec((256, 256), full)],
        out_specs=pl.BlockSpec((tm, 256), row),
        out_shape=jax.ShapeDtypeStruct((t, 256), BF16),
        compiler_params=_cparams(("parallel",)),
        name="gdn_out",
    )(of, ob, main, g, bd)


def _gdn(main, gate, q, bd, b, l):
    qn, kn, vs, gb = _gdn_prep(main, gate, q, bd, b, l)
    of, ob = _gdn_chunks(qn, kn, vs, gb, b, l)
    return _gdn_out(of, ob, main, q['gdn_o_norm'], bd)


def _moe(x1, x1b, top_idx, gates, p):
    t = x1.shape[0]
    m = t * TOP_K
    flat_e = top_idx.reshape(-1)
    order = jnp.argsort(flat_e, stable=True).astype(jnp.int32)
    inv = jnp.argsort(order).astype(jnp.int32)
    sorted_e = flat_e[order]
    experts = jnp.arange(N_EXPERTS, dtype=jnp.int32)
    counts = jnp.sum((flat_e[:, None] == experts[None, :]).astype(jnp.int32), axis=0)
    padded = (counts + MOE_BLOCK - 1) // MOE_BLOCK * MOE_BLOCK
    group_start = jnp.cumsum(counts) - counts
    padded_end = jnp.cumsum(padded)
    padded_start = padded_end - padded
    dest = (padded_start[sorted_e] + jnp.arange(m, dtype=jnp.int32) - group_start[sorted_e]).astype(jnp.int32)
    n_blocks = -(-m // MOE_BLOCK) + N_EXPERTS
    n_slots = n_blocks * MOE_BLOCK
    block_start = jnp.arange(n_blocks, dtype=jnp.int32) * MOE_BLOCK
    block_expert = jnp.minimum(jnp.sum((padded_end[None, :] <= block_start[:, None]).astype(jnp.int32), axis=1),
                               N_EXPERTS - 1).astype(jnp.int32)
    n_used = (padded_end[-1] // MOE_BLOCK).astype(jnp.int32).reshape(1)
    slot = jnp.arange(n_slots, dtype=jnp.int32)
    slot_e = jnp.repeat(block_expert, MOE_BLOCK)
    rank = slot - padded_start[slot_e]
    src = jnp.clip(group_start[slot_e] + rank, 0, m - 1)
    slot_token = jnp.where(rank < counts[slot_e], order[src] // TOP_K, 0).astype(jnp.int32)
    xb = x1b[slot_token]
    yb = _ffn(xb, block_expert, n_used, p['moe_w1'], p['moe_b1'], p['moe_w2'], p['moe_b2'])
    dest_tk = dest[inv].reshape(t, TOP_K)
    return jnp.sum(yb[dest_tk].astype(F32) * gates[..., None], axis=1)


def _block_diag_mean(width, group):
    idx = np.arange(width)
    return jnp.asarray((idx[:, None] // group == idx[None, :] // group) / group, BF16)


def _axial_tables(l):
    rows = l // GRID_W
    row_pos = np.repeat(np.arange(rows), GRID_W).astype(np.float64)
    col_pos = np.tile(np.arange(GRID_W), rows).astype(np.float64)
    lane = np.arange(LANES)
    d = lane % HEAD_DIM
    e = d % 32
    f = e % 16
    inv = (AXIAL_THETA ** (-(np.arange(0, 32, 2, dtype=np.float32)) / 32)).astype(np.float32)
    pos = np.where((d // 32)[None, :] == 0, row_pos[:, None], col_pos[:, None]).astype(np.float32)
    ang = pos * inv[f][None, :]
    sign = np.where(e < 16, -1.0, 1.0)[None, :]
    return jnp.asarray(np.cos(ang), F32), jnp.asarray(np.sin(ang) * sign, F32)


def _diff_tables(l):
    lane = np.arange(256)
    e = lane % 32
    f = e % 4
    inv = (ROPE_THETA ** (-(np.arange(0, PARTIAL_ROPE_DIMS, 2, dtype=np.float32)) / PARTIAL_ROPE_DIMS)).astype(np.float32)
    ang = np.arange(l, dtype=np.float32)[:, None] * inv[f][None, :]
    roped = (e < PARTIAL_ROPE_DIMS)[None, :]
    sign = np.where(e < 4, -1.0, 1.0)[None, :]
    c = np.where(roped, np.cos(ang), 1.0)
    s = np.where(roped, np.sin(ang) * sign, 0.0)
    return jnp.asarray(c, F32), jnp.asarray(s, F32)


def _prep_layer(params, layer, w1_all):
    p = {name: arr[layer] for name, arr in params.items()}
    w_in = p['w_in']
    w_in = jnp.concatenate([w_in[:, :ORIG_CA], w_in[:, ORIG_CA + 16:], w_in[:, ORIG_CA:ORIG_CA + 16],
                            jnp.zeros((D_MODEL, GATE_W - 16), F32)], axis=1)
    q = dict(p)
    q['w_in'] = w_in.astype(BF16)
    q['w_out'] = p['w_out'].astype(BF16)
    q['router_w'] = jnp.concatenate([p['router_w'], jnp.zeros((D_MODEL, LANES - N_EXPERTS), F32)], axis=1).astype(BF16)
    q['router_b'] = jnp.concatenate([p['router_b'], jnp.full((LANES - N_EXPERTS,), -1e30, F32)])[None, :]
    q['moe_w1'] = w1_all[layer]
    b1 = p['moe_b1']
    q['moe_b1'] = b1.reshape(b1.shape[0], -1, LANES, 2).transpose(0, 1, 3, 2).reshape(b1.shape[0], 1, -1)
    q['moe_w2'] = p['moe_w2'].astype(BF16)
    q['moe_b2'] = p['moe_b2'][:, None, :]
    for name in ('ln1_g', 'ln1_b', 'ln2_g', 'ln2_b'):
        q[name] = p[name][None, :]
    q['gqa_q_norm'] = jnp.tile(p['gqa_q_norm'], 4)[None, :]
    q['gqa_k_norm'] = jnp.tile(p['gqa_k_norm'], 2)[None, :]
    lambda_init = 0.8 - 0.6 * math.exp(-0.3 * layer)
    q['diff_subln'] = (jnp.tile(p['diff_subln'], 4) * (1.0 - lambda_init))[None, :]
    lam = (jnp.exp(jnp.sum(p['diff_lambda_q1'] * p['diff_lambda_k1']))
           - jnp.exp(jnp.sum(p['diff_lambda_q2'] * p['diff_lambda_k2'])) + lambda_init)
    q['diff_lam'] = lam.reshape(1).astype(F32)
    q['s5'] = _s5_prep(p)
    q['ssm_glu_w'] = p['ssm_glu_w'].astype(BF16)
    q['ssm_glu_b'] = p['ssm_glu_b'][None, :]
    q['gdn_conv_w'] = jnp.concatenate([p['gdn_conv_w'], jnp.zeros((3, 768), F32)], axis=0)
    pad8 = lambda x: jnp.concatenate([x.reshape(-1), jnp.zeros((LANES - 8,), F32)])[None, :]
    q['gdn_al'] = pad8(jnp.exp(p['gdn_a_log']))
    q['gdn_dtb'] = pad8(p['gdn_dt_bias'])
    q['gdn_o_norm'] = jnp.tile(p['gdn_o_norm'], 4)[None, :]
    return q


def _mixers(main, gate, q, tabs, b, l):
    a_out = _s5(main, q, b, l)
    b_out = _gqa(main, tabs['axial'], q['gqa_q_norm'], q['gqa_k_norm'], tabs['bd64'], b, l)
    c_out = _gdn(main, gate, q, tabs['bd64'], b, l)
    d_out = _diff(main, q['diff_lam'], tabs['diff'], q['diff_subln'], tabs['bd64'], b, l)
    return a_out, b_out, c_out, d_out


def _layer(x, q, tabs, b, l, alpha):
    main, gate = _inproj(x, q['w_in'])
    pieces = _mixers(main, gate, q, tabs, b, l)
    x1, x1b, idx, gates = _outproj(pieces, q['w_out'], x, q['ln1_g'], q['ln1_b'],
                                   q['router_w'], q['router_b'], alpha)
    ffn = _moe(x1, x1b, idx[:, :TOP_K], gates[:, :TOP_K], q)
    return _ln2(x1, ffn, q['ln2_g'], q['ln2_b'], alpha)


def _trunk(x, layers, alpha):
    b, l, _ = x.shape
    tabs = {'axial': _axial_tables(l), 'diff': _diff_tables(l), 'bd64': _block_diag_mean(256, HEAD_DIM)}
    h = x.reshape(b * l, D_MODEL)
    for q in layers:
        h = _layer(h, q, tabs, b, l, alpha)
    return h.reshape(b, l, D_MODEL)


def kernel(x_prompt, x_sample, w_in, w_out, ssm_a_re, ssm_a_im, ssm_log_dt, ssm_b_re, ssm_b_im, ssm_c_re, ssm_c_im, ssm_d, ssm_glu_w, ssm_glu_b, gqa_q_norm, gqa_k_norm, gdn_conv_w, gdn_a_log, gdn_dt_bias, gdn_o_norm, diff_lambda_q1, diff_lambda_k1, diff_lambda_q2, diff_lambda_k2, diff_subln, router_w, router_b, moe_w1, moe_b1, moe_w2, moe_b2, ln1_g, ln1_b, ln2_g, ln2_b):
    params = {
        'w_in': w_in, 'w_out': w_out,
        'ssm_a_re': ssm_a_re, 'ssm_a_im': ssm_a_im, 'ssm_log_dt': ssm_log_dt,
        'ssm_b_re': ssm_b_re, 'ssm_b_im': ssm_b_im, 'ssm_c_re': ssm_c_re, 'ssm_c_im': ssm_c_im,
        'ssm_d': ssm_d, 'ssm_glu_w': ssm_glu_w, 'ssm_glu_b': ssm_glu_b,
        'gqa_q_norm': gqa_q_norm, 'gqa_k_norm': gqa_k_norm,
        'gdn_conv_w': gdn_conv_w, 'gdn_a_log': gdn_a_log, 'gdn_dt_bias': gdn_dt_bias, 'gdn_o_norm': gdn_o_norm,
        'diff_lambda_q1': diff_lambda_q1, 'diff_lambda_k1': diff_lambda_k1,
        'diff_lambda_q2': diff_lambda_q2, 'diff_lambda_k2': diff_lambda_k2, 'diff_subln': diff_subln,
        'router_w': router_w, 'router_b': router_b,
        'moe_w1': moe_w1, 'moe_b1': moe_b1, 'moe_w2': moe_w2, 'moe_b2': moe_b2,
        'ln1_g': ln1_g, 'ln1_b': ln1_b, 'ln2_g': ln2_g, 'ln2_b': ln2_b,
    }
    depth = w_in.shape[0]
    alpha = (2.0 * depth) ** 0.25
    w1_all = _w1_prep(moe_w1.reshape((-1,) + moe_w1.shape[2:])).reshape(moe_w1.shape)
    layers = [_prep_layer(params, layer, w1_all) for layer in range(depth)]
    return (_trunk(x_prompt, layers, alpha), _trunk(x_sample, layers, alpha))
```

```python
import functools
import math

import jax
import jax.numpy as jnp
import numpy as np
from jax import lax
from jax.experimental import pallas as pl
from jax.experimental.pallas import tpu as pltpu

F32 = jnp.float32
BF16 = jnp.bfloat16

D_MODEL = 1024
GROUP_WIDTH = 256
HEAD_DIM = 64
SSM_GROUPS = 16
SSM_GROUP_CH = 16
SSM_STATE = 64
GDN_HEADS = 4
GDN_CHUNK = 64
DIFF_SUB_DIM = 32
PARTIAL_ROPE_DIMS = 8
AXIAL_THETA = 10000.0
ROPE_THETA = 500000.0
GRID_W = 64
N_EXPERTS = 32
TOP_K = 4
SWIGLU_ALPHA = 1.702
SWIGLU_LIMIT = 7.0
NORM_EPS = 1e-6
LN_EPS = 1e-5
LOG2E = math.log2(math.e)

LANES = 128
VMEM_LIMIT = 56 * 1024 * 1024

COL_U, COL_GQ, COL_GK, COL_GV = 0, 256, 512, 640
COL_CQ, COL_CK, COL_CV, COL_CZ = 768, 1024, 1280, 1536
COL_DQ, COL_DK, COL_DV = 1792, 2048, 2304
MAIN_W = 2560
GATE_W = 128
ORIG_CA = 1792


def _cparams(sem):
    return pltpu.CompilerParams(dimension_semantics=sem, vmem_limit_bytes=VMEM_LIMIT)


def _dot(a, b):
    return jnp.dot(a, b, preferred_element_type=F32)


def _dot_nt(a, b):
    return lax.dot_general(a, b, (((1,), (1,)), ((), ())), preferred_element_type=F32)


def _split_dot(x, m):
    hi = x.astype(BF16)
    lo = (x - hi.astype(F32)).astype(BF16)
    return _dot(hi, m) + _dot(lo, m)


def _inproj_body(x_ref, w_ref, main_ref, gate_ref):
    x = x_ref[...].astype(BF16)
    step = 640
    for c in range(MAIN_W // step):
        main_ref[:, c * step:(c + 1) * step] = _dot(x, w_ref[:, c * step:(c + 1) * step]).astype(BF16)
    gate_ref[...] = _dot(x, w_ref[:, MAIN_W:])


def _inproj(x, w):
    t = x.shape[0]
    tm = min(512, t)
    return pl.pallas_call(
        _inproj_body,
        grid=(t // tm,),
        in_specs=[pl.BlockSpec((tm, D_MODEL), lambda i: (i, 0)),
                  pl.BlockSpec((D_MODEL, MAIN_W + GATE_W), lambda i: (0, 0))],
        out_specs=[pl.BlockSpec((tm, MAIN_W), lambda i: (i, 0)),
                   pl.BlockSpec((tm, GATE_W), lambda i: (i, 0))],
        out_shape=[jax.ShapeDtypeStruct((t, MAIN_W), BF16),
                   jax.ShapeDtypeStruct((t, GATE_W), F32)],
        compiler_params=_cparams(("parallel",)),
        name="inproj",
    )(x, w)


def _head_mean_sq(xf, bd):
    return _split_dot(xf * xf, bd)


def _rope_lanes(x, c, s, half):
    n = x.shape[-1]
    lane = lax.broadcasted_iota(jnp.int32, x.shape, 1)
    first = (lane % (2 * half)) < half
    swapped = jnp.where(first, pltpu.roll(x, n - half, 1), pltpu.roll(x, half, 1))
    return x * c + swapped * s


def _gqa_body(q_ref, k_ref, v_ref, cq_ref, sq_ref, ck_ref, sk_ref, gq_ref, gk_ref, bd_ref,
              o_ref, kdup_ref, vaug_ref):
    i = pl.program_id(1)
    bd = bd_ref[...]

    @pl.when(i == 0)
    def _():
        kf = k_ref[...].astype(F32)
        ms = _head_mean_sq(kf, bd[:LANES, :LANES])
        kn = kf * lax.rsqrt(ms + NORM_EPS) * gk_ref[...]
        kn = _rope_lanes(kn, ck_ref[...], sk_ref[...], 16)
        lane = lax.broadcasted_iota(jnp.int32, kn.shape, 1)
        sw = pltpu.roll(kn, 64, 1)
        kdup_ref[0] = jnp.where(lane < 64, kn, sw).astype(BF16)
        kdup_ref[1] = jnp.where(lane < 64, sw, kn).astype(BF16)
        v = v_ref[...]
        one = jnp.ones(v.shape, BF16)
        vaug_ref[0] = jnp.where(lane < 64, v, one)
        vaug_ref[1] = jnp.where(lane < 64, one, v)

    qf = q_ref[...].astype(F32)
    ms = _head_mean_sq(qf, bd)
    qn = qf * lax.rsqrt(ms + NORM_EPS) * gq_ref[...]
    cq = jnp.concatenate([cq_ref[...], cq_ref[...]], axis=1)
    sq = jnp.concatenate([sq_ref[...], sq_ref[...]], axis=1)
    qn = (_rope_lanes(qn, cq, sq, 16) * (HEAD_DIM ** -0.5 * LOG2E)).astype(BF16)
    tq = qn.shape[0]
    lane = lax.broadcasted_iota(jnp.int32, (tq, LANES), 1)
    zero = jnp.zeros((tq, LANES), BF16)
    for h in range(2):
        qh = qn[:, h * LANES:(h + 1) * LANES]
        q2 = jnp.concatenate([jnp.where(lane < 64, qh, zero), jnp.where(lane < 64, zero, qh)], axis=0)
        s = _dot_nt(q2, kdup_ref[h])
        m = jnp.max(s, axis=-1, keepdims=True)
        p = jnp.exp2(s - m).astype(BF16)
        o2 = _dot(p, vaug_ref[h])
        o2 = o2 / pltpu.roll(o2, 64, 1)
        top, bot = o2[:tq], o2[tq:]
        if h == 0:
            oh = jnp.where(lane < 64, top, pltpu.roll(bot, 64, 1))
        else:
            oh = jnp.where(lane < 64, pltpu.roll(top, 64, 1), bot)
        o_ref[:, h * LANES:(h + 1) * LANES] = oh.astype(BF16)


def _gqa(main, tabs, gq, gk, bd, b, l):
    tq = min(256, l)
    nq = l // tq
    cq, sq = tabs
    full = lambda bi, i: (0, 0)
    return pl.pallas_call(
        _gqa_body,
        grid=(b, nq),
        in_specs=[pl.BlockSpec((tq, 256), lambda bi, i: (bi * nq + i, COL_GQ // 256)),
                  pl.BlockSpec((l, LANES), lambda bi, i: (bi, COL_GK // LANES)),
                  pl.BlockSpec((l, LANES), lambda bi, i: (bi, COL_GV // LANES)),
                  pl.BlockSpec((tq, LANES), lambda bi, i: (i, 0)),
                  pl.BlockSpec((tq, LANES), lambda bi, i: (i, 0)),
                  pl.BlockSpec((l, LANES), full),
                  pl.BlockSpec((l, LANES), full),
                  pl.BlockSpec((1, 256), full),
                  pl.BlockSpec((1, LANES), full),
                  pl.BlockSpec((256, 256), full)],
        out_specs=pl.BlockSpec((tq, 256), lambda bi, i: (bi * nq + i, 0)),
        out_shape=jax.ShapeDtypeStruct((b * l, 256), BF16),
        scratch_shapes=[pltpu.VMEM((2, l, LANES), BF16), pltpu.VMEM((2, l, LANES), BF16)],
        compiler_params=_cparams(("parallel", "arbitrary")),
        name="gqa",
    )(main, main, main, cq, sq, cq, sq, gq, gk, bd)


def _diff_body(lam_ref, q_ref, k_ref, v_ref, cq_ref, sq_ref, ck_ref, sk_ref, g_ref, bd_ref,
               o_ref, kr_ref, vaug_ref):
    i = pl.program_id(1)

    @pl.when(i == 0)
    def _():
        kr_ref[...] = _rope_lanes(k_ref[...].astype(F32), ck_ref[...], sk_ref[...], 4).astype(BF16)
        v = v_ref[...]
        vlane = lax.broadcasted_iota(jnp.int32, v.shape, 1)
        one = jnp.ones(v.shape, BF16)
        for h in range(4):
            vaug_ref[h] = jnp.where(vlane // 64 == h, v, one)

    lam = lam_ref[0]
    qr = _rope_lanes(q_ref[...].astype(F32), cq_ref[...], sq_ref[...], 4)
    qr = (qr * (DIFF_SUB_DIM ** -0.5 * LOG2E)).astype(BF16)
    tq = qr.shape[0]
    lane = lax.broadcasted_iota(jnp.int32, (tq, 256), 1)
    zero = jnp.zeros((tq, 256), BF16)
    kr = kr_ref[...]
    acc = jnp.zeros((tq, 256), F32)
    for h in range(4):
        q2 = jnp.concatenate([jnp.where(lane // 32 == 2 * h, qr, zero),
                              jnp.where(lane // 32 == 2 * h + 1, qr, zero)], axis=0)
        s = _dot_nt(q2, kr)
        m = jnp.max(s, axis=-1, keepdims=True)
        e = jnp.exp2(s - m).astype(BF16)
        o2 = _dot(e, vaug_ref[h])
        o2 = o2 / pltpu.roll(o2, 64, 1)
        o = o2[:tq] - lam * o2[tq:]
        acc = jnp.where(lane // 64 == h, o, acc)
    ms = _head_mean_sq(acc, bd_ref[...])
    o_ref[...] = (acc * lax.rsqrt(ms + NORM_EPS) * g_ref[...]).astype(BF16)


def _diff(main, lam, tabs, g, bd, b, l):
    tq = min(256, l)
    nq = l // tq
    c, s = tabs
    full = lambda bi, i, *_: (0, 0)
    once = pl.Buffered(1)
    grid_spec = pltpu.PrefetchScalarGridSpec(
        num_scalar_prefetch=1,
        grid=(b, nq),
        in_specs=[pl.BlockSpec((tq, 256), lambda bi, i, *_: (bi * nq + i, COL_DQ // 256)),
                  pl.BlockSpec((l, 256), lambda bi, i, *_: (bi, COL_DK // 256)),
                  pl.BlockSpec((l, 256), lambda bi, i, *_: (bi, COL_DV // 256)),
                  pl.BlockSpec((tq, 256), lambda bi, i, *_: (i, 0)),
                  pl.BlockSpec((tq, 256), lambda bi, i, *_: (i, 0)),
                  pl.BlockSpec((l, 256), full, pipeline_mode=once),
                  pl.BlockSpec((l, 256), full, pipeline_mode=once),
                  pl.BlockSpec((1, 256), full),
                  pl.BlockSpec((256, 256), full)],
        out_specs=pl.BlockSpec((tq, 256), lambda bi, i, *_: (bi * nq + i, 0)),
        scratch_shapes=[pltpu.VMEM((l, 256), BF16), pltpu.VMEM((4, l, 256), BF16)],
    )
    return pl.pallas_call(
        _diff_body,
        grid_spec=grid_spec,
        out_shape=jax.ShapeDtypeStruct((b * l, 256), BF16),
        compiler_params=_cparams(("parallel", "arbitrary")),
        name="diffattn",
    )(lam, main, main, main, c, s, c, s, g, bd)


def _layer_norm_rows(y, g, b):
    mu = jnp.mean(y, axis=-1, keepdims=True)
    d = y - mu
    var = jnp.mean(d * d, axis=-1, keepdims=True)
    return d * lax.rsqrt(var + LN_EPS) * g + b


def _outproj_body(alpha, a_ref, b_ref, c_ref, d_ref, w_ref, x_ref, g_ref, beta_ref, rw_ref, rb_ref,
                  x1_ref, x1b_ref, idx_ref, gate_ref):
    mixed = (_dot(a_ref[...], w_ref[0:256, :]) + _dot(b_ref[...], w_ref[256:512, :])
             + _dot(c_ref[...], w_ref[512:768, :]) + _dot(d_ref[...], w_ref[768:1024, :]))
    x1 = _layer_norm_rows(alpha * x_ref[...] + mixed, g_ref[...], beta_ref[...])
    x1_ref[...] = x1
    x1b = x1.astype(BF16)
    x1b_ref[...] = x1b
    logits = _dot(x1b, rw_ref[...]) + rb_ref[...]
    lane = lax.broadcasted_iota(jnp.int32, logits.shape, 1)
    vals, idxs = [], []
    for _ in range(TOP_K):
        m = jnp.max(logits, axis=-1, keepdims=True)
        ix = jnp.min(jnp.where(logits == m, lane, LANES), axis=-1, keepdims=True)
        vals.append(m)
        idxs.append(ix)
        logits = jnp.where(lane == ix, -jnp.inf, logits)
    es = [jnp.exp(vk - vals[0]) for vk in vals]
    tot = es[0] + es[1] + es[2] + es[3]
    idx_out = jnp.zeros(logits.shape, jnp.int32)
    gate_out = jnp.zeros(logits.shape, F32)
    for k in range(TOP_K):
        idx_out = jnp.where(lane == k, idxs[k], idx_out)
        gate_out = jnp.where(lane == k, es[k] / tot, gate_out)
    idx_ref[...] = idx_out
    gate_ref[...] = gate_out


def _outproj(pieces, w, x, g, beta, rw, rb, alpha):
    t = x.shape[0]
    tm = min(512, t)
    row = lambda i: (i, 0)
    full = lambda i: (0, 0)
    return pl.pallas_call(
        functools.partial(_outproj_body, alpha),
        grid=(t // tm,),
        in_specs=[pl.BlockSpec((tm, 256), row)] * 4 + [
            pl.BlockSpec((D_MODEL, D_MODEL), full),
            pl.BlockSpec((tm, D_MODEL), row),
            pl.BlockSpec((1, D_MODEL), full),
            pl.BlockSpec((1, D_MODEL), full),
            pl.BlockSpec((D_MODEL, LANES), full),
            pl.BlockSpec((1, LANES), full)],
        out_specs=[pl.BlockSpec((tm, D_MODEL), row), pl.BlockSpec((tm, D_MODEL), row),
                   pl.BlockSpec((tm, LANES), row), pl.BlockSpec((tm, LANES), row)],
        out_shape=[jax.ShapeDtypeStruct((t, D_MODEL), F32), jax.ShapeDtypeStruct((t, D_MODEL), BF16),
                   jax.ShapeDtypeStruct((t, LANES), jnp.int32), jax.ShapeDtypeStruct((t, LANES), F32)],
        compiler_params=_cparams(("parallel",)),
        name="outproj_ln_router",
    )(*pieces, w, x, g, beta, rw, rb)


MOE_BLOCK = 512


def _ffn_body(be_ref, nb_ref, x_ref, wg_ref, wl_ref, bg_ref, bl_ref, w2_ref, b2_ref, o_ref):
    i = pl.program_id(0)

    @pl.when(i < nb_ref[0])
    def _():
        x = x_ref[...]
        glu = jnp.minimum(_dot(x, wg_ref[...]) + bg_ref[...], SWIGLU_LIMIT)
        lin = jnp.clip(_dot(x, wl_ref[...]) + bl_ref[...], -SWIGLU_LIMIT, SWIGLU_LIMIT)
        act = (glu * jax.nn.sigmoid(SWIGLU_ALPHA * glu) * (lin + 1.0)).astype(BF16)
        o_ref[...] = (_dot(act, w2_ref[...]) + b2_ref[...]).astype(o_ref.dtype)

    @pl.when(i >= nb_ref[0])
    def _():
        o_ref[...] = jnp.zeros(o_ref.shape, o_ref.dtype)


def _ffn(xb, block_expert, n_used, wg, wl, bg, bl, w2, b2):
    n_slots = xb.shape[0]
    n_blocks = n_slots // MOE_BLOCK
    de = wg.shape[2]
    expert = lambda i, be, nb: (be[i], 0, 0)
    grid_spec = pltpu.PrefetchScalarGridSpec(
        num_scalar_prefetch=2,
        grid=(n_blocks,),
        in_specs=[pl.BlockSpec((MOE_BLOCK, D_MODEL), lambda i, be, nb: (i, 0)),
                  pl.BlockSpec((None, D_MODEL, de), expert),
                  pl.BlockSpec((None, D_MODEL, de), expert),
                  pl.BlockSpec((None, 1, de), expert),
                  pl.BlockSpec((None, 1, de), expert),
                  pl.BlockSpec((None, de, D_MODEL), expert),
                  pl.BlockSpec((None, 1, D_MODEL), expert)],
        out_specs=pl.BlockSpec((MOE_BLOCK, D_MODEL), lambda i, be, nb: (i, 0)),
    )
    return pl.pallas_call(
        _ffn_body,
        grid_spec=grid_spec,
        out_shape=jax.ShapeDtypeStruct((n_slots, D_MODEL), BF16),
        compiler_params=_cparams(("arbitrary",)),
        name="moe_ffn",
    )(block_expert, n_used, xb, wg, wl, bg, bl, w2, b2)


def _w1_prep_body(w_ref, pg_ref, pl_ref, g_ref, l_ref):
    w = w_ref[...].astype(BF16)
    g_ref[...] = _dot(w, pg_ref[...]).astype(BF16)
    l_ref[...] = _dot(w, pl_ref[...]).astype(BF16)


def _w1_prep(w1):
    ne, dm, de2 = w1.shape
    tn = 512
    r = np.arange(tn)[:, None]
    c = np.arange(tn // 2)[None, :]
    sel_g = jnp.asarray(r == 2 * c, BF16)
    sel_l = jnp.asarray(r == 2 * c + 1, BF16)
    full = lambda e, j: (0, 0)
    out = jax.ShapeDtypeStruct((ne, dm, de2 // 2), BF16)
    return pl.pallas_call(
        _w1_prep_body,
        grid=(ne, de2 // tn),
        in_specs=[pl.BlockSpec((None, dm, tn), lambda e, j: (e, 0, j)),
                  pl.BlockSpec((tn, tn // 2), full), pl.BlockSpec((tn, tn // 2), full)],
        out_specs=[pl.BlockSpec((None, dm, tn // 2), lambda e, j: (e, 0, j))] * 2,
        out_shape=[out, out],
        compiler_params=_cparams(("parallel", "parallel")),
        name="w1_prep",
    )(w1, sel_g, sel_l)


def _ln2_body(alpha, x_ref, f_ref, g_ref, b_ref, o_ref):
    o_ref[...] = _layer_norm_rows(alpha * x_ref[...] + f_ref[...], g_ref[...], b_ref[...])


def _ln2(x, f, g, b, alpha):
    t = x.shape[0]
    tm = min(512, t)
    row = lambda i: (i, 0)
    full = lambda i: (0, 0)
    return pl.pallas_call(
        functools.partial(_ln2_body, alpha),
        grid=(t // tm,),
        in_specs=[pl.BlockSpec((tm, D_MODEL), row), pl.BlockSpec((tm, D_MODEL), row),
                  pl.BlockSpec((1, D_MODEL), full), pl.BlockSpec((1, D_MODEL), full)],
        out_specs=pl.BlockSpec((tm, D_MODEL), row),
        out_shape=jax.ShapeDtypeStruct((t, D_MODEL), F32),
        compiler_params=_cparams(("parallel",)),
        name="ln2",
    )(x, f, g, b)


S5_CHUNK = 16


S5_SEL_PAD = (SSM_GROUPS - 1) * SSM_GROUP_CH
S5_ROWS = 256


def _s5_sel():
    c = np.arange(256)
    target = (c // 16) * 256 + c % 16
    r = np.arange(S5_SEL_PAD + 4096)[:, None] - S5_SEL_PAD
    return jnp.asarray(r == target[None, :], BF16)


def _s5_in_body(*refs):
    x_refs = refs[:S5_CHUNK]
    sel_ref, t_ref, w_ref, y_ref, h_ref = refs[S5_CHUNK:]
    xcat = jnp.concatenate([x[...] for x in x_refs], axis=1)
    us = []
    for g in range(SSM_GROUPS):
        start = S5_SEL_PAD - SSM_GROUP_CH * g
        u = _dot(xcat, sel_ref[start:start + 4096, :]).astype(BF16)
        y_ref[g] = _dot(u, t_ref[g]).astype(BF16)
        us.append(u)
    for i in range(SSM_GROUPS // 2):
        hp = _dot(jnp.concatenate([us[2 * i], us[2 * i + 1]], axis=1), w_ref[i])
        for c in range(4):
            h_ref[c, :, i * LANES:(i + 1) * LANES] = hp[:, c * LANES:(c + 1) * LANES]


def _s5_in(main2, sel, sp):
    t16 = main2.shape[0]
    rb = min(S5_ROWS, t16)
    tok = MAIN_W // 256
    full2 = lambda i: (0, 0)
    full3 = lambda i: (0, 0, 0)
    return pl.pallas_call(
        _s5_in_body,
        grid=(t16 // rb,),
        in_specs=[pl.BlockSpec((rb, 256), functools.partial(lambda t, i: (i, tok * t + COL_U // 256), t))
                  for t in range(S5_CHUNK)] + [
            pl.BlockSpec(sel.shape, full2),
            pl.BlockSpec((SSM_GROUPS, 256, 256), full3),
            pl.BlockSpec((SSM_GROUPS // 2, 512, 512), full3)],
        out_specs=[pl.BlockSpec((SSM_GROUPS, rb, 256), lambda i: (0, i, 0)),
                   pl.BlockSpec((4, rb, 1024), lambda i: (0, i, 0))],
        out_shape=[jax.ShapeDtypeStruct((SSM_GROUPS, t16, 256), BF16),
                   jax.ShapeDtypeStruct((4, t16, 1024), F32)],
        compiler_params=_cparams(("parallel",)),
        name="s5_in",
    )(*([main2] * S5_CHUNK), sel, sp['t'], sp['w'])


def _s5_scan_body(n1, h_ref, lam_ref, e_ref):
    lam = lam_ref[...]
    zero = jnp.zeros((1, 1024), F32)

    def sweep(base, reverse):
        lr, li = lam[base:base + 1], lam[base + 1:base + 2]

        def tile(k, carry):
            er, ei = carry
            r0 = pl.multiple_of(((n1 // 8 - 1 - k) if reverse else k) * 8, 8)
            hr = h_ref[base, pl.ds(r0, 8), :]
            hi = h_ref[base + 1, pl.ds(r0, 8), :]
            outs_r, outs_i = [None] * 8, [None] * 8
            for j in (range(7, -1, -1) if reverse else range(8)):
                outs_r[j], outs_i[j] = er, ei
                er, ei = (lr * er - li * ei + hr[j:j + 1], lr * ei + li * er + hi[j:j + 1])
            e_ref[base, pl.ds(r0, 8), :] = jnp.concatenate(outs_r, axis=0)
            e_ref[base + 1, pl.ds(r0, 8), :] = jnp.concatenate(outs_i, axis=0)
            return er, ei

        lax.fori_loop(0, n1 // 8, tile, (zero, zero))

    sweep(0, False)
    sweep(2, True)


def _s5_scan(h, lam, b, n1):
    return pl.pallas_call(
        functools.partial(_s5_scan_body, n1),
        grid=(b,),
        in_specs=[pl.BlockSpec((4, n1, 1024), lambda i: (0, i, 0)),
                  pl.BlockSpec((8, 1024), lambda i: (0, 0))],
        out_specs=pl.BlockSpec((4, n1, 1024), lambda i: (0, i, 0)),
        out_shape=jax.ShapeDtypeStruct(h.shape, F32),
        compiler_params=_cparams(("parallel",)),
        name="s5_scan",
    )(h, lam)


def _s5_fin_body(y_ref, e_ref, sel_ref, v_ref, w_ref, b_ref, o_ref):
    ys = []
    for i in range(SSM_GROUPS // 2):
        ep = jnp.concatenate([e_ref[c, :, i * LANES:(i + 1) * LANES] for c in range(4)], axis=1)
        yi = _dot(ep.astype(BF16), v_ref[i])
        ys.append((y_ref[2 * i].astype(F32) + yi[:, :256]).astype(BF16))
        ys.append((y_ref[2 * i + 1].astype(F32) + yi[:, 256:]).astype(BF16))
    ycat = jnp.concatenate(ys, axis=1)
    for t in range(S5_CHUNK):
        start = S5_SEL_PAD - SSM_GROUP_CH * t
        y = jax.nn.gelu(_dot(ycat, sel_ref[start:start + 4096, :]))
        z = _dot(y.astype(BF16), w_ref[...]) + b_ref[...]
        o_ref[:, t * 256:(t + 1) * 256] = (y * jax.nn.sigmoid(z)).astype(BF16)


def _s5_fin(y, e, sel, sp, w, bias):
    t16 = y.shape[1]
    rb = min(S5_ROWS, t16)
    full2 = lambda i: (0, 0)
    full3 = lambda i: (0, 0, 0)
    return pl.pallas_call(
        _s5_fin_body,
        grid=(t16 // rb,),
        in_specs=[pl.BlockSpec((SSM_GROUPS, rb, 256), lambda i: (0, i, 0)),
                  pl.BlockSpec((4, rb, 1024), lambda i: (0, i, 0)),
                  pl.BlockSpec(sel.shape, full2),
                  pl.BlockSpec((SSM_GROUPS // 2, 512, 512), full3),
                  pl.BlockSpec((256, 256), full2),
                  pl.BlockSpec((1, 256), full2)],
        out_specs=pl.BlockSpec((rb, S5_CHUNK * 256), lambda i: (i, 0)),
        out_shape=jax.ShapeDtypeStruct((t16, S5_CHUNK * 256), BF16),
        compiler_params=_cparams(("parallel",)),
        name="s5_fin",
    )(y, e, sel, sp['v'], w, bias)


def _s5(main, q, sel, b, l):
    n1 = l // S5_CHUNK
    main2 = main.reshape(b * n1, S5_CHUNK * MAIN_W)
    y, h = _s5_in(main2, sel, q['s5'])
    e = _s5_scan(h, q['s5']['lam'], b, n1)
    out = _s5_fin(y, e, sel, q['s5'], q['ssm_glu_w'], q['ssm_glu_b'])
    return out.reshape(b * l, 256)


def _s5_prep(p):
    hp = lax.Precision.HIGHEST
    c = S5_CHUNK
    tau = jnp.arange(c + 1, dtype=F32)[:, None, None]
    ks, ws, vs, lams = [], [], [], []
    for d in (0, 1):
        lam_re = p['ssm_a_re'][d]
        lam_im = p['ssm_a_im'][d]
        dt = jnp.exp(p['ssm_log_dt'][d])[:, None]
        mag = jnp.exp(lam_re * dt)
        abar_re = mag * jnp.cos(lam_im * dt)
        abar_im = mag * jnp.sin(lam_im * dt)
        den = lam_re * lam_re + lam_im * lam_im
        coef_re = ((abar_re - 1.0) * lam_re + abar_im * lam_im) / den
        coef_im = (abar_im * lam_re - (abar_re - 1.0) * lam_im) / den
        br, bi = p['ssm_b_re'][d], p['ssm_b_im'][d]
        bbar_re = coef_re[..., None] * br - coef_im[..., None] * bi
        bbar_im = coef_re[..., None] * bi + coef_im[..., None] * br
        cr, ci = p['ssm_c_re'][d], p['ssm_c_im'][d]
        pm = jnp.exp(tau * (lam_re * dt)[None])
        pr = pm * jnp.cos(tau * (lam_im * dt)[None])
        pi = pm * jnp.sin(tau * (lam_im * dt)[None])
        m_re = cr[None] * pr[:, :, None, :] - ci[None] * pi[:, :, None, :]
        m_im = cr[None] * pi[:, :, None, :] + ci[None] * pr[:, :, None, :]
        k = (jnp.einsum('tghp,gpk->gtkh', m_re[:c], bbar_re, precision=hp)
             - jnp.einsum('tghp,gpk->gtkh', m_im[:c], bbar_im, precision=hp))
        ks.append(k)
        pw = jnp.arange(c - 1, -1, -1) if d == 0 else jnp.arange(c)
        w_re = pr[pw][:, :, None, :] * jnp.swapaxes(bbar_re, 1, 2)[None] - pi[pw][:, :, None, :] * jnp.swapaxes(bbar_im, 1, 2)[None]
        w_im = pr[pw][:, :, None, :] * jnp.swapaxes(bbar_im, 1, 2)[None] + pi[pw][:, :, None, :] * jnp.swapaxes(bbar_re, 1, 2)[None]
        ws.append((jnp.transpose(w_re, (1, 0, 2, 3)).reshape(SSM_GROUPS, 256, SSM_STATE),
                   jnp.transpose(w_im, (1, 0, 2, 3)).reshape(SSM_GROUPS, 256, SSM_STATE)))
        po = jnp.arange(1, c + 1) if d == 0 else jnp.arange(c, 0, -1)
        v_re = m_re[po]
        v_im = m_im[po]
        vs.append((jnp.transpose(v_re, (1, 3, 0, 2)).reshape(SSM_GROUPS, SSM_STATE, 256),
                   jnp.transpose(-v_im, (1, 3, 0, 2)).reshape(SSM_GROUPS, SSM_STATE, 256)))
        lams.append((pr[c], pi[c]))
    j = jnp.arange(c)[:, None]
    t = jnp.arange(c)[None, :]
    k0 = ks[0][:, jnp.clip(t - j, 0, c - 1)] * (t >= j)[None, :, :, None, None]
    k1 = ks[1][:, jnp.clip(j - t, 0, c - 1)] * (j >= t)[None, :, :, None, None]
    tm = jnp.transpose(k0 + k1, (0, 1, 3, 2, 4)).reshape(SSM_GROUPS, 256, 256)
    dd = jnp.tile(p['ssm_d'], (1, c))
    tm = tm + jnp.eye(256, dtype=F32)[None] * dd[:, None, :]
    npair = SSM_GROUPS // 2

    def pair_diag(x):
        g, r, cc = x.shape
        x = x.reshape(npair, 2, r, cc)
        z = jnp.zeros((npair, 2, r, 2, cc), F32)
        z = z.at[:, 0, :, 0, :].set(x[:, 0]).at[:, 1, :, 1, :].set(x[:, 1])
        return z.reshape(npair, 2 * r, 2 * cc)

    w = jnp.concatenate([pair_diag(ws[0][0]), pair_diag(ws[0][1]), pair_diag(ws[1][0]), pair_diag(ws[1][1])], axis=2)
    v = jnp.concatenate([pair_diag(vs[0][0]), pair_diag(vs[0][1]), pair_diag(vs[1][0]), pair_diag(vs[1][1])], axis=1)
    lam = jnp.stack([lams[0][0], lams[0][1], lams[1][0], lams[1][1]], axis=0)
    lam = lam.reshape(4, SSM_GROUPS * SSM_STATE)
    lam = jnp.concatenate([lam, jnp.zeros_like(lam)], axis=0)
    return {'t': tm.astype(BF16), 'w': w.astype(BF16), 'v': v.astype(BF16), 'lam': lam}


GDN_HALO = 16


def _gdn_prep_body(nt, q_ref, qp_ref, qn_ref, k_ref, kp_ref, kn_ref, v_ref, vp_ref, vn_ref,
                   gate_ref, cw_ref, al_ref, dtb_ref, bd_ref, qo_ref, ko_ref, vo_ref, gb_ref):
    i = pl.program_id(1)
    cw = cw_ref[...]
    tl = q_ref.shape[0]

    def conv(cur_ref, prev_ref, next_ref, col):
        prev = jnp.where(i > 0, prev_ref[...].astype(F32), 0.0)
        nxt = jnp.where(i < nt - 1, next_ref[...].astype(F32), 0.0)
        xe = jnp.concatenate([prev, cur_ref[...].astype(F32), nxt], axis=0)
        n = xe.shape[0]
        acc = jnp.zeros((tl, 256), F32)
        for tap in range(5):
            s = tap - 2
            sh = xe if s == 0 else pltpu.roll(xe, (-s) % n, 0)
            acc = acc + sh[GDN_HALO:GDN_HALO + tl] * cw[tap:tap + 1, col:col + 256]
        return acc * jax.nn.sigmoid(acc)

    bd = bd_ref[...]

    def l2n(x):
        return x * lax.rsqrt(_head_mean_sq(x, bd) * HEAD_DIM + NORM_EPS)

    qo_ref[...] = l2n(conv(q_ref, qp_ref, qn_ref, 0)).astype(BF16)
    ko_ref[...] = l2n(conv(k_ref, kp_ref, kn_ref, 256)).astype(BF16)
    vo_ref[...] = conv(v_ref, vp_ref, vn_ref, 512).astype(BF16)
    gt = gate_ref[...]
    lane = lax.broadcasted_iota(jnp.int32, gt.shape, 1)
    x = gt + dtb_ref[...]
    softplus = jnp.maximum(x, 0.0) + jnp.log(1.0 + jnp.exp(-jnp.abs(x)))
    gb_ref[...] = jnp.where(lane < 8, -al_ref[...] * softplus, jnp.where(lane < 16, jax.nn.sigmoid(gt), 0.0))


def _gdn_prep(main, gate, q, bd, b, l):
    tl = min(256, l)
    nt = l // tl
    hb = tl // GDN_HALO
    nh = l // GDN_HALO

    def cur(col):
        return pl.BlockSpec((tl, 256), lambda bi, i: (bi * nt + i, col // 256))

    def prev(col):
        return pl.BlockSpec((GDN_HALO, 256), lambda bi, i: (bi * nh + jnp.maximum(i * hb - 1, 0), col // 256))

    def nxt(col):
        return pl.BlockSpec((GDN_HALO, 256), lambda bi, i: (bi * nh + jnp.minimum((i + 1) * hb, nh - 1), col // 256))

    full = lambda bi, i: (0, 0)
    row = lambda bi, i: (bi * nt + i, 0)
    specs = []
    for col in (COL_CQ, COL_CK, COL_CV):
        specs += [cur(col), prev(col), nxt(col)]
    specs += [pl.BlockSpec((tl, GATE_W), row), pl.BlockSpec((8, 768), full), pl.BlockSpec((1, LANES), full),
              pl.BlockSpec((1, LANES), full), pl.BlockSpec((256, 256), full)]
    return pl.pallas_call(
        functools.partial(_gdn_prep_body, nt),
        grid=(b, nt),
        in_specs=specs,
        out_specs=[pl.BlockSpec((tl, 256), row)] * 3 + [pl.BlockSpec((tl, GATE_W), row)],
        out_shape=[jax.ShapeDtypeStruct((b * l, 256), BF16)] * 3 + [jax.ShapeDtypeStruct((b * l, GATE_W), F32)],
        compiler_params=_cparams(("parallel", "parallel")),
        name="gdn_prep",
    )(*([main] * 9), gate, q['gdn_conv_w'], q['gdn_al'], q['gdn_dtb'], bd)


def _gdn_chunk_step(chains):
    c = GDN_CHUNK
    r256 = lax.broadcasted_iota(jnp.int32, (256, 256), 0)
    c256 = lax.broadcasted_iota(jnp.int32, (256, 256), 1)
    blockmask = (r256 // c) == (c256 // c)
    rl = lax.broadcasted_iota(jnp.int32, (LANES, 256), 0)
    cl = lax.broadcasted_iota(jnp.int32, (LANES, 256), 1)
    ri = lax.broadcasted_iota(jnp.int32, (c, c), 0)
    ci = lax.broadcasted_iota(jnp.int32, (c, c), 1)
    i_s = lax.broadcasted_iota(jnp.int32, (c, 256), 0)
    j_s = lax.broadcasted_iota(jnp.int32, (c, 256), 1) % c
    ones = jnp.ones((c, c), BF16)
    eye = (i_s == j_s).astype(F32)
    scale = HEAD_DIM ** -0.5

    def bdv(y):
        return jnp.where(blockmask, jnp.concatenate([y, y, y, y], axis=0), 0.0).astype(BF16)

    def hilo(x):
        hi = x.astype(BF16)
        return hi, (x - hi.astype(F32)).astype(BF16)

    st = []
    for dirn, qb, kb16, vb, gb, s_ref in chains:
        rev = dirn == 1
        e_g = (rl == dirn * 4 + cl // c).astype(BF16)
        e_b = (rl == 8 + dirn * 4 + cl // c).astype(BF16)
        ghi, glo = hilo(gb)
        st.append(dict(rev=rev, q=qb.astype(F32), k=kb16.astype(F32), v=vb.astype(F32), s_ref=s_ref,
                       g=_dot(ghi, e_g) + _dot(glo, e_g), beta=_dot(ghi, e_b) + _dot(glo, e_b),
                       tri=((ci >= ri) if rev else (ci <= ri)).astype(BF16),
                       allowed=(j_s >= i_s) if rev else (j_s <= i_s),
                       strict=(j_s > i_s) if rev else (j_s < i_s)))
    for d in st:
        ghi, glo = hilo(d['g'])
        d['gc'] = _dot(d['tri'], ghi) + _dot(d['tri'], glo)
    for d in st:
        zhi, zlo = hilo(jnp.where(i_s == j_s, d['gc'], 0.0))
        d['gct'] = _dot(ones, zhi) + _dot(ones, zlo)
    for d in st:
        d['decay'] = jnp.exp(jnp.where(d['allowed'], d['gc'] - d['gct'], -jnp.inf))
        d['eg'] = jnp.exp(d['gc'])
        d['kbeta'] = d['k'] * d['beta']
    for d in st:
        kk_qk = _dot_nt(jnp.concatenate([d['kbeta'], d['q'] * scale], axis=0).astype(BF16), bdv(d['k']))
        d['p'] = -jnp.where(d['strict'], kk_qk[:c] * d['decay'], 0.0)
        d['intra'] = jnp.where(d['allowed'], kk_qk[c:] * d['decay'], 0.0)
        d['t'] = eye + d['p']
    for _ in range(5):
        for d in st:
            d['p'] = _dot(d['p'].astype(BF16), bdv(d['p']))
        for d in st:
            d['t'] = d['t'] + _dot(d['t'].astype(BF16), bdv(d['p']))
    for d in st:
        t16 = d['t'].astype(BF16)
        d['u'] = _dot(t16, bdv(d['v'] * d['beta']))
        d['w'] = _dot(t16, bdv(d['kbeta'] * d['eg']))
    for d in st:
        d['s'] = d['s_ref'][...]
        d['ws_qs'] = _dot(jnp.concatenate([d['w'], d['q'] * scale * d['eg']], axis=0).astype(BF16),
                          d['s'].astype(BF16))
    for d in st:
        d['v_new'] = d['u'] - d['ws_qs'][:c]
        d['o'] = d['ws_qs'][c:] + _dot(d['intra'].astype(BF16), bdv(d['v_new']))
    for d in st:
        last = 0 if d['rev'] else c - 1
        g_last = d['gc'][last:last + 1, :]
        kg = (d['k'] * jnp.exp(g_last - d['gc'])).astype(BF16)
        upd = lax.dot_general(kg, d['v_new'].astype(BF16), (((0,), (0,)), ((), ())), preferred_element_type=F32)
        d['s_ref'][...] = d['s'] * jnp.exp(g_last) + jnp.where(blockmask, upd, 0.0)
    return [d['o'] for d in st]


GDN_SEQS = 2


def _gdn_chunk_body(qf_ref, kf_ref, vf_ref, gf_ref, qb_ref, kb_ref, vb_ref, gb_ref, of_ref, ob_ref, s_ref):
    @pl.when(pl.program_id(1) == 0)
    def _():
        s_ref[...] = jnp.zeros(s_ref.shape, F32)

    chains = []
    for j in range(qf_ref.shape[0]):
        chains.append((0, qf_ref[j], kf_ref[j], vf_ref[j], gf_ref[j], s_ref.at[j, 0]))
        chains.append((1, qb_ref[j], kb_ref[j], vb_ref[j], gb_ref[j], s_ref.at[j, 1]))
    outs = _gdn_chunk_step(chains)
    for j in range(qf_ref.shape[0]):
        of_ref[j] = outs[2 * j].astype(BF16)
        ob_ref[j] = outs[2 * j + 1].astype(BF16)


def _gdn_chunks(qn, kn, vs, gb, b, l):
    c = GDN_CHUNK
    n = l // c
    nseq = GDN_SEQS if b % GDN_SEQS == 0 else 1
    fwd = lambda bi, i: (bi, i, 0)
    bwd = lambda bi, i: (bi, n - 1 - i, 0)
    blk = lambda m: pl.BlockSpec((nseq, c, 256), m)
    gblk = lambda m: pl.BlockSpec((nseq, c, GATE_W), m)
    qn, kn, vs = (a.reshape(b, l, 256) for a in (qn, kn, vs))
    gb = gb.reshape(b, l, GATE_W)
    of, ob = pl.pallas_call(
        _gdn_chunk_body,
        grid=(b // nseq, n),
        in_specs=[blk(fwd), blk(fwd), blk(fwd), gblk(fwd), blk(bwd), blk(bwd), blk(bwd), gblk(bwd)],
        out_specs=[blk(fwd), blk(bwd)],
        out_shape=[jax.ShapeDtypeStruct((b, l, 256), BF16)] * 2,
        scratch_shapes=[pltpu.VMEM((nseq, 2, 256, 256), F32)],
        compiler_params=_cparams(("parallel", "arbitrary")),
        name="gdn_chunks",
    )(qn, kn, vs, gb, qn, kn, vs, gb)
    return of.reshape(b * l, 256), ob.reshape(b * l, 256)


def _gdn_out_body(of_ref, ob_ref, z_ref, g_ref, bd_ref, o_ref):
    o = of_ref[...].astype(F32) + ob_ref[...].astype(F32)
    z = z_ref[...].astype(F32)
    ms = _head_mean_sq(o, bd_ref[...])
    o_ref[...] = (o * lax.rsqrt(ms + NORM_EPS) * g_ref[...] * (z * jax.nn.sigmoid(z))).astype(BF16)


def _gdn_out(of, ob, main, g, bd):
    t = of.shape[0]
    tm = min(1024, t)
    row = lambda i: (i, 0)
    full = lambda i: (0, 0)
    return pl.pallas_call(
        _gdn_out_body,
        grid=(t // tm,),
        in_specs=[pl.BlockSpec((tm, 256), row), pl.BlockSpec((tm, 256), row),
                  pl.BlockSpec((tm, 256), lambda i: (i, COL_CZ // 256)),
                  pl.BlockSpec((1, 256), full), pl.BlockSpec((256, 256), full)],
        out_specs=pl.BlockSpec((tm, 256), row),
        out_shape=jax.ShapeDtypeStruct((t, 256), BF16),
        compiler_params=_cparams(("parallel",)),
        name="gdn_out",
    )(of, ob, main, g, bd)


def _gdn(main, gate, q, bd, b, l):
    qn, kn, vs, gb = _gdn_prep(main, gate, q, bd, b, l)
    of, ob = _gdn_chunks(qn, kn, vs, gb, b, l)
    return _gdn_out(of, ob, main, q['gdn_o_norm'], bd)


def _moe(x1, x1b, top_idx, gates, p):
    t = x1.shape[0]
    m = t * TOP_K
    flat_e = top_idx.reshape(-1)
    order = jnp.argsort(flat_e, stable=True).astype(jnp.int32)
    inv = jnp.argsort(order).astype(jnp.int32)
    sorted_e = flat_e[order]
    experts = jnp.arange(N_EXPERTS, dtype=jnp.int32)
    counts = jnp.sum((flat_e[:, None] == experts[None, :]).astype(jnp.int32), axis=0)
    padded = (counts + MOE_BLOCK - 1) // MOE_BLOCK * MOE_BLOCK
    group_start = jnp.cumsum(counts) - counts
    padded_end = jnp.cumsum(padded)
    padded_start = padded_end - padded
    dest = (padded_start[sorted_e] + jnp.arange(m, dtype=jnp.int32) - group_start[sorted_e]).astype(jnp.int32)
    n_blocks = -(-m // MOE_BLOCK) + N_EXPERTS
    n_slots = n_blocks * MOE_BLOCK
    block_start = jnp.arange(n_blocks, dtype=jnp.int32) * MOE_BLOCK
    block_expert = jnp.minimum(jnp.sum((padded_end[None, :] <= block_start[:, None]).astype(jnp.int32), axis=1),
                               N_EXPERTS - 1).astype(jnp.int32)
    n_used = (padded_end[-1] // MOE_BLOCK).astype(jnp.int32).reshape(1)
    slot = jnp.arange(n_slots, dtype=jnp.int32)
    slot_e = jnp.repeat(block_expert, MOE_BLOCK)
    rank = slot - padded_start[slot_e]
    src = jnp.clip(group_start[slot_e] + rank, 0, m - 1)
    slot_token = jnp.where(rank < counts[slot_e], order[src] // TOP_K, 0).astype(jnp.int32)
    xb = x1b[slot_token]
    yb = _ffn(xb, block_expert, n_used, p['moe_wg'], p['moe_wl'], p['moe_bg'], p['moe_bl'],
              p['moe_w2'], p['moe_b2'])
    dest_tk = dest[inv].reshape(t, TOP_K)
    return jnp.sum(yb[dest_tk].astype(F32) * gates[..., None], axis=1)


def _block_diag_mean(width, group):
    idx = np.arange(width)
    return jnp.asarray((idx[:, None] // group == idx[None, :] // group) / group, BF16)


def _axial_tables(l):
    rows = l // GRID_W
    row_pos = np.repeat(np.arange(rows), GRID_W).astype(np.float64)
    col_pos = np.tile(np.arange(GRID_W), rows).astype(np.float64)
    lane = np.arange(LANES)
    d = lane % HEAD_DIM
    e = d % 32
    f = e % 16
    inv = (AXIAL_THETA ** (-(np.arange(0, 32, 2, dtype=np.float32)) / 32)).astype(np.float32)
    pos = np.where((d // 32)[None, :] == 0, row_pos[:, None], col_pos[:, None]).astype(np.float32)
    ang = pos * inv[f][None, :]
    sign = np.where(e < 16, -1.0, 1.0)[None, :]
    return jnp.asarray(np.cos(ang), F32), jnp.asarray(np.sin(ang) * sign, F32)


def _diff_tables(l):
    lane = np.arange(256)
    e = lane % 32
    f = e % 4
    inv = (ROPE_THETA ** (-(np.arange(0, PARTIAL_ROPE_DIMS, 2, dtype=np.float32)) / PARTIAL_ROPE_DIMS)).astype(np.float32)
    ang = np.arange(l, dtype=np.float32)[:, None] * inv[f][None, :]
    roped = (e < PARTIAL_ROPE_DIMS)[None, :]
    sign = np.where(e < 4, -1.0, 1.0)[None, :]
    c = np.where(roped, np.cos(ang), 1.0)
    s = np.where(roped, np.sin(ang) * sign, 0.0)
    return jnp.asarray(c, F32), jnp.asarray(s, F32)


def _prep_layer(params, layer, w1_all):
    p = {name: arr[layer] for name, arr in params.items()}
    w_in = p['w_in']
    w_in = jnp.concatenate([w_in[:, :ORIG_CA], w_in[:, ORIG_CA + 16:], w_in[:, ORIG_CA:ORIG_CA + 16],
                            jnp.zeros((D_MODEL, GATE_W - 16), F32)], axis=1)
    q = dict(p)
    q['w_in'] = w_in.astype(BF16)
    q['w_out'] = p['w_out'].astype(BF16)
    q['router_w'] = jnp.concatenate([p['router_w'], jnp.zeros((D_MODEL, LANES - N_EXPERTS), F32)], axis=1).astype(BF16)
    q['router_b'] = jnp.concatenate([p['router_b'], jnp.full((LANES - N_EXPERTS,), -1e30, F32)])[None, :]
    q['moe_wg'] = w1_all[0][layer]
    q['moe_wl'] = w1_all[1][layer]
    q['moe_bg'] = p['moe_b1'][:, None, 0::2]
    q['moe_bl'] = p['moe_b1'][:, None, 1::2]
    q['moe_w2'] = p['moe_w2'].astype(BF16)
    q['moe_b2'] = p['moe_b2'][:, None, :]
    for name in ('ln1_g', 'ln1_b', 'ln2_g', 'ln2_b'):
        q[name] = p[name][None, :]
    q['gqa_q_norm'] = jnp.tile(p['gqa_q_norm'], 4)[None, :]
    q['gqa_k_norm'] = jnp.tile(p['gqa_k_norm'], 2)[None, :]
    lambda_init = 0.8 - 0.6 * math.exp(-0.3 * layer)
    q['diff_subln'] = (jnp.tile(p['diff_subln'], 4) * (1.0 - lambda_init))[None, :]
    lam = (jnp.exp(jnp.sum(p['diff_lambda_q1'] * p['diff_lambda_k1']))
           - jnp.exp(jnp.sum(p['diff_lambda_q2'] * p['diff_lambda_k2'])) + lambda_init)
    q['diff_lam'] = lam.reshape(1).astype(F32)
    q['s5'] = _s5_prep(p)
    q['ssm_glu_w'] = p['ssm_glu_w'].astype(BF16)
    q['ssm_glu_b'] = p['ssm_glu_b'][None, :]
    q['gdn_conv_w'] = jnp.concatenate([p['gdn_conv_w'], jnp.zeros((3, 768), F32)], axis=0)
    pad8 = lambda x: jnp.concatenate([x.reshape(-1), jnp.zeros((LANES - 8,), F32)])[None, :]
    q['gdn_al'] = pad8(jnp.exp(p['gdn_a_log']))
    q['gdn_dtb'] = pad8(p['gdn_dt_bias'])
    q['gdn_o_norm'] = jnp.tile(p['gdn_o_norm'], 4)[None, :]
    return q


def _mixers(main, gate, q, tabs, b, l):
    a_out = _s5(main, q, tabs['s5sel'], b, l)
    b_out = _gqa(main, tabs['axial'], q['gqa_q_norm'], q['gqa_k_norm'], tabs['bd64'], b, l)
    c_out = _gdn(main, gate, q, tabs['bd64'], b, l)
    d_out = _diff(main, q['diff_lam'], tabs['diff'], q['diff_subln'], tabs['bd64'], b, l)
    return a_out, b_out, c_out, d_out


def _layer(x, q, tabs, b, l, alpha):
    main, gate = _inproj(x, q['w_in'])
    pieces = _mixers(main, gate, q, tabs, b, l)
    x1, x1b, idx, gates = _outproj(pieces, q['w_out'], x, q['ln1_g'], q['ln1_b'],
                                   q['router_w'], q['router_b'], alpha)
    ffn = _moe(x1, x1b, idx[:, :TOP_K], gates[:, :TOP_K], q)
    return _ln2(x1, ffn, q['ln2_g'], q['ln2_b'], alpha)


def _trunk(x, layers, alpha):
    b, l, _ = x.shape
    tabs = {'axial': _axial_tables(l), 'diff': _diff_tables(l), 'bd64': _block_diag_mean(256, HEAD_DIM),
            's5sel': _s5_sel()}
    h = x.reshape(b * l, D_MODEL)
    for q in layers:
        h = _layer(h, q, tabs, b, l, alpha)
    return h.reshape(b, l, D_MODEL)


def kernel(x_prompt, x_sample, w_in, w_out, ssm_a_re, ssm_a_im, ssm_log_dt, ssm_b_re, ssm_b_im, ssm_c_re, ssm_c_im, ssm_d, ssm_glu_w, ssm_glu_b, gqa_q_norm, gqa_k_norm, gdn_conv_w, gdn_a_log, gdn_dt_bias, gdn_o_norm, diff_lambda_q1, diff_lambda_k1, diff_lambda_q2, diff_lambda_k2, diff_subln, router_w, router_b, moe_w1, moe_b1, moe_w2, moe_b2, ln1_g, ln1_b, ln2_g, ln2_b):
    params = {
        'w_in': w_in, 'w_out': w_out,
        'ssm_a_re': ssm_a_re, 'ssm_a_im': ssm_a_im, 'ssm_log_dt': ssm_log_dt,
        'ssm_b_re': ssm_b_re, 'ssm_b_im': ssm_b_im, 'ssm_c_re': ssm_c_re, 'ssm_c_im': ssm_c_im,
        'ssm_d': ssm_d, 'ssm_glu_w': ssm_glu_w, 'ssm_glu_b': ssm_glu_b,
        'gqa_q_norm': gqa_q_norm, 'gqa_k_norm': gqa_k_norm,
        'gdn_conv_w': gdn_conv_w, 'gdn_a_log': gdn_a_log, 'gdn_dt_bias': gdn_dt_bias, 'gdn_o_norm': gdn_o_norm,
        'diff_lambda_q1': diff_lambda_q1, 'diff_lambda_k1': diff_lambda_k1,
        'diff_lambda_q2': diff_lambda_q2, 'diff_lambda_k2': diff_lambda_k2, 'diff_subln': diff_subln,
        'router_w': router_w, 'router_b': router_b,
        'moe_w1': moe_w1, 'moe_b1': moe_b1, 'moe_w2': moe_w2, 'moe_b2': moe_b2,
        'ln1_g': ln1_g, 'ln1_b': ln1_b, 'ln2_g': ln2_g, 'ln2_b': ln2_b,
    }
    depth = w_in.shape[0]
    alpha = (2.0 * depth) ** 0.25
    w1_all = [w.reshape(moe_w1.shape[:3] + (-1,)) for w in _w1_prep(moe_w1.reshape((-1,) + moe_w1.shape[2:]))]
    layers = [_prep_layer(params, layer, w1_all) for layer in range(depth)]
    return (_trunk(x_prompt, layers, alpha), _trunk(x_sample, layers, alpha))
```

```python
import functools
import math

import jax
import jax.numpy as jnp
import numpy as np
from jax import lax
from jax.experimental import pallas as pl
from jax.experimental.pallas import tpu as pltpu

F32 = jnp.float32
BF16 = jnp.bfloat16

D_MODEL = 1024
GROUP_WIDTH = 256
HEAD_DIM = 64
SSM_GROUPS = 16
SSM_GROUP_CH = 16
SSM_STATE = 64
GDN_HEADS = 4
GDN_CHUNK = 64
DIFF_SUB_DIM = 32
PARTIAL_ROPE_DIMS = 8
AXIAL_THETA = 10000.0
ROPE_THETA = 500000.0
GRID_W = 64
N_EXPERTS = 32
TOP_K = 4
SWIGLU_ALPHA = 1.702
SWIGLU_LIMIT = 7.0
NORM_EPS = 1e-6
LN_EPS = 1e-5
LOG2E = math.log2(math.e)

LANES = 128
VMEM_LIMIT = 56 * 1024 * 1024

COL_U, COL_GQ, COL_GK, COL_GV = 0, 256, 512, 640
COL_CQ, COL_CK, COL_CV, COL_CZ = 768, 1024, 1280, 1536
COL_DQ, COL_DK, COL_DV = 1792, 2048, 2304
MAIN_W = 2560
GATE_W = 128
ORIG_CA = 1792


def _cparams(sem):
    return pltpu.CompilerParams(dimension_semantics=sem, vmem_limit_bytes=VMEM_LIMIT)


def _dot(a, b):
    return jnp.dot(a, b, preferred_element_type=F32)


def _dot_nt(a, b):
    return lax.dot_general(a, b, (((1,), (1,)), ((), ())), preferred_element_type=F32)


def _split_dot(x, m):
    hi = x.astype(BF16)
    lo = (x - hi.astype(F32)).astype(BF16)
    return _dot(hi, m) + _dot(lo, m)


def _chunk_rows_sel(n_chunks, t):
    r = lax.broadcasted_iota(jnp.int32, (n_chunks, S5_CHUNK * n_chunks), 0)
    c = lax.broadcasted_iota(jnp.int32, (n_chunks, S5_CHUNK * n_chunks), 1)
    return (c == S5_CHUNK * r + t).astype(BF16)


def _inproj_body(x_ref, w_ref, main_ref, gate_ref, ux_ref):
    x = x_ref[...].astype(BF16)
    step = 640
    for c in range(MAIN_W // step):
        main_ref[:, c * step:(c + 1) * step] = _dot(x, w_ref[:, c * step:(c + 1) * step]).astype(BF16)
    gate_ref[...] = _dot(x, w_ref[:, MAIN_W:])
    u = main_ref[:, COL_U:COL_U + 256]
    for t in range(S5_CHUNK):
        ux_ref[:, t * 256:(t + 1) * 256] = _dot(_chunk_rows_sel(ux_ref.shape[0], t), u).astype(BF16)


def _inproj(x, w):
    t = x.shape[0]
    tm = min(512, t)
    return pl.pallas_call(
        _inproj_body,
        grid=(t // tm,),
        in_specs=[pl.BlockSpec((tm, D_MODEL), lambda i: (i, 0)),
                  pl.BlockSpec((D_MODEL, MAIN_W + GATE_W), lambda i: (0, 0))],
        out_specs=[pl.BlockSpec((tm, MAIN_W), lambda i: (i, 0)),
                   pl.BlockSpec((tm, GATE_W), lambda i: (i, 0)),
                   pl.BlockSpec((tm // S5_CHUNK, S5_CHUNK * 256), lambda i: (i, 0))],
        out_shape=[jax.ShapeDtypeStruct((t, MAIN_W), BF16),
                   jax.ShapeDtypeStruct((t, GATE_W), F32),
                   jax.ShapeDtypeStruct((t // S5_CHUNK, S5_CHUNK * 256), BF16)],
        compiler_params=_cparams(("parallel",)),
        name="inproj",
    )(x, w)


def _head_mean_sq(xf, bd):
    return _split_dot(xf * xf, bd)


def _rope_lanes(x, c, s, half):
    n = x.shape[-1]
    lane = lax.broadcasted_iota(jnp.int32, x.shape, 1)
    first = (lane % (2 * half)) < half
    swapped = jnp.where(first, pltpu.roll(x, n - half, 1), pltpu.roll(x, half, 1))
    return x * c + swapped * s


def _gqa_body(q_ref, k_ref, v_ref, cq_ref, sq_ref, ck_ref, sk_ref, gq_ref, gk_ref, bd_ref,
              o_ref, kdup_ref, vaug_ref):
    i = pl.program_id(1)
    bd = bd_ref[...]

    @pl.when(i == 0)
    def _():
        kf = k_ref[...].astype(F32)
        ms = _head_mean_sq(kf, bd[:LANES, :LANES])
        kn = kf * lax.rsqrt(ms + NORM_EPS) * gk_ref[...]
        kn = _rope_lanes(kn, ck_ref[...], sk_ref[...], 16)
        lane = lax.broadcasted_iota(jnp.int32, kn.shape, 1)
        sw = pltpu.roll(kn, 64, 1)
        kdup_ref[0] = jnp.where(lane < 64, kn, sw).astype(BF16)
        kdup_ref[1] = jnp.where(lane < 64, sw, kn).astype(BF16)
        v = v_ref[...]
        one = jnp.ones(v.shape, BF16)
        vaug_ref[0] = jnp.where(lane < 64, v, one)
        vaug_ref[1] = jnp.where(lane < 64, one, v)

    qf = q_ref[...].astype(F32)
    ms = _head_mean_sq(qf, bd)
    qn = qf * lax.rsqrt(ms + NORM_EPS) * gq_ref[...]
    cq = jnp.concatenate([cq_ref[...], cq_ref[...]], axis=1)
    sq = jnp.concatenate([sq_ref[...], sq_ref[...]], axis=1)
    qn = (_rope_lanes(qn, cq, sq, 16) * (HEAD_DIM ** -0.5 * LOG2E)).astype(BF16)
    tq = qn.shape[0]
    lane = lax.broadcasted_iota(jnp.int32, (tq, LANES), 1)
    zero = jnp.zeros((tq, LANES), BF16)
    for h in range(2):
        qh = qn[:, h * LANES:(h + 1) * LANES]
        q2 = jnp.concatenate([jnp.where(lane < 64, qh, zero), jnp.where(lane < 64, zero, qh)], axis=0)
        s = _dot_nt(q2, kdup_ref[h])
        m = jnp.max(s, axis=-1, keepdims=True)
        p = jnp.exp2(s - m).astype(BF16)
        o2 = _dot(p, vaug_ref[h])
        o2 = o2 / pltpu.roll(o2, 64, 1)
        top, bot = o2[:tq], o2[tq:]
        if h == 0:
            oh = jnp.where(lane < 64, top, pltpu.roll(bot, 64, 1))
        else:
            oh = jnp.where(lane < 64, pltpu.roll(top, 64, 1), bot)
        o_ref[:, h * LANES:(h + 1) * LANES] = oh.astype(BF16)


def _gqa(main, tabs, gq, gk, bd, b, l):
    tq = min(256, l)
    nq = l // tq
    cq, sq = tabs
    full = lambda bi, i: (0, 0)
    return pl.pallas_call(
        _gqa_body,
        grid=(b, nq),
        in_specs=[pl.BlockSpec((tq, 256), lambda bi, i: (bi * nq + i, COL_GQ // 256)),
                  pl.BlockSpec((l, LANES), lambda bi, i: (bi, COL_GK // LANES)),
                  pl.BlockSpec((l, LANES), lambda bi, i: (bi, COL_GV // LANES)),
                  pl.BlockSpec((tq, LANES), lambda bi, i: (i, 0)),
                  pl.BlockSpec((tq, LANES), lambda bi, i: (i, 0)),
                  pl.BlockSpec((l, LANES), full),
                  pl.BlockSpec((l, LANES), full),
                  pl.BlockSpec((1, 256), full),
                  pl.BlockSpec((1, LANES), full),
                  pl.BlockSpec((256, 256), full)],
        out_specs=pl.BlockSpec((tq, 256), lambda bi, i: (bi * nq + i, 0)),
        out_shape=jax.ShapeDtypeStruct((b * l, 256), BF16),
        scratch_shapes=[pltpu.VMEM((2, l, LANES), BF16), pltpu.VMEM((2, l, LANES), BF16)],
        compiler_params=_cparams(("parallel", "arbitrary")),
        name="gqa",
    )(main, main, main, cq, sq, cq, sq, gq, gk, bd)


def _diff_body(lam_ref, q_ref, k_ref, v_ref, cq_ref, sq_ref, ck_ref, sk_ref, g_ref, bd_ref,
               o_ref, kr_ref, vaug_ref):
    i = pl.program_id(1)

    @pl.when(i == 0)
    def _():
        kr_ref[...] = _rope_lanes(k_ref[...].astype(F32), ck_ref[...], sk_ref[...], 4).astype(BF16)
        v = v_ref[...]
        vlane = lax.broadcasted_iota(jnp.int32, v.shape, 1)
        one = jnp.ones(v.shape, BF16)
        for h in range(4):
            vaug_ref[h] = jnp.where(vlane // 64 == h, v, one)

    lam = lam_ref[0]
    qr = _rope_lanes(q_ref[...].astype(F32), cq_ref[...], sq_ref[...], 4)
    qr = (qr * (DIFF_SUB_DIM ** -0.5 * LOG2E)).astype(BF16)
    tq = qr.shape[0]
    lane = lax.broadcasted_iota(jnp.int32, (tq, 256), 1)
    zero = jnp.zeros((tq, 256), BF16)
    kr = kr_ref[...]
    acc = jnp.zeros((tq, 256), F32)
    for h in range(4):
        q2 = jnp.concatenate([jnp.where(lane // 32 == 2 * h, qr, zero),
                              jnp.where(lane // 32 == 2 * h + 1, qr, zero)], axis=0)
        s = _dot_nt(q2, kr)
        m = jnp.max(s, axis=-1, keepdims=True)
        e = jnp.exp2(s - m).astype(BF16)
        o2 = _dot(e, vaug_ref[h])
        o2 = o2 / pltpu.roll(o2, 64, 1)
        o = o2[:tq] - lam * o2[tq:]
        acc = jnp.where(lane // 64 == h, o, acc)
    ms = _head_mean_sq(acc, bd_ref[...])
    o_ref[...] = (acc * lax.rsqrt(ms + NORM_EPS) * g_ref[...]).astype(BF16)


def _diff(main, lam, tabs, g, bd, b, l):
    tq = min(256, l)
    nq = l // tq
    c, s = tabs
    full = lambda bi, i, *_: (0, 0)
    once = pl.Buffered(1)
    grid_spec = pltpu.PrefetchScalarGridSpec(
        num_scalar_prefetch=1,
        grid=(b, nq),
        in_specs=[pl.BlockSpec((tq, 256), lambda bi, i, *_: (bi * nq + i, COL_DQ // 256)),
                  pl.BlockSpec((l, 256), lambda bi, i, *_: (bi, COL_DK // 256)),
                  pl.BlockSpec((l, 256), lambda bi, i, *_: (bi, COL_DV // 256)),
                  pl.BlockSpec((tq, 256), lambda bi, i, *_: (i, 0)),
                  pl.BlockSpec((tq, 256), lambda bi, i, *_: (i, 0)),
                  pl.BlockSpec((l, 256), full, pipeline_mode=once),
                  pl.BlockSpec((l, 256), full, pipeline_mode=once),
                  pl.BlockSpec((1, 256), full),
                  pl.BlockSpec((256, 256), full)],
        out_specs=pl.BlockSpec((tq, 256), lambda bi, i, *_: (bi * nq + i, 0)),
        scratch_shapes=[pltpu.VMEM((l, 256), BF16), pltpu.VMEM((4, l, 256), BF16)],
    )
    return pl.pallas_call(
        _diff_body,
        grid_spec=grid_spec,
        out_shape=jax.ShapeDtypeStruct((b * l, 256), BF16),
        compiler_params=_cparams(("parallel", "arbitrary")),
        name="diffattn",
    )(lam, main, main, main, c, s, c, s, g, bd)


def _layer_norm_rows(y, g, b):
    mu = jnp.mean(y, axis=-1, keepdims=True)
    d = y - mu
    var = jnp.mean(d * d, axis=-1, keepdims=True)
    return d * lax.rsqrt(var + LN_EPS) * g + b


def _outproj_body(alpha, a_ref, b_ref, c_ref, d_ref, w_ref, x_ref, g_ref, beta_ref, rw_ref, rb_ref,
                  x1_ref, x1b_ref, idx_ref, gate_ref):
    nch = a_ref.shape[0]
    a = jnp.zeros((nch * S5_CHUNK, 256), F32)
    r = lax.broadcasted_iota(jnp.int32, (nch * S5_CHUNK, nch), 0)
    n = lax.broadcasted_iota(jnp.int32, (nch * S5_CHUNK, nch), 1)
    for t in range(S5_CHUNK):
        a = a + _dot((r == S5_CHUNK * n + t).astype(BF16), a_ref[:, t * 256:(t + 1) * 256])
    mixed = (_dot(a.astype(BF16), w_ref[0:256, :]) + _dot(b_ref[...], w_ref[256:512, :])
             + _dot(c_ref[...], w_ref[512:768, :]) + _dot(d_ref[...], w_ref[768:1024, :]))
    x1 = _layer_norm_rows(alpha * x_ref[...] + mixed, g_ref[...], beta_ref[...])
    x1_ref[...] = x1
    x1b = x1.astype(BF16)
    x1b_ref[...] = x1b
    logits = _dot(x1b, rw_ref[...]) + rb_ref[...]
    lane = lax.broadcasted_iota(jnp.int32, logits.shape, 1)
    vals, idxs = [], []
    for _ in range(TOP_K):
        m = jnp.max(logits, axis=-1, keepdims=True)
        ix = jnp.min(jnp.where(logits == m, lane, LANES), axis=-1, keepdims=True)
        vals.append(m)
        idxs.append(ix)
        logits = jnp.where(lane == ix, -jnp.inf, logits)
    es = [jnp.exp(vk - vals[0]) for vk in vals]
    tot = es[0] + es[1] + es[2] + es[3]
    idx_out = jnp.zeros(logits.shape, jnp.int32)
    gate_out = jnp.zeros(logits.shape, F32)
    for k in range(TOP_K):
        idx_out = jnp.where(lane == k, idxs[k], idx_out)
        gate_out = jnp.where(lane == k, es[k] / tot, gate_out)
    idx_ref[...] = idx_out
    gate_ref[...] = gate_out


def _outproj(pieces, w, x, g, beta, rw, rb, alpha):
    t = x.shape[0]
    tm = min(512, t)
    row = lambda i: (i, 0)
    full = lambda i: (0, 0)
    return pl.pallas_call(
        functools.partial(_outproj_body, alpha),
        grid=(t // tm,),
        in_specs=[pl.BlockSpec((tm // S5_CHUNK, S5_CHUNK * 256), row)] + [pl.BlockSpec((tm, 256), row)] * 3 + [
            pl.BlockSpec((D_MODEL, D_MODEL), full),
            pl.BlockSpec((tm, D_MODEL), row),
            pl.BlockSpec((1, D_MODEL), full),
            pl.BlockSpec((1, D_MODEL), full),
            pl.BlockSpec((D_MODEL, LANES), full),
            pl.BlockSpec((1, LANES), full)],
        out_specs=[pl.BlockSpec((tm, D_MODEL), row), pl.BlockSpec((tm, D_MODEL), row),
                   pl.BlockSpec((tm, LANES), row), pl.BlockSpec((tm, LANES), row)],
        out_shape=[jax.ShapeDtypeStruct((t, D_MODEL), F32), jax.ShapeDtypeStruct((t, D_MODEL), BF16),
                   jax.ShapeDtypeStruct((t, LANES), jnp.int32), jax.ShapeDtypeStruct((t, LANES), F32)],
        compiler_params=_cparams(("parallel",)),
        name="outproj_ln_router",
    )(*pieces, w, x, g, beta, rw, rb)


MOE_BLOCK = 512
MOE_RANGES = 4


def _ffn_body(be_ref, nb_ref, x_ref, wg_ref, wl_ref, bg_ref, bl_ref, w2_ref, b2_ref, *rest):
    o_ref = rest[-1]
    i = pl.program_id(0)

    @pl.when(i < nb_ref[0])
    def _():
        x = x_ref[...]
        glu = jnp.minimum(_dot(x, wg_ref[...]) + bg_ref[...], SWIGLU_LIMIT)
        lin = jnp.clip(_dot(x, wl_ref[...]) + bl_ref[...], -SWIGLU_LIMIT, SWIGLU_LIMIT)
        act = (glu * jax.nn.sigmoid(SWIGLU_ALPHA * glu) * (lin + 1.0)).astype(BF16)
        o_ref[...] = (_dot(act, w2_ref[...]) + b2_ref[...]).astype(o_ref.dtype)

    @pl.when(i >= nb_ref[0])
    def _():
        o_ref[...] = jnp.zeros(o_ref.shape, o_ref.dtype)


def _ffn(xb, block_expert, n_used, wg, wl, bg, bl, w2, b2, yb_prev, n_slots, block_offset):
    n_blocks = xb.shape[0] // MOE_BLOCK
    de = wg.shape[2]
    expert = lambda i, be, nb: (be[i], 0, 0)
    in_specs = [pl.BlockSpec((MOE_BLOCK, D_MODEL), lambda i, be, nb: (i, 0)),
                pl.BlockSpec((None, D_MODEL, de), expert),
                pl.BlockSpec((None, D_MODEL, de), expert),
                pl.BlockSpec((None, 1, de), expert),
                pl.BlockSpec((None, 1, de), expert),
                pl.BlockSpec((None, de, D_MODEL), expert),
                pl.BlockSpec((None, 1, D_MODEL), expert)]
    args = [block_expert, n_used, xb, wg, wl, bg, bl, w2, b2]
    aliases = {}
    if yb_prev is not None:
        in_specs.append(pl.BlockSpec(memory_space=pl.ANY))
        aliases = {len(args): 0}
        args.append(yb_prev)
    grid_spec = pltpu.PrefetchScalarGridSpec(
        num_scalar_prefetch=2,
        grid=(n_blocks,),
        in_specs=in_specs,
        out_specs=pl.BlockSpec((MOE_BLOCK, D_MODEL), lambda i, be, nb: (i + block_offset, 0)),
    )
    return pl.pallas_call(
        _ffn_body,
        grid_spec=grid_spec,
        out_shape=jax.ShapeDtypeStruct((n_slots, D_MODEL), BF16),
        input_output_aliases=aliases,
        compiler_params=_cparams(("arbitrary",)),
        name="moe_ffn",
    )(*args)


def _w1_prep_body(w_ref, pg_ref, pl_ref, g_ref, l_ref):
    w = w_ref[...].astype(BF16)
    g_ref[...] = _dot(w, pg_ref[...]).astype(BF16)
    l_ref[...] = _dot(w, pl_ref[...]).astype(BF16)


def _w1_prep(w1):
    ne, dm, de2 = w1.shape
    tn = 512
    r = np.arange(tn)[:, None]
    c = np.arange(tn // 2)[None, :]
    sel_g = jnp.asarray(r == 2 * c, BF16)
    sel_l = jnp.asarray(r == 2 * c + 1, BF16)
    full = lambda e, j: (0, 0)
    out = jax.ShapeDtypeStruct((ne, dm, de2 // 2), BF16)
    return pl.pallas_call(
        _w1_prep_body,
        grid=(ne, de2 // tn),
        in_specs=[pl.BlockSpec((None, dm, tn), lambda e, j: (e, 0, j)),
                  pl.BlockSpec((tn, tn // 2), full), pl.BlockSpec((tn, tn // 2), full)],
        out_specs=[pl.BlockSpec((None, dm, tn // 2), lambda e, j: (e, 0, j))] * 2,
        out_shape=[out, out],
        compiler_params=_cparams(("parallel", "parallel")),
        name="w1_prep",
    )(w1, sel_g, sel_l)


def _ln2_body(alpha, x_ref, f_ref, g_ref, b_ref, o_ref):
    o_ref[...] = _layer_norm_rows(alpha * x_ref[...] + f_ref[...], g_ref[...], b_ref[...])


def _ln2(x, f, g, b, alpha):
    t = x.shape[0]
    tm = min(512, t)
    row = lambda i: (i, 0)
    full = lambda i: (0, 0)
    return pl.pallas_call(
        functools.partial(_ln2_body, alpha),
        grid=(t // tm,),
        in_specs=[pl.BlockSpec((tm, D_MODEL), row), pl.BlockSpec((tm, D_MODEL), row),
                  pl.BlockSpec((1, D_MODEL), full), pl.BlockSpec((1, D_MODEL), full)],
        out_specs=pl.BlockSpec((tm, D_MODEL), row),
        out_shape=jax.ShapeDtypeStruct((t, D_MODEL), F32),
        compiler_params=_cparams(("parallel",)),
        name="ln2",
    )(x, f, g, b)


S5_CHUNK = 16


S5_SEL_PAD = (SSM_GROUPS - 1) * SSM_GROUP_CH
S5_ROWS = 256


def _s5_sel():
    c = np.arange(256)
    target = (c // 16) * 256 + c % 16
    r = np.arange(S5_SEL_PAD + 4096)[:, None] - S5_SEL_PAD
    return jnp.asarray(r == target[None, :], BF16)


def _s5_in_body(x_ref, sel_ref, t_ref, w_ref, y_ref, h_ref):
    xcat = x_ref[...]
    us = []
    for g in range(SSM_GROUPS):
        start = S5_SEL_PAD - SSM_GROUP_CH * g
        u = _dot(xcat, sel_ref[start:start + 4096, :]).astype(BF16)
        y_ref[g] = _dot(u, t_ref[g]).astype(BF16)
        us.append(u)
    for i in range(SSM_GROUPS // 2):
        hp = _dot(jnp.concatenate([us[2 * i], us[2 * i + 1]], axis=1), w_ref[i])
        for c in range(4):
            h_ref[c, :, i * LANES:(i + 1) * LANES] = hp[:, c * LANES:(c + 1) * LANES]


def _s5_in(ux, sel, sp):
    t16 = ux.shape[0]
    rb = min(S5_ROWS, t16)
    full2 = lambda i: (0, 0)
    full3 = lambda i: (0, 0, 0)
    return pl.pallas_call(
        _s5_in_body,
        grid=(t16 // rb,),
        in_specs=[pl.BlockSpec((rb, S5_CHUNK * 256), lambda i: (i, 0)),
                  pl.BlockSpec(sel.shape, full2),
                  pl.BlockSpec((SSM_GROUPS, 256, 256), full3),
                  pl.BlockSpec((SSM_GROUPS // 2, 512, 512), full3)],
        out_specs=[pl.BlockSpec((SSM_GROUPS, rb, 256), lambda i: (0, i, 0)),
                   pl.BlockSpec((4, rb, 1024), lambda i: (0, i, 0))],
        out_shape=[jax.ShapeDtypeStruct((SSM_GROUPS, t16, 256), BF16),
                   jax.ShapeDtypeStruct((4, t16, 1024), F32)],
        compiler_params=_cparams(("parallel",)),
        name="s5_in",
    )(ux, sel, sp['t'], sp['w'])


def _s5_scan_body(n1, h_ref, lam_ref, e_ref):
    lam = lam_ref[...]
    zero = jnp.zeros((1, 1024), F32)

    def sweep(base, reverse):
        lr, li = lam[base:base + 1], lam[base + 1:base + 2]

        def tile(k, carry):
            er, ei = carry
            r0 = pl.multiple_of(((n1 // 8 - 1 - k) if reverse else k) * 8, 8)
            hr = h_ref[base, pl.ds(r0, 8), :]
            hi = h_ref[base + 1, pl.ds(r0, 8), :]
            outs_r, outs_i = [None] * 8, [None] * 8
            for j in (range(7, -1, -1) if reverse else range(8)):
                outs_r[j], outs_i[j] = er, ei
                er, ei = (lr * er - li * ei + hr[j:j + 1], lr * ei + li * er + hi[j:j + 1])
            e_ref[base, pl.ds(r0, 8), :] = jnp.concatenate(outs_r, axis=0)
            e_ref[base + 1, pl.ds(r0, 8), :] = jnp.concatenate(outs_i, axis=0)
            return er, ei

        lax.fori_loop(0, n1 // 8, tile, (zero, zero))

    sweep(0, False)
    sweep(2, True)


def _s5_scan(h, lam, b, n1):
    return pl.pallas_call(
        functools.partial(_s5_scan_body, n1),
        grid=(b,),
        in_specs=[pl.BlockSpec((4, n1, 1024), lambda i: (0, i, 0)),
                  pl.BlockSpec((8, 1024), lambda i: (0, 0))],
        out_specs=pl.BlockSpec((4, n1, 1024), lambda i: (0, i, 0)),
        out_shape=jax.ShapeDtypeStruct(h.shape, F32),
        compiler_params=_cparams(("parallel",)),
        name="s5_scan",
    )(h, lam)


def _s5_fin_body(y_ref, e_ref, sel_ref, v_ref, w_ref, b_ref, o_ref):
    ys = []
    for i in range(SSM_GROUPS // 2):
        ep = jnp.concatenate([e_ref[c, :, i * LANES:(i + 1) * LANES] for c in range(4)], axis=1)
        yi = _dot(ep.astype(BF16), v_ref[i])
        ys.append((y_ref[2 * i].astype(F32) + yi[:, :256]).astype(BF16))
        ys.append((y_ref[2 * i + 1].astype(F32) + yi[:, 256:]).astype(BF16))
    ycat = jnp.concatenate(ys, axis=1)
    for t in range(S5_CHUNK):
        start = S5_SEL_PAD - SSM_GROUP_CH * t
        y = jax.nn.gelu(_dot(ycat, sel_ref[start:start + 4096, :]))
        z = _dot(y.astype(BF16), w_ref[...]) + b_ref[...]
        o_ref[:, t * 256:(t + 1) * 256] = (y * jax.nn.sigmoid(z)).astype(BF16)


def _s5_fin(y, e, sel, sp, w, bias):
    t16 = y.shape[1]
    rb = min(S5_ROWS, t16)
    full2 = lambda i: (0, 0)
    full3 = lambda i: (0, 0, 0)
    return pl.pallas_call(
        _s5_fin_body,
        grid=(t16 // rb,),
        in_specs=[pl.BlockSpec((SSM_GROUPS, rb, 256), lambda i: (0, i, 0)),
                  pl.BlockSpec((4, rb, 1024), lambda i: (0, i, 0)),
                  pl.BlockSpec(sel.shape, full2),
                  pl.BlockSpec((SSM_GROUPS // 2, 512, 512), full3),
                  pl.BlockSpec((256, 256), full2),
                  pl.BlockSpec((1, 256), full2)],
        out_specs=pl.BlockSpec((rb, S5_CHUNK * 256), lambda i: (i, 0)),
        out_shape=jax.ShapeDtypeStruct((t16, S5_CHUNK * 256), BF16),
        compiler_params=_cparams(("parallel",)),
        name="s5_fin",
    )(y, e, sel, sp['v'], w, bias)


def _s5(ux, q, sel, b, l):
    y, h = _s5_in(ux, sel, q['s5'])
    e = _s5_scan(h, q['s5']['lam'], b, l // S5_CHUNK)
    return _s5_fin(y, e, sel, q['s5'], q['ssm_glu_w'], q['ssm_glu_b'])


def _s5_prep(p):
    hp = lax.Precision.HIGHEST
    c = S5_CHUNK
    tau = jnp.arange(c + 1, dtype=F32)[:, None, None]
    ks, ws, vs, lams = [], [], [], []
    for d in (0, 1):
        lam_re = p['ssm_a_re'][d]
        lam_im = p['ssm_a_im'][d]
        dt = jnp.exp(p['ssm_log_dt'][d])[:, None]
        mag = jnp.exp(lam_re * dt)
        abar_re = mag * jnp.cos(lam_im * dt)
        abar_im = mag * jnp.sin(lam_im * dt)
        den = lam_re * lam_re + lam_im * lam_im
        coef_re = ((abar_re - 1.0) * lam_re + abar_im * lam_im) / den
        coef_im = (abar_im * lam_re - (abar_re - 1.0) * lam_im) / den
        br, bi = p['ssm_b_re'][d], p['ssm_b_im'][d]
        bbar_re = coef_re[..., None] * br - coef_im[..., None] * bi
        bbar_im = coef_re[..., None] * bi + coef_im[..., None] * br
        cr, ci = p['ssm_c_re'][d], p['ssm_c_im'][d]
        pm = jnp.exp(tau * (lam_re * dt)[None])
        pr = pm * jnp.cos(tau * (lam_im * dt)[None])
        pi = pm * jnp.sin(tau * (lam_im * dt)[None])
        m_re = cr[None] * pr[:, :, None, :] - ci[None] * pi[:, :, None, :]
        m_im = cr[None] * pi[:, :, None, :] + ci[None] * pr[:, :, None, :]
        k = (jnp.einsum('tghp,gpk->gtkh', m_re[:c], bbar_re, precision=hp)
             - jnp.einsum('tghp,gpk->gtkh', m_im[:c], bbar_im, precision=hp))
        ks.append(k)
        pw = jnp.arange(c - 1, -1, -1) if d == 0 else jnp.arange(c)
        w_re = pr[pw][:, :, None, :] * jnp.swapaxes(bbar_re, 1, 2)[None] - pi[pw][:, :, None, :] * jnp.swapaxes(bbar_im, 1, 2)[None]
        w_im = pr[pw][:, :, None, :] * jnp.swapaxes(bbar_im, 1, 2)[None] + pi[pw][:, :, None, :] * jnp.swapaxes(bbar_re, 1, 2)[None]
        ws.append((jnp.transpose(w_re, (1, 0, 2, 3)).reshape(SSM_GROUPS, 256, SSM_STATE),
                   jnp.transpose(w_im, (1, 0, 2, 3)).reshape(SSM_GROUPS, 256, SSM_STATE)))
        po = jnp.arange(1, c + 1) if d == 0 else jnp.arange(c, 0, -1)
        v_re = m_re[po]
        v_im = m_im[po]
        vs.append((jnp.transpose(v_re, (1, 3, 0, 2)).reshape(SSM_GROUPS, SSM_STATE, 256),
                   jnp.transpose(-v_im, (1, 3, 0, 2)).reshape(SSM_GROUPS, SSM_STATE, 256)))
        lams.append((pr[c], pi[c]))
    j = jnp.arange(c)[:, None]
    t = jnp.arange(c)[None, :]
    k0 = ks[0][:, jnp.clip(t - j, 0, c - 1)] * (t >= j)[None, :, :, None, None]
    k1 = ks[1][:, jnp.clip(j - t, 0, c - 1)] * (j >= t)[None, :, :, None, None]
    tm = jnp.transpose(k0 + k1, (0, 1, 3, 2, 4)).reshape(SSM_GROUPS, 256, 256)
    dd = jnp.tile(p['ssm_d'], (1, c))
    tm = tm + jnp.eye(256, dtype=F32)[None] * dd[:, None, :]
    npair = SSM_GROUPS // 2

    def pair_diag(x):
        g, r, cc = x.shape
        x = x.reshape(npair, 2, r, cc)
        z = jnp.zeros((npair, 2, r, 2, cc), F32)
        z = z.at[:, 0, :, 0, :].set(x[:, 0]).at[:, 1, :, 1, :].set(x[:, 1])
        return z.reshape(npair, 2 * r, 2 * cc)

    w = jnp.concatenate([pair_diag(ws[0][0]), pair_diag(ws[0][1]), pair_diag(ws[1][0]), pair_diag(ws[1][1])], axis=2)
    v = jnp.concatenate([pair_diag(vs[0][0]), pair_diag(vs[0][1]), pair_diag(vs[1][0]), pair_diag(vs[1][1])], axis=1)
    lam = jnp.stack([lams[0][0], lams[0][1], lams[1][0], lams[1][1]], axis=0)
    lam = lam.reshape(4, SSM_GROUPS * SSM_STATE)
    lam = jnp.concatenate([lam, jnp.zeros_like(lam)], axis=0)
    return {'t': tm.astype(BF16), 'w': w.astype(BF16), 'v': v.astype(BF16), 'lam': lam}


GDN_HALO = 16


def _gdn_prep_body(nt, q_ref, qp_ref, qn_ref, k_ref, kp_ref, kn_ref, v_ref, vp_ref, vn_ref,
                   gate_ref, cw_ref, al_ref, dtb_ref, bd_ref, qo_ref, ko_ref, vo_ref, gb_ref):
    i = pl.program_id(1)
    cw = cw_ref[...]
    tl = q_ref.shape[0]

    def conv(cur_ref, prev_ref, next_ref, col):
        prev = jnp.where(i > 0, prev_ref[...].astype(F32), 0.0)
        nxt = jnp.where(i < nt - 1, next_ref[...].astype(F32), 0.0)
        xe = jnp.concatenate([prev, cur_ref[...].astype(F32), nxt], axis=0)
        n = xe.shape[0]
        acc = jnp.zeros((tl, 256), F32)
        for tap in range(5):
            s = tap - 2
            sh = xe if s == 0 else pltpu.roll(xe, (-s) % n, 0)
            acc = acc + sh[GDN_HALO:GDN_HALO + tl] * cw[tap:tap + 1, col:col + 256]
        return acc * jax.nn.sigmoid(acc)

    bd = bd_ref[...]

    def l2n(x):
        return x * lax.rsqrt(_head_mean_sq(x, bd) * HEAD_DIM + NORM_EPS)

    qo_ref[...] = l2n(conv(q_ref, qp_ref, qn_ref, 0)).astype(BF16)
    ko_ref[...] = l2n(conv(k_ref, kp_ref, kn_ref, 256)).astype(BF16)
    vo_ref[...] = conv(v_ref, vp_ref, vn_ref, 512).astype(BF16)
    gt = gate_ref[...]
    lane = lax.broadcasted_iota(jnp.int32, gt.shape, 1)
    x = gt + dtb_ref[...]
    softplus = jnp.maximum(x, 0.0) + jnp.log(1.0 + jnp.exp(-jnp.abs(x)))
    gb_ref[...] = jnp.where(lane < 8, -al_ref[...] * softplus, jnp.where(lane < 16, jax.nn.sigmoid(gt), 0.0))


def _gdn_prep(main, gate, q, bd, b, l):
    tl = min(256, l)
    nt = l // tl
    hb = tl // GDN_HALO
    nh = l // GDN_HALO

    def cur(col):
        return pl.BlockSpec((tl, 256), lambda bi, i: (bi * nt + i, col // 256))

    def prev(col):
        return pl.BlockSpec((GDN_HALO, 256), lambda bi, i: (bi * nh + jnp.maximum(i * hb - 1, 0), col // 256))

    def nxt(col):
        return pl.BlockSpec((GDN_HALO, 256), lambda bi, i: (bi * nh + jnp.minimum((i + 1) * hb, nh - 1), col // 256))

    full = lambda bi, i: (0, 0)
    row = lambda bi, i: (bi * nt + i, 0)
    specs = []
    for col in (COL_CQ, COL_CK, COL_CV):
        specs += [cur(col), prev(col), nxt(col)]
    specs += [pl.BlockSpec((tl, GATE_W), row), pl.BlockSpec((8, 768), full), pl.BlockSpec((1, LANES), full),
              pl.BlockSpec((1, LANES), full), pl.BlockSpec((256, 256), full)]
    return pl.pallas_call(
        functools.partial(_gdn_prep_body, nt),
        grid=(b, nt),
        in_specs=specs,
        out_specs=[pl.BlockSpec((tl, 256), row)] * 3 + [pl.BlockSpec((tl, GATE_W), row)],
        out_shape=[jax.ShapeDtypeStruct((b * l, 256), BF16)] * 3 + [jax.ShapeDtypeStruct((b * l, GATE_W), F32)],
        compiler_params=_cparams(("parallel", "parallel")),
        name="gdn_prep",
    )(*([main] * 9), gate, q['gdn_conv_w'], q['gdn_al'], q['gdn_dtb'], bd)


def _gdn_chunk_step(chains):
    c = GDN_CHUNK
    r256 = lax.broadcasted_iota(jnp.int32, (256, 256), 0)
    c256 = lax.broadcasted_iota(jnp.int32, (256, 256), 1)
    blockmask = (r256 // c) == (c256 // c)
    rl = lax.broadcasted_iota(jnp.int32, (LANES, 256), 0)
    cl = lax.broadcasted_iota(jnp.int32, (LANES, 256), 1)
    ri = lax.broadcasted_iota(jnp.int32, (c, c), 0)
    ci = lax.broadcasted_iota(jnp.int32, (c, c), 1)
    i_s = lax.broadcasted_iota(jnp.int32, (c, 256), 0)
    j_s = lax.broadcasted_iota(jnp.int32, (c, 256), 1) % c
    ones = jnp.ones((c, c), BF16)
    eye = (i_s == j_s).astype(F32)
    scale = HEAD_DIM ** -0.5

    def bdv(y):
        return jnp.where(blockmask, jnp.concatenate([y, y, y, y], axis=0), 0.0).astype(BF16)

    def hilo(x):
        hi = x.astype(BF16)
        return hi, (x - hi.astype(F32)).astype(BF16)

    st = []
    for dirn, qb, kb16, vb, gb, s_ref in chains:
        rev = dirn == 1
        e_g = (rl == dirn * 4 + cl // c).astype(BF16)
        e_b = (rl == 8 + dirn * 4 + cl // c).astype(BF16)
        ghi, glo = hilo(gb)
        st.append(dict(rev=rev, q=qb.astype(F32), k=kb16.astype(F32), v=vb.astype(F32), s_ref=s_ref,
                       g=_dot(ghi, e_g) + _dot(glo, e_g), beta=_dot(ghi, e_b) + _dot(glo, e_b),
                       tri=((ci >= ri) if rev else (ci <= ri)).astype(BF16),
                       allowed=(j_s >= i_s) if rev else (j_s <= i_s),
                       strict=(j_s > i_s) if rev else (j_s < i_s)))
    for d in st:
        ghi, glo = hilo(d['g'])
        d['gc'] = _dot(d['tri'], ghi) + _dot(d['tri'], glo)
    for d in st:
        zhi, zlo = hilo(jnp.where(i_s == j_s, d['gc'], 0.0))
        d['gct'] = _dot(ones, zhi) + _dot(ones, zlo)
    for d in st:
        d['decay'] = jnp.exp(jnp.where(d['allowed'], d['gc'] - d['gct'], -jnp.inf))
        d['eg'] = jnp.exp(d['gc'])
        d['kbeta'] = d['k'] * d['beta']
    for d in st:
        kk_qk = _dot_nt(jnp.concatenate([d['kbeta'], d['q'] * scale], axis=0).astype(BF16), bdv(d['k']))
        d['p'] = -jnp.where(d['strict'], kk_qk[:c] * d['decay'], 0.0)
        d['intra'] = jnp.where(d['allowed'], kk_qk[c:] * d['decay'], 0.0)
        d['t'] = eye + d['p']
    for _ in range(5):
        for d in st:
            d['p'] = _dot(d['p'].astype(BF16), bdv(d['p']))
        for d in st:
            d['t'] = d['t'] + _dot(d['t'].astype(BF16), bdv(d['p']))
    for d in st:
        t16 = d['t'].astype(BF16)
        d['u'] = _dot(t16, bdv(d['v'] * d['beta']))
        d['w'] = _dot(t16, bdv(d['kbeta'] * d['eg']))
    for d in st:
        d['s'] = d['s_ref'][...]
        d['ws_qs'] = _dot(jnp.concatenate([d['w'], d['q'] * scale * d['eg']], axis=0).astype(BF16),
                          d['s'].astype(BF16))
    for d in st:
        d['v_new'] = d['u'] - d['ws_qs'][:c]
        d['o'] = d['ws_qs'][c:] + _dot(d['intra'].astype(BF16), bdv(d['v_new']))
    for d in st:
        last = 0 if d['rev'] else c - 1
        g_last = d['gc'][last:last + 1, :]
        kg = (d['k'] * jnp.exp(g_last - d['gc'])).astype(BF16)
        upd = lax.dot_general(kg, d['v_new'].astype(BF16), (((0,), (0,)), ((), ())), preferred_element_type=F32)
        d['s_ref'][...] = d['s'] * jnp.exp(g_last) + jnp.where(blockmask, upd, 0.0)
    return [d['o'] for d in st]


GDN_SEQS = 2


def _gdn_chunk_body(qf_ref, kf_ref, vf_ref, gf_ref, qb_ref, kb_ref, vb_ref, gb_ref, of_ref, ob_ref, s_ref):
    @pl.when(pl.program_id(1) == 0)
    def _():
        s_ref[...] = jnp.zeros(s_ref.shape, F32)

    chains = []
    for j in range(qf_ref.shape[0]):
        chains.append((0, qf_ref[j], kf_ref[j], vf_ref[j], gf_ref[j], s_ref.at[j, 0]))
        chains.append((1, qb_ref[j], kb_ref[j], vb_ref[j], gb_ref[j], s_ref.at[j, 1]))
    outs = _gdn_chunk_step(chains)
    for j in range(qf_ref.shape[0]):
        of_ref[j] = outs[2 * j].astype(BF16)
        ob_ref[j] = outs[2 * j + 1].astype(BF16)


def _gdn_chunks(qn, kn, vs, gb, b, l):
    c = GDN_CHUNK
    n = l // c
    nseq = GDN_SEQS if b % GDN_SEQS == 0 else 1
    fwd = lambda bi, i: (bi, i, 0)
    bwd = lambda bi, i: (bi, n - 1 - i, 0)
    blk = lambda m: pl.BlockSpec((nseq, c, 256), m)
    gblk = lambda m: pl.BlockSpec((nseq, c, GATE_W), m)
    qn, kn, vs = (a.reshape(b, l, 256) for a in (qn, kn, vs))
    gb = gb.reshape(b, l, GATE_W)
    of, ob = pl.pallas_call(
        _gdn_chunk_body,
        grid=(b // nseq, n),
        in_specs=[blk(fwd), blk(fwd), blk(fwd), gblk(fwd), blk(bwd), blk(bwd), blk(bwd), gblk(bwd)],
        out_specs=[blk(fwd), blk(bwd)],
        out_shape=[jax.ShapeDtypeStruct((b, l, 256), BF16)] * 2,
        scratch_shapes=[pltpu.VMEM((nseq, 2, 256, 256), F32)],
        compiler_params=_cparams(("parallel", "arbitrary")),
        name="gdn_chunks",
    )(qn, kn, vs, gb, qn, kn, vs, gb)
    return of.reshape(b * l, 256), ob.reshape(b * l, 256)


def _gdn_out_body(of_ref, ob_ref, z_ref, g_ref, bd_ref, o_ref):
    o = of_ref[...].astype(F32) + ob_ref[...].astype(F32)
    z = z_ref[...].astype(F32)
    ms = _head_mean_sq(o, bd_ref[...])
    o_ref[...] = (o * lax.rsqrt(ms + NORM_EPS) * g_ref[...] * (z * jax.nn.sigmoid(z))).astype(BF16)


def _gdn_out(of, ob, main, g, bd):
    t = of.shape[0]
    tm = min(1024, t)
    row = lambda i: (i, 0)
    full = lambda i: (0, 0)
    return pl.pallas_call(
        _gdn_out_body,
        grid=(t // tm,),
        in_specs=[pl.BlockSpec((tm, 256), row), pl.BlockSpec((tm, 256), row),
                  pl.BlockSpec((tm, 256), lambda i: (i, COL_CZ // 256)),
                  pl.BlockSpec((1, 256), full), pl.BlockSpec((256, 256), full)],
        out_specs=pl.BlockSpec((tm, 256), row),
        out_shape=jax.ShapeDtypeStruct((t, 256), BF16),
        compiler_params=_cparams(("parallel",)),
        name="gdn_out",
    )(of, ob, main, g, bd)


def _gdn(main, gate, q, bd, b, l):
    qn, kn, vs, gb = _gdn_prep(main, gate, q, bd, b, l)
    of, ob = _gdn_chunks(qn, kn, vs, gb, b, l)
    return _gdn_out(of, ob, main, q['gdn_o_norm'], bd)


def _moe(x1, x1b, top_idx, gates, p):
    t = x1.shape[0]
    m = t * TOP_K
    flat_e = top_idx.reshape(-1)
    order = jnp.argsort(flat_e, stable=True).astype(jnp.int32)
    inv = jnp.argsort(order).astype(jnp.int32)
    sorted_e = flat_e[order]
    experts = jnp.arange(N_EXPERTS, dtype=jnp.int32)
    counts = jnp.sum((flat_e[:, None] == experts[None, :]).astype(jnp.int32), axis=0)
    padded = (counts + MOE_BLOCK - 1) // MOE_BLOCK * MOE_BLOCK
    group_start = jnp.cumsum(counts) - counts
    padded_end = jnp.cumsum(padded)
    padded_start = padded_end - padded
    dest = (padded_start[sorted_e] + jnp.arange(m, dtype=jnp.int32) - group_start[sorted_e]).astype(jnp.int32)
    n_blocks = -(-m // MOE_BLOCK) + N_EXPERTS
    n_slots = n_blocks * MOE_BLOCK
    block_start = jnp.arange(n_blocks, dtype=jnp.int32) * MOE_BLOCK
    block_expert = jnp.minimum(jnp.sum((padded_end[None, :] <= block_start[:, None]).astype(jnp.int32), axis=1),
                               N_EXPERTS - 1).astype(jnp.int32)
    n_used = (padded_end[-1] // MOE_BLOCK).astype(jnp.int32).reshape(1)
    slot = jnp.arange(n_slots, dtype=jnp.int32)
    slot_e = jnp.repeat(block_expert, MOE_BLOCK)
    rank = slot - padded_start[slot_e]
    src = jnp.clip(group_start[slot_e] + rank, 0, m - 1)
    slot_token = jnp.where(rank < counts[slot_e], order[src] // TOP_K, 0).astype(jnp.int32)
    n_ranges = MOE_RANGES if n_blocks % MOE_RANGES == 0 else 1
    rblocks = n_blocks // n_ranges
    yb = None
    for r in range(n_ranges):
        b0 = r * rblocks
        xb = x1b[slot_token[b0 * MOE_BLOCK:(b0 + rblocks) * MOE_BLOCK]]
        yb = _ffn(xb, block_expert[b0:b0 + rblocks], jnp.clip(n_used - b0, 0, rblocks),
                  p['moe_wg'], p['moe_wl'], p['moe_bg'], p['moe_bl'], p['moe_w2'], p['moe_b2'],
                  yb, n_slots, b0)
    dest_tk = dest[inv].reshape(t, TOP_K)
    return jnp.sum(yb[dest_tk].astype(F32) * gates[..., None], axis=1)


def _block_diag_mean(width, group):
    idx = np.arange(width)
    return jnp.asarray((idx[:, None] // group == idx[None, :] // group) / group, BF16)


def _axial_tables(l):
    rows = l // GRID_W
    row_pos = np.repeat(np.arange(rows), GRID_W).astype(np.float64)
    col_pos = np.tile(np.arange(GRID_W), rows).astype(np.float64)
    lane = np.arange(LANES)
    d = lane % HEAD_DIM
    e = d % 32
    f = e % 16
    inv = (AXIAL_THETA ** (-(np.arange(0, 32, 2, dtype=np.float32)) / 32)).astype(np.float32)
    pos = np.where((d // 32)[None, :] == 0, row_pos[:, None], col_pos[:, None]).astype(np.float32)
    ang = pos * inv[f][None, :]
    sign = np.where(e < 16, -1.0, 1.0)[None, :]
    return jnp.asarray(np.cos(ang), F32), jnp.asarray(np.sin(ang) * sign, F32)


def _diff_tables(l):
    lane = np.arange(256)
    e = lane % 32
    f = e % 4
    inv = (ROPE_THETA ** (-(np.arange(0, PARTIAL_ROPE_DIMS, 2, dtype=np.float32)) / PARTIAL_ROPE_DIMS)).astype(np.float32)
    ang = np.arange(l, dtype=np.float32)[:, None] * inv[f][None, :]
    roped = (e < PARTIAL_ROPE_DIMS)[None, :]
    sign = np.where(e < 4, -1.0, 1.0)[None, :]
    c = np.where(roped, np.cos(ang), 1.0)
    s = np.where(roped, np.sin(ang) * sign, 0.0)
    return jnp.asarray(c, F32), jnp.asarray(s, F32)


def _prep_layer(params, layer, w1_all):
    p = {name: arr[layer] for name, arr in params.items()}
    w_in = p['w_in']
    w_in = jnp.concatenate([w_in[:, :ORIG_CA], w_in[:, ORIG_CA + 16:], w_in[:, ORIG_CA:ORIG_CA + 16],
                            jnp.zeros((D_MODEL, GATE_W - 16), F32)], axis=1)
    q = dict(p)
    q['w_in'] = w_in.astype(BF16)
    q['w_out'] = p['w_out'].astype(BF16)
    q['router_w'] = jnp.concatenate([p['router_w'], jnp.zeros((D_MODEL, LANES - N_EXPERTS), F32)], axis=1).astype(BF16)
    q['router_b'] = jnp.concatenate([p['router_b'], jnp.full((LANES - N_EXPERTS,), -1e30, F32)])[None, :]
    q['moe_wg'] = w1_all[0][layer]
    q['moe_wl'] = w1_all[1][layer]
    q['moe_bg'] = p['moe_b1'][:, None, 0::2]
    q['moe_bl'] = p['moe_b1'][:, None, 1::2]
    q['moe_w2'] = p['moe_w2'].astype(BF16)
    q['moe_b2'] = p['moe_b2'][:, None, :]
    for name in ('ln1_g', 'ln1_b', 'ln2_g', 'ln2_b'):
        q[name] = p[name][None, :]
    q['gqa_q_norm'] = jnp.tile(p['gqa_q_norm'], 4)[None, :]
    q['gqa_k_norm'] = jnp.tile(p['gqa_k_norm'], 2)[None, :]
    lambda_init = 0.8 - 0.6 * math.exp(-0.3 * layer)
    q['diff_subln'] = (jnp.tile(p['diff_subln'], 4) * (1.0 - lambda_init))[None, :]
    lam = (jnp.exp(jnp.sum(p['diff_lambda_q1'] * p['diff_lambda_k1']))
           - jnp.exp(jnp.sum(p['diff_lambda_q2'] * p['diff_lambda_k2'])) + lambda_init)
    q['diff_lam'] = lam.reshape(1).astype(F32)
    q['s5'] = _s5_prep(p)
    q['ssm_glu_w'] = p['ssm_glu_w'].astype(BF16)
    q['ssm_glu_b'] = p['ssm_glu_b'][None, :]
    q['gdn_conv_w'] = jnp.concatenate([p['gdn_conv_w'], jnp.zeros((3, 768), F32)], axis=0)
    pad8 = lambda x: jnp.concatenate([x.reshape(-1), jnp.zeros((LANES - 8,), F32)])[None, :]
    q['gdn_al'] = pad8(jnp.exp(p['gdn_a_log']))
    q['gdn_dtb'] = pad8(p['gdn_dt_bias'])
    q['gdn_o_norm'] = jnp.tile(p['gdn_o_norm'], 4)[None, :]
    return q


def _mixers(main, gate, ux, q, tabs, b, l):
    a_out = _s5(ux, q, tabs['s5sel'], b, l)
    b_out = _gqa(main, tabs['axial'], q['gqa_q_norm'], q['gqa_k_norm'], tabs['bd64'], b, l)
    c_out = _gdn(main, gate, q, tabs['bd64'], b, l)
    d_out = _diff(main, q['diff_lam'], tabs['diff'], q['diff_subln'], tabs['bd64'], b, l)
    return a_out, b_out, c_out, d_out


def _layer(x, q, tabs, b, l, alpha):
    main, gate, ux = _inproj(x, q['w_in'])
    pieces = _mixers(main, gate, ux, q, tabs, b, l)
    x1, x1b, idx, gates = _outproj(pieces, q['w_out'], x, q['ln1_g'], q['ln1_b'],
                                   q['router_w'], q['router_b'], alpha)
    ffn = _moe(x1, x1b, idx[:, :TOP_K], gates[:, :TOP_K], q)
    return _ln2(x1, ffn, q['ln2_g'], q['ln2_b'], alpha)


def _trunk(x, layers, alpha):
    b, l, _ = x.shape
    tabs = {'axial': _axial_tables(l), 'diff': _diff_tables(l), 'bd64': _block_diag_mean(256, HEAD_DIM),
            's5sel': _s5_sel()}
    h = x.reshape(b * l, D_MODEL)
    for q in layers:
        h = _layer(h, q, tabs, b, l, alpha)
    return h.reshape(b, l, D_MODEL)


def kernel(x_prompt, x_sample, w_in, w_out, ssm_a_re, ssm_a_im, ssm_log_dt, ssm_b_re, ssm_b_im, ssm_c_re, ssm_c_im, ssm_d, ssm_glu_w, ssm_glu_b, gqa_q_norm, gqa_k_norm, gdn_conv_w, gdn_a_log, gdn_dt_bias, gdn_o_norm, diff_lambda_q1, diff_lambda_k1, diff_lambda_q2, diff_lambda_k2, diff_subln, router_w, router_b, moe_w1, moe_b1, moe_w2, moe_b2, ln1_g, ln1_b, ln2_g, ln2_b):
    params = {
        'w_in': w_in, 'w_out': w_out,
        'ssm_a_re': ssm_a_re, 'ssm_a_im': ssm_a_im, 'ssm_log_dt': ssm_log_dt,
        'ssm_b_re': ssm_b_re, 'ssm_b_im': ssm_b_im, 'ssm_c_re': ssm_c_re, 'ssm_c_im': ssm_c_im,
        'ssm_d': ssm_d, 'ssm_glu_w': ssm_glu_w, 'ssm_glu_b': ssm_glu_b,
        'gqa_q_norm': gqa_q_norm, 'gqa_k_norm': gqa_k_norm,
        'gdn_conv_w': gdn_conv_w, 'gdn_a_log': gdn_a_log, 'gdn_dt_bias': gdn_dt_bias, 'gdn_o_norm': gdn_o_norm,
        'diff_lambda_q1': diff_lambda_q1, 'diff_lambda_k1': diff_lambda_k1,
        'diff_lambda_q2': diff_lambda_q2, 'diff_lambda_k2': diff_lambda_k2, 'diff_subln': diff_subln,
        'router_w': router_w, 'router_b': router_b,
        'moe_w1': moe_w1, 'moe_b1': moe_b1, 'moe_w2': moe_w2, 'moe_b2': moe_b2,
        'ln1_g': ln1_g, 'ln1_b': ln1_b, 'ln2_g': ln2_g, 'ln2_b': ln2_b,
    }
    depth = w_in.shape[0]
    alpha = (2.0 * depth) ** 0.25
    w1_all = [w.reshape(moe_w1.shape[:3] + (-1,)) for w in _w1_prep(moe_w1.reshape((-1,) + moe_w1.shape[2:]))]
    layers = [_prep_layer(params, layer, w1_all) for layer in range(depth)]
    return (_trunk(x_prompt, layers, alpha), _trunk(x_sample, layers, alpha))
```

```python
import functools
import math

import jax
import jax.numpy as jnp
import numpy as np
from jax import lax
from jax.experimental import pallas as pl
from jax.experimental.pallas import tpu as pltpu

F32 = jnp.float32
BF16 = jnp.bfloat16

D_MODEL = 1024
GROUP_WIDTH = 256
HEAD_DIM = 64
SSM_GROUPS = 16
SSM_GROUP_CH = 16
SSM_STATE = 64
GDN_HEADS = 4
GDN_CHUNK = 64
DIFF_SUB_DIM = 32
PARTIAL_ROPE_DIMS = 8
AXIAL_THETA = 10000.0
ROPE_THETA = 500000.0
GRID_W = 64
N_EXPERTS = 32
TOP_K = 4
SWIGLU_ALPHA = 1.702
SWIGLU_LIMIT = 7.0
NORM_EPS = 1e-6
LN_EPS = 1e-5
LOG2E = math.log2(math.e)

LANES = 128
VMEM_LIMIT = 56 * 1024 * 1024

COL_U, COL_GQ, COL_GK, COL_GV = 0, 256, 512, 640
COL_CQ, COL_CK, COL_CV, COL_CZ = 768, 1024, 1280, 1536
COL_DQ, COL_DK, COL_DV = 1792, 2048, 2304
MAIN_W = 2560
GATE_W = 128
ORIG_CA = 1792


def _cparams(sem):
    return pltpu.CompilerParams(dimension_semantics=sem, vmem_limit_bytes=VMEM_LIMIT)


def _dot(a, b):
    return jnp.dot(a, b, preferred_element_type=F32)


def _dot_nt(a, b):
    return lax.dot_general(a, b, (((1,), (1,)), ((), ())), preferred_element_type=F32)


def _split_dot(x, m):
    hi = x.astype(BF16)
    lo = (x - hi.astype(F32)).astype(BF16)
    return _dot(hi, m) + _dot(lo, m)


def _chunk_rows_sel(n_chunks, t):
    r = lax.broadcasted_iota(jnp.int32, (n_chunks, S5_CHUNK * n_chunks), 0)
    c = lax.broadcasted_iota(jnp.int32, (n_chunks, S5_CHUNK * n_chunks), 1)
    return (c == S5_CHUNK * r + t).astype(BF16)


def _inproj_body(x_ref, w_ref, main_ref, gate_ref, ux_ref):
    x = x_ref[...].astype(BF16)
    step = 640
    for c in range(MAIN_W // step):
        main_ref[:, c * step:(c + 1) * step] = _dot(x, w_ref[:, c * step:(c + 1) * step]).astype(BF16)
    gate_ref[...] = _dot(x, w_ref[:, MAIN_W:])
    u = main_ref[:, COL_U:COL_U + 256]
    for t in range(S5_CHUNK):
        ux_ref[:, t * 256:(t + 1) * 256] = _dot(_chunk_rows_sel(ux_ref.shape[0], t), u).astype(BF16)


def _inproj(x, w):
    t = x.shape[0]
    tm = min(512, t)
    return pl.pallas_call(
        _inproj_body,
        grid=(t // tm,),
        in_specs=[pl.BlockSpec((tm, D_MODEL), lambda i: (i, 0)),
                  pl.BlockSpec((D_MODEL, MAIN_W + GATE_W), lambda i: (0, 0))],
        out_specs=[pl.BlockSpec((tm, MAIN_W), lambda i: (i, 0)),
                   pl.BlockSpec((tm, GATE_W), lambda i: (i, 0)),
                   pl.BlockSpec((tm // S5_CHUNK, S5_CHUNK * 256), lambda i: (i, 0))],
        out_shape=[jax.ShapeDtypeStruct((t, MAIN_W), BF16),
                   jax.ShapeDtypeStruct((t, GATE_W), F32),
                   jax.ShapeDtypeStruct((t // S5_CHUNK, S5_CHUNK * 256), BF16)],
        compiler_params=_cparams(("parallel",)),
        name="inproj",
    )(x, w)


def _head_mean_sq(xf, bd):
    return _split_dot(xf * xf, bd)


def _rope_lanes(x, c, s, half):
    n = x.shape[-1]
    lane = lax.broadcasted_iota(jnp.int32, x.shape, 1)
    first = (lane % (2 * half)) < half
    swapped = jnp.where(first, pltpu.roll(x, n - half, 1), pltpu.roll(x, half, 1))
    return x * c + swapped * s


def _gqa_body(q_ref, k_ref, v_ref, cq_ref, sq_ref, ck_ref, sk_ref, gq_ref, gk_ref, bd_ref,
              o_ref, kdup_ref, vaug_ref):
    i = pl.program_id(1)
    bd = bd_ref[...]

    @pl.when(i == 0)
    def _():
        kf = k_ref[...].astype(F32)
        ms = _head_mean_sq(kf, bd[:LANES, :LANES])
        kn = kf * lax.rsqrt(ms + NORM_EPS) * gk_ref[...]
        kn = _rope_lanes(kn, ck_ref[...], sk_ref[...], 16)
        lane = lax.broadcasted_iota(jnp.int32, kn.shape, 1)
        sw = pltpu.roll(kn, 64, 1)
        kdup_ref[0] = jnp.where(lane < 64, kn, sw).astype(BF16)
        kdup_ref[1] = jnp.where(lane < 64, sw, kn).astype(BF16)
        v = v_ref[...]
        one = jnp.ones(v.shape, BF16)
        vaug_ref[0] = jnp.where(lane < 64, v, one)
        vaug_ref[1] = jnp.where(lane < 64, one, v)

    qf = q_ref[...].astype(F32)
    ms = _head_mean_sq(qf, bd)
    qn = qf * lax.rsqrt(ms + NORM_EPS) * gq_ref[...]
    cq = jnp.concatenate([cq_ref[...], cq_ref[...]], axis=1)
    sq = jnp.concatenate([sq_ref[...], sq_ref[...]], axis=1)
    qn = (_rope_lanes(qn, cq, sq, 16) * (HEAD_DIM ** -0.5 * LOG2E)).astype(BF16)
    tq = qn.shape[0]
    lane = lax.broadcasted_iota(jnp.int32, (tq, LANES), 1)
    zero = jnp.zeros((tq, LANES), BF16)
    for h in range(2):
        qh = qn[:, h * LANES:(h + 1) * LANES]
        q2 = jnp.concatenate([jnp.where(lane < 64, qh, zero), jnp.where(lane < 64, zero, qh)], axis=0)
        s = _dot_nt(q2, kdup_ref[h])
        m = jnp.max(s, axis=-1, keepdims=True)
        p = jnp.exp2(s - m).astype(BF16)
        o2 = _dot(p, vaug_ref[h])
        o2 = o2 / pltpu.roll(o2, 64, 1)
        top, bot = o2[:tq], o2[tq:]
        if h == 0:
            oh = jnp.where(lane < 64, top, pltpu.roll(bot, 64, 1))
        else:
            oh = jnp.where(lane < 64, pltpu.roll(top, 64, 1), bot)
        o_ref[:, h * LANES:(h + 1) * LANES] = oh.astype(BF16)


def _gqa(main, tabs, gq, gk, bd, b, l):
    tq = min(256, l)
    nq = l // tq
    cq, sq = tabs
    full = lambda bi, i: (0, 0)
    return pl.pallas_call(
        _gqa_body,
        grid=(b, nq),
        in_specs=[pl.BlockSpec((tq, 256), lambda bi, i: (bi * nq + i, COL_GQ // 256)),
                  pl.BlockSpec((l, LANES), lambda bi, i: (bi, COL_GK // LANES)),
                  pl.BlockSpec((l, LANES), lambda bi, i: (bi, COL_GV // LANES)),
                  pl.BlockSpec((tq, LANES), lambda bi, i: (i, 0)),
                  pl.BlockSpec((tq, LANES), lambda bi, i: (i, 0)),
                  pl.BlockSpec((l, LANES), full),
                  pl.BlockSpec((l, LANES), full),
                  pl.BlockSpec((1, 256), full),
                  pl.BlockSpec((1, LANES), full),
                  pl.BlockSpec((256, 256), full)],
        out_specs=pl.BlockSpec((tq, 256), lambda bi, i: (bi * nq + i, 0)),
        out_shape=jax.ShapeDtypeStruct((b * l, 256), BF16),
        scratch_shapes=[pltpu.VMEM((2, l, LANES), BF16), pltpu.VMEM((2, l, LANES), BF16)],
        compiler_params=_cparams(("parallel", "arbitrary")),
        name="gqa",
    )(main, main, main, cq, sq, cq, sq, gq, gk, bd)


def _diff_body(lam_ref, q_ref, k_ref, v_ref, cq_ref, sq_ref, ck_ref, sk_ref, g_ref, bd_ref,
               o_ref, kr_ref, vaug_ref):
    i = pl.program_id(1)

    @pl.when(i == 0)
    def _():
        kr_ref[...] = _rope_lanes(k_ref[...].astype(F32), ck_ref[...], sk_ref[...], 4).astype(BF16)
        v = v_ref[...]
        vlane = lax.broadcasted_iota(jnp.int32, v.shape, 1)
        one = jnp.ones(v.shape, BF16)
        for h in range(4):
            vaug_ref[h] = jnp.where(vlane // 64 == h, v, one)

    lam = lam_ref[0]
    qr = _rope_lanes(q_ref[...].astype(F32), cq_ref[...], sq_ref[...], 4)
    qr = (qr * (DIFF_SUB_DIM ** -0.5 * LOG2E)).astype(BF16)
    tq = qr.shape[0]
    lane = lax.broadcasted_iota(jnp.int32, (tq, 256), 1)
    zero = jnp.zeros((tq, 256), BF16)
    kr = kr_ref[...]
    acc = jnp.zeros((tq, 256), F32)
    for h in range(4):
        q2 = jnp.concatenate([jnp.where(lane // 32 == 2 * h, qr, zero),
                              jnp.where(lane // 32 == 2 * h + 1, qr, zero)], axis=0)
        s = _dot_nt(q2, kr)
        m = jnp.max(s, axis=-1, keepdims=True)
        e = jnp.exp2(s - m).astype(BF16)
        o2 = _dot(e, vaug_ref[h])
        o2 = o2 / pltpu.roll(o2, 64, 1)
        o = o2[:tq] - lam * o2[tq:]
        acc = jnp.where(lane // 64 == h, o, acc)
    ms = _head_mean_sq(acc, bd_ref[...])
    o_ref[...] = (acc * lax.rsqrt(ms + NORM_EPS) * g_ref[...]).astype(BF16)


def _diff(main, lam, tabs, g, bd, b, l):
    tq = min(256, l)
    nq = l // tq
    c, s = tabs
    full = lambda bi, i, *_: (0, 0)
    once = pl.Buffered(1)
    grid_spec = pltpu.PrefetchScalarGridSpec(
        num_scalar_prefetch=1,
        grid=(b, nq),
        in_specs=[pl.BlockSpec((tq, 256), lambda bi, i, *_: (bi * nq + i, COL_DQ // 256)),
                  pl.BlockSpec((l, 256), lambda bi, i, *_: (bi, COL_DK // 256)),
                  pl.BlockSpec((l, 256), lambda bi, i, *_: (bi, COL_DV // 256)),
                  pl.BlockSpec((tq, 256), lambda bi, i, *_: (i, 0)),
                  pl.BlockSpec((tq, 256), lambda bi, i, *_: (i, 0)),
                  pl.BlockSpec((l, 256), full, pipeline_mode=once),
                  pl.BlockSpec((l, 256), full, pipeline_mode=once),
                  pl.BlockSpec((1, 256), full),
                  pl.BlockSpec((256, 256), full)],
        out_specs=pl.BlockSpec((tq, 256), lambda bi, i, *_: (bi * nq + i, 0)),
        scratch_shapes=[pltpu.VMEM((l, 256), BF16), pltpu.VMEM((4, l, 256), BF16)],
    )
    return pl.pallas_call(
        _diff_body,
        grid_spec=grid_spec,
        out_shape=jax.ShapeDtypeStruct((b * l, 256), BF16),
        compiler_params=_cparams(("parallel", "arbitrary")),
        name="diffattn",
    )(lam, main, main, main, c, s, c, s, g, bd)


def _layer_norm_rows(y, g, b):
    mu = jnp.mean(y, axis=-1, keepdims=True)
    d = y - mu
    var = jnp.mean(d * d, axis=-1, keepdims=True)
    return d * lax.rsqrt(var + LN_EPS) * g + b


def _outproj_body(alpha, a_ref, b_ref, c_ref, d_ref, w_ref, x_ref, g_ref, beta_ref, rw_ref, rb_ref,
                  x1_ref, x1b_ref, idx_ref, gate_ref):
    nch = a_ref.shape[0]
    a = jnp.zeros((nch * S5_CHUNK, 256), F32)
    r = lax.broadcasted_iota(jnp.int32, (nch * S5_CHUNK, nch), 0)
    n = lax.broadcasted_iota(jnp.int32, (nch * S5_CHUNK, nch), 1)
    for t in range(S5_CHUNK):
        a = a + _dot((r == S5_CHUNK * n + t).astype(BF16), a_ref[:, t * 256:(t + 1) * 256])
    mixed = (_dot(a.astype(BF16), w_ref[0:256, :]) + _dot(b_ref[...], w_ref[256:512, :])
             + _dot(c_ref[...], w_ref[512:768, :]) + _dot(d_ref[...], w_ref[768:1024, :]))
    x1 = _layer_norm_rows(alpha * x_ref[...] + mixed, g_ref[...], beta_ref[...])
    x1_ref[...] = x1
    x1b = x1.astype(BF16)
    x1b_ref[...] = x1b
    logits = _dot(x1b, rw_ref[...]) + rb_ref[...]
    lane = lax.broadcasted_iota(jnp.int32, logits.shape, 1)
    vals, idxs = [], []
    for _ in range(TOP_K):
        m = jnp.max(logits, axis=-1, keepdims=True)
        ix = jnp.min(jnp.where(logits == m, lane, LANES), axis=-1, keepdims=True)
        vals.append(m)
        idxs.append(ix)
        logits = jnp.where(lane == ix, -jnp.inf, logits)
    es = [jnp.exp(vk - vals[0]) for vk in vals]
    tot = es[0] + es[1] + es[2] + es[3]
    idx_out = jnp.zeros(logits.shape, F32)
    gate_out = jnp.zeros(logits.shape, F32)
    for k in range(TOP_K):
        idx_out = jnp.where(lane == k, idxs[k].astype(F32), idx_out)
        gate_out = jnp.where(lane == k, es[k] / tot, gate_out)
    idx_ref[...] = jnp.transpose(idx_out)[:8, :].astype(jnp.int32)
    gate_ref[...] = gate_out


def _outproj(pieces, w, x, g, beta, rw, rb, alpha):
    t = x.shape[0]
    tm = min(512, t)
    row = lambda i: (i, 0)
    full = lambda i: (0, 0)
    return pl.pallas_call(
        functools.partial(_outproj_body, alpha),
        grid=(t // tm,),
        in_specs=[pl.BlockSpec((tm // S5_CHUNK, S5_CHUNK * 256), row)] + [pl.BlockSpec((tm, 256), row)] * 3 + [
            pl.BlockSpec((D_MODEL, D_MODEL), full),
            pl.BlockSpec((tm, D_MODEL), row),
            pl.BlockSpec((1, D_MODEL), full),
            pl.BlockSpec((1, D_MODEL), full),
            pl.BlockSpec((D_MODEL, LANES), full),
            pl.BlockSpec((1, LANES), full)],
        out_specs=[pl.BlockSpec((tm, D_MODEL), row), pl.BlockSpec((tm, D_MODEL), row),
                   pl.BlockSpec((8, tm), lambda i: (0, i)), pl.BlockSpec((tm, LANES), row)],
        out_shape=[jax.ShapeDtypeStruct((t, D_MODEL), F32), jax.ShapeDtypeStruct((t, D_MODEL), BF16),
                   jax.ShapeDtypeStruct((8, t), jnp.int32), jax.ShapeDtypeStruct((t, LANES), F32)],
        compiler_params=_cparams(("parallel",)),
        name="outproj_ln_router",
    )(*pieces, w, x, g, beta, rw, rb)


MOE_BLOCK = 512
MOE_RANGES = 4


def _ffn_body(be_ref, nb_ref, x_ref, wg_ref, wl_ref, bg_ref, bl_ref, w2_ref, b2_ref, *rest):
    o_ref = rest[-1]
    i = pl.program_id(0)

    @pl.when(i < nb_ref[0])
    def _():
        x = x_ref[...]
        glu = jnp.minimum(_dot(x, wg_ref[...]) + bg_ref[...], SWIGLU_LIMIT)
        lin = jnp.clip(_dot(x, wl_ref[...]) + bl_ref[...], -SWIGLU_LIMIT, SWIGLU_LIMIT)
        act = (glu * jax.nn.sigmoid(SWIGLU_ALPHA * glu) * (lin + 1.0)).astype(BF16)
        o_ref[...] = (_dot(act, w2_ref[...]) + b2_ref[...]).astype(o_ref.dtype)

    @pl.when(i >= nb_ref[0])
    def _():
        o_ref[...] = jnp.zeros(o_ref.shape, o_ref.dtype)


def _ffn(xb, block_expert, n_used, wg, wl, bg, bl, w2, b2, yb_prev, n_slots, block_offset):
    n_blocks = xb.shape[0] // MOE_BLOCK
    de = wg.shape[2]
    expert = lambda i, be, nb: (be[i], 0, 0)
    in_specs = [pl.BlockSpec((MOE_BLOCK, D_MODEL), lambda i, be, nb: (i, 0)),
                pl.BlockSpec((None, D_MODEL, de), expert),
                pl.BlockSpec((None, D_MODEL, de), expert),
                pl.BlockSpec((None, 1, de), expert),
                pl.BlockSpec((None, 1, de), expert),
                pl.BlockSpec((None, de, D_MODEL), expert),
                pl.BlockSpec((None, 1, D_MODEL), expert)]
    args = [block_expert, n_used, xb, wg, wl, bg, bl, w2, b2]
    aliases = {}
    if yb_prev is not None:
        in_specs.append(pl.BlockSpec(memory_space=pl.ANY))
        aliases = {len(args): 0}
        args.append(yb_prev)
    grid_spec = pltpu.PrefetchScalarGridSpec(
        num_scalar_prefetch=2,
        grid=(n_blocks,),
        in_specs=in_specs,
        out_specs=pl.BlockSpec((MOE_BLOCK, D_MODEL), lambda i, be, nb: (i + block_offset, 0)),
    )
    return pl.pallas_call(
        _ffn_body,
        grid_spec=grid_spec,
        out_shape=jax.ShapeDtypeStruct((n_slots, D_MODEL), BF16),
        input_output_aliases=aliases,
        compiler_params=_cparams(("arbitrary",)),
        name="moe_ffn",
    )(*args)


def _w1_prep_body(w_ref, pg_ref, pl_ref, g_ref, l_ref):
    w = w_ref[...].astype(BF16)
    g_ref[...] = _dot(w, pg_ref[...]).astype(BF16)
    l_ref[...] = _dot(w, pl_ref[...]).astype(BF16)


def _w1_prep(w1):
    ne, dm, de2 = w1.shape
    tn = 512
    r = np.arange(tn)[:, None]
    c = np.arange(tn // 2)[None, :]
    sel_g = jnp.asarray(r == 2 * c, BF16)
    sel_l = jnp.asarray(r == 2 * c + 1, BF16)
    full = lambda e, j: (0, 0)
    out = jax.ShapeDtypeStruct((ne, dm, de2 // 2), BF16)
    return pl.pallas_call(
        _w1_prep_body,
        grid=(ne, de2 // tn),
        in_specs=[pl.BlockSpec((None, dm, tn), lambda e, j: (e, 0, j)),
                  pl.BlockSpec((tn, tn // 2), full), pl.BlockSpec((tn, tn // 2), full)],
        out_specs=[pl.BlockSpec((None, dm, tn // 2), lambda e, j: (e, 0, j))] * 2,
        out_shape=[out, out],
        compiler_params=_cparams(("parallel", "parallel")),
        name="w1_prep",
    )(w1, sel_g, sel_l)


def _ln2_body(alpha, x_ref, f_ref, g_ref, b_ref, o_ref):
    o_ref[...] = _layer_norm_rows(alpha * x_ref[...] + f_ref[...], g_ref[...], b_ref[...])


def _ln2(x, f, g, b, alpha):
    t = x.shape[0]
    tm = min(512, t)
    row = lambda i: (i, 0)
    full = lambda i: (0, 0)
    return pl.pallas_call(
        functools.partial(_ln2_body, alpha),
        grid=(t // tm,),
        in_specs=[pl.BlockSpec((tm, D_MODEL), row), pl.BlockSpec((tm, D_MODEL), row),
                  pl.BlockSpec((1, D_MODEL), full), pl.BlockSpec((1, D_MODEL), full)],
        out_specs=pl.BlockSpec((tm, D_MODEL), row),
        out_shape=jax.ShapeDtypeStruct((t, D_MODEL), F32),
        compiler_params=_cparams(("parallel",)),
        name="ln2",
    )(x, f, g, b)


S5_CHUNK = 16


S5_SEL_PAD = (SSM_GROUPS - 1) * SSM_GROUP_CH
S5_ROWS = 256


def _s5_sel():
    c = np.arange(256)
    target = (c // 16) * 256 + c % 16
    r = np.arange(S5_SEL_PAD + 4096)[:, None] - S5_SEL_PAD
    return jnp.asarray(r == target[None, :], BF16)


def _s5_in_body(x_ref, sel_ref, t_ref, w_ref, y_ref, h_ref):
    xcat = x_ref[...]
    us = []
    for g in range(SSM_GROUPS):
        start = S5_SEL_PAD - SSM_GROUP_CH * g
        u = _dot(xcat, sel_ref[start:start + 4096, :]).astype(BF16)
        y_ref[g] = _dot(u, t_ref[g]).astype(BF16)
        us.append(u)
    for i in range(SSM_GROUPS // 2):
        hp = _dot(jnp.concatenate([us[2 * i], us[2 * i + 1]], axis=1), w_ref[i])
        for c in range(4):
            h_ref[c, :, i * LANES:(i + 1) * LANES] = hp[:, c * LANES:(c + 1) * LANES]


def _s5_in(ux, sel, sp):
    t16 = ux.shape[0]
    rb = min(S5_ROWS, t16)
    full2 = lambda i: (0, 0)
    full3 = lambda i: (0, 0, 0)
    return pl.pallas_call(
        _s5_in_body,
        grid=(t16 // rb,),
        in_specs=[pl.BlockSpec((rb, S5_CHUNK * 256), lambda i: (i, 0)),
                  pl.BlockSpec(sel.shape, full2),
                  pl.BlockSpec((SSM_GROUPS, 256, 256), full3),
                  pl.BlockSpec((SSM_GROUPS // 2, 512, 512), full3)],
        out_specs=[pl.BlockSpec((SSM_GROUPS, rb, 256), lambda i: (0, i, 0)),
                   pl.BlockSpec((4, rb, 1024), lambda i: (0, i, 0))],
        out_shape=[jax.ShapeDtypeStruct((SSM_GROUPS, t16, 256), BF16),
                   jax.ShapeDtypeStruct((4, t16, 1024), F32)],
        compiler_params=_cparams(("parallel",)),
        name="s5_in",
    )(ux, sel, sp['t'], sp['w'])


def _s5_scan_body(n1, h_ref, lam_ref, e_ref):
    lam = lam_ref[...]
    zero = jnp.zeros((1, 1024), F32)

    def sweep(base, reverse):
        lr, li = lam[base:base + 1], lam[base + 1:base + 2]

        def tile(k, carry):
            er, ei = carry
            r0 = pl.multiple_of(((n1 // 8 - 1 - k) if reverse else k) * 8, 8)
            hr = h_ref[base, pl.ds(r0, 8), :]
            hi = h_ref[base + 1, pl.ds(r0, 8), :]
            outs_r, outs_i = [None] * 8, [None] * 8
            for j in (range(7, -1, -1) if reverse else range(8)):
                outs_r[j], outs_i[j] = er, ei
                er, ei = (lr * er - li * ei + hr[j:j + 1], lr * ei + li * er + hi[j:j + 1])
            e_ref[base, pl.ds(r0, 8), :] = jnp.concatenate(outs_r, axis=0)
            e_ref[base + 1, pl.ds(r0, 8), :] = jnp.concatenate(outs_i, axis=0)
            return er, ei

        lax.fori_loop(0, n1 // 8, tile, (zero, zero))

    sweep(0, False)
    sweep(2, True)


def _s5_scan(h, lam, b, n1):
    return pl.pallas_call(
        functools.partial(_s5_scan_body, n1),
        grid=(b,),
        in_specs=[pl.BlockSpec((4, n1, 1024), lambda i: (0, i, 0)),
                  pl.BlockSpec((8, 1024), lambda i: (0, 0))],
        out_specs=pl.BlockSpec((4, n1, 1024), lambda i: (0, i, 0)),
        out_shape=jax.ShapeDtypeStruct(h.shape, F32),
        compiler_params=_cparams(("parallel",)),
        name="s5_scan",
    )(h, lam)


def _s5_fin_body(y_ref, e_ref, sel_ref, v_ref, w_ref, b_ref, o_ref):
    ys = []
    for i in range(SSM_GROUPS // 2):
        ep = jnp.concatenate([e_ref[c, :, i * LANES:(i + 1) * LANES] for c in range(4)], axis=1)
        yi = _dot(ep.astype(BF16), v_ref[i])
        ys.append((y_ref[2 * i].astype(F32) + yi[:, :256]).astype(BF16))
        ys.append((y_ref[2 * i + 1].astype(F32) + yi[:, 256:]).astype(BF16))
    ycat = jnp.concatenate(ys, axis=1)
    for t in range(S5_CHUNK):
        start = S5_SEL_PAD - SSM_GROUP_CH * t
        y = jax.nn.gelu(_dot(ycat, sel_ref[start:start + 4096, :]))
        z = _dot(y.astype(BF16), w_ref[...]) + b_ref[...]
        o_ref[:, t * 256:(t + 1) * 256] = (y * jax.nn.sigmoid(z)).astype(BF16)


def _s5_fin(y, e, sel, sp, w, bias):
    t16 = y.shape[1]
    rb = min(S5_ROWS, t16)
    full2 = lambda i: (0, 0)
    full3 = lambda i: (0, 0, 0)
    return pl.pallas_call(
        _s5_fin_body,
        grid=(t16 // rb,),
        in_specs=[pl.BlockSpec((SSM_GROUPS, rb, 256), lambda i: (0, i, 0)),
                  pl.BlockSpec((4, rb, 1024), lambda i: (0, i, 0)),
                  pl.BlockSpec(sel.shape, full2),
                  pl.BlockSpec((SSM_GROUPS // 2, 512, 512), full3),
                  pl.BlockSpec((256, 256), full2),
                  pl.BlockSpec((1, 256), full2)],
        out_specs=pl.BlockSpec((rb, S5_CHUNK * 256), lambda i: (i, 0)),
        out_shape=jax.ShapeDtypeStruct((t16, S5_CHUNK * 256), BF16),
        compiler_params=_cparams(("parallel",)),
        name="s5_fin",
    )(y, e, sel, sp['v'], w, bias)


def _s5(ux, q, sel, b, l):
    y, h = _s5_in(ux, sel, q['s5'])
    e = _s5_scan(h, q['s5']['lam'], b, l // S5_CHUNK)
    return _s5_fin(y, e, sel, q['s5'], q['ssm_glu_w'], q['ssm_glu_b'])


def _s5_prep(p):
    hp = lax.Precision.HIGHEST
    c = S5_CHUNK
    tau = jnp.arange(c + 1, dtype=F32)[:, None, None]
    ks, ws, vs, lams = [], [], [], []
    for d in (0, 1):
        lam_re = p['ssm_a_re'][d]
        lam_im = p['ssm_a_im'][d]
        dt = jnp.exp(p['ssm_log_dt'][d])[:, None]
        mag = jnp.exp(lam_re * dt)
        abar_re = mag * jnp.cos(lam_im * dt)
        abar_im = mag * jnp.sin(lam_im * dt)
        den = lam_re * lam_re + lam_im * lam_im
        coef_re = ((abar_re - 1.0) * lam_re + abar_im * lam_im) / den
        coef_im = (abar_im * lam_re - (abar_re - 1.0) * lam_im) / den
        br, bi = p['ssm_b_re'][d], p['ssm_b_im'][d]
        bbar_re = coef_re[..., None] * br - coef_im[..., None] * bi
        bbar_im = coef_re[..., None] * bi + coef_im[..., None] * br
        cr, ci = p['ssm_c_re'][d], p['ssm_c_im'][d]
        pm = jnp.exp(tau * (lam_re * dt)[None])
        pr = pm * jnp.cos(tau * (lam_im * dt)[None])
        pi = pm * jnp.sin(tau * (lam_im * dt)[None])
        m_re = cr[None] * pr[:, :, None, :] - ci[None] * pi[:, :, None, :]
        m_im = cr[None] * pi[:, :, None, :] + ci[None] * pr[:, :, None, :]
        k = (jnp.einsum('tghp,gpk->gtkh', m_re[:c], bbar_re, precision=hp)
             - jnp.einsum('tghp,gpk->gtkh', m_im[:c], bbar_im, precision=hp))
        ks.append(k)
        pw = jnp.arange(c - 1, -1, -1) if d == 0 else jnp.arange(c)
        w_re = pr[pw][:, :, None, :] * jnp.swapaxes(bbar_re, 1, 2)[None] - pi[pw][:, :, None, :] * jnp.swapaxes(bbar_im, 1, 2)[None]
        w_im = pr[pw][:, :, None, :] * jnp.swapaxes(bbar_im, 1, 2)[None] + pi[pw][:, :, None, :] * jnp.swapaxes(bbar_re, 1, 2)[None]
        ws.append((jnp.transpose(w_re, (1, 0, 2, 3)).reshape(SSM_GROUPS, 256, SSM_STATE),
                   jnp.transpose(w_im, (1, 0, 2, 3)).reshape(SSM_GROUPS, 256, SSM_STATE)))
        po = jnp.arange(1, c + 1) if d == 0 else jnp.arange(c, 0, -1)
        v_re = m_re[po]
        v_im = m_im[po]
        vs.append((jnp.transpose(v_re, (1, 3, 0, 2)).reshape(SSM_GROUPS, SSM_STATE, 256),
                   jnp.transpose(-v_im, (1, 3, 0, 2)).reshape(SSM_GROUPS, SSM_STATE, 256)))
        lams.append((pr[c], pi[c]))
    j = jnp.arange(c)[:, None]
    t = jnp.arange(c)[None, :]
    k0 = ks[0][:, jnp.clip(t - j, 0, c - 1)] * (t >= j)[None, :, :, None, None]
    k1 = ks[1][:, jnp.clip(j - t, 0, c - 1)] * (j >= t)[None, :, :, None, None]
    tm = jnp.transpose(k0 + k1, (0, 1, 3, 2, 4)).reshape(SSM_GROUPS, 256, 256)
    dd = jnp.tile(p['ssm_d'], (1, c))
    tm = tm + jnp.eye(256, dtype=F32)[None] * dd[:, None, :]
    npair = SSM_GROUPS // 2

    def pair_diag(x):
        g, r, cc = x.shape
        x = x.reshape(npair, 2, r, cc)
        z = jnp.zeros((npair, 2, r, 2, cc), F32)
        z = z.at[:, 0, :, 0, :].set(x[:, 0]).at[:, 1, :, 1, :].set(x[:, 1])
        return z.reshape(npair, 2 * r, 2 * cc)

    w = jnp.concatenate([pair_diag(ws[0][0]), pair_diag(ws[0][1]), pair_diag(ws[1][0]), pair_diag(ws[1][1])], axis=2)
    v = jnp.concatenate([pair_diag(vs[0][0]), pair_diag(vs[0][1]), pair_diag(vs[1][0]), pair_diag(vs[1][1])], axis=1)
    lam = jnp.stack([lams[0][0], lams[0][1], lams[1][0], lams[1][1]], axis=0)
    lam = lam.reshape(4, SSM_GROUPS * SSM_STATE)
    lam = jnp.concatenate([lam, jnp.zeros_like(lam)], axis=0)
    return {'t': tm.astype(BF16), 'w': w.astype(BF16), 'v': v.astype(BF16), 'lam': lam}


GDN_HALO = 16


def _gdn_prep_body(nt, q_ref, qp_ref, qn_ref, k_ref, kp_ref, kn_ref, v_ref, vp_ref, vn_ref,
                   gate_ref, cw_ref, al_ref, dtb_ref, bd_ref, qo_ref, ko_ref, vo_ref, gb_ref):
    i = pl.program_id(1)
    cw = cw_ref[...]
    tl = q_ref.shape[0]

    def conv(cur_ref, prev_ref, next_ref, col):
        prev = jnp.where(i > 0, prev_ref[...].astype(F32), 0.0)
        nxt = jnp.where(i < nt - 1, next_ref[...].astype(F32), 0.0)
        xe = jnp.concatenate([prev, cur_ref[...].astype(F32), nxt], axis=0)
        n = xe.shape[0]
        acc = jnp.zeros((tl, 256), F32)
        for tap in range(5):
            s = tap - 2
            sh = xe if s == 0 else pltpu.roll(xe, (-s) % n, 0)
            acc = acc + sh[GDN_HALO:GDN_HALO + tl] * cw[tap:tap + 1, col:col + 256]
        return acc * jax.nn.sigmoid(acc)

    bd = bd_ref[...]

    def l2n(x):
        return x * lax.rsqrt(_head_mean_sq(x, bd) * HEAD_DIM + NORM_EPS)

    qo_ref[...] = l2n(conv(q_ref, qp_ref, qn_ref, 0)).astype(BF16)
    ko_ref[...] = l2n(conv(k_ref, kp_ref, kn_ref, 256)).astype(BF16)
    vo_ref[...] = conv(v_ref, vp_ref, vn_ref, 512).astype(BF16)
    gt = gate_ref[...]
    lane = lax.broadcasted_iota(jnp.int32, gt.shape, 1)
    x = gt + dtb_ref[...]
    softplus = jnp.maximum(x, 0.0) + jnp.log(1.0 + jnp.exp(-jnp.abs(x)))
    gb_ref[...] = jnp.where(lane < 8, -al_ref[...] * softplus, jnp.where(lane < 16, jax.nn.sigmoid(gt), 0.0))


def _gdn_prep(main, gate, q, bd, b, l):
    tl = min(256, l)
    nt = l // tl
    hb = tl // GDN_HALO
    nh = l // GDN_HALO

    def cur(col):
        return pl.BlockSpec((tl, 256), lambda bi, i: (bi * nt + i, col // 256))

    def prev(col):
        return pl.BlockSpec((GDN_HALO, 256), lambda bi, i: (bi * nh + jnp.maximum(i * hb - 1, 0), col // 256))

    def nxt(col):
        return pl.BlockSpec((GDN_HALO, 256), lambda bi, i: (bi * nh + jnp.minimum((i + 1) * hb, nh - 1), col // 256))

    full = lambda bi, i: (0, 0)
    row = lambda bi, i: (bi * nt + i, 0)
    specs = []
    for col in (COL_CQ, COL_CK, COL_CV):
        specs += [cur(col), prev(col), nxt(col)]
    specs += [pl.BlockSpec((tl, GATE_W), row), pl.BlockSpec((8, 768), full), pl.BlockSpec((1, LANES), full),
              pl.BlockSpec((1, LANES), full), pl.BlockSpec((256, 256), full)]
    return pl.pallas_call(
        functools.partial(_gdn_prep_body, nt),
        grid=(b, nt),
        in_specs=specs,
        out_specs=[pl.BlockSpec((tl, 256), row)] * 3 + [pl.BlockSpec((tl, GATE_W), row)],
        out_shape=[jax.ShapeDtypeStruct((b * l, 256), BF16)] * 3 + [jax.ShapeDtypeStruct((b * l, GATE_W), F32)],
        compiler_params=_cparams(("parallel", "parallel")),
        name="gdn_prep",
    )(*([main] * 9), gate, q['gdn_conv_w'], q['gdn_al'], q['gdn_dtb'], bd)


def _gdn_chunk_step(chains):
    c = GDN_CHUNK
    r256 = lax.broadcasted_iota(jnp.int32, (256, 256), 0)
    c256 = lax.broadcasted_iota(jnp.int32, (256, 256), 1)
    blockmask = (r256 // c) == (c256 // c)
    rl = lax.broadcasted_iota(jnp.int32, (LANES, 256), 0)
    cl = lax.broadcasted_iota(jnp.int32, (LANES, 256), 1)
    ri = lax.broadcasted_iota(jnp.int32, (c, c), 0)
    ci = lax.broadcasted_iota(jnp.int32, (c, c), 1)
    i_s = lax.broadcasted_iota(jnp.int32, (c, 256), 0)
    j_s = lax.broadcasted_iota(jnp.int32, (c, 256), 1) % c
    ones = jnp.ones((c, c), BF16)
    eye = (i_s == j_s).astype(F32)
    scale = HEAD_DIM ** -0.5

    def bdv(y):
        return jnp.where(blockmask, jnp.concatenate([y, y, y, y], axis=0), 0.0).astype(BF16)

    def hilo(x):
        hi = x.astype(BF16)
        return hi, (x - hi.astype(F32)).astype(BF16)

    st = []
    for dirn, qb, kb16, vb, gb, s_ref in chains:
        rev = dirn == 1
        e_g = (rl == dirn * 4 + cl // c).astype(BF16)
        e_b = (rl == 8 + dirn * 4 + cl // c).astype(BF16)
        ghi, glo = hilo(gb)
        st.append(dict(rev=rev, q=qb.astype(F32), k=kb16.astype(F32), v=vb.astype(F32), s_ref=s_ref,
                       g=_dot(ghi, e_g) + _dot(glo, e_g), beta=_dot(ghi, e_b) + _dot(glo, e_b),
                       tri=((ci >= ri) if rev else (ci <= ri)).astype(BF16),
                       allowed=(j_s >= i_s) if rev else (j_s <= i_s),
                       strict=(j_s > i_s) if rev else (j_s < i_s)))
    for d in st:
        ghi, glo = hilo(d['g'])
        d['gc'] = _dot(d['tri'], ghi) + _dot(d['tri'], glo)
    for d in st:
        zhi, zlo = hilo(jnp.where(i_s == j_s, d['gc'], 0.0))
        d['gct'] = _dot(ones, zhi) + _dot(ones, zlo)
    for d in st:
        d['decay'] = jnp.exp(jnp.where(d['allowed'], d['gc'] - d['gct'], -jnp.inf))
        d['eg'] = jnp.exp(d['gc'])
        d['kbeta'] = d['k'] * d['beta']
    for d in st:
        kk_qk = _dot_nt(jnp.concatenate([d['kbeta'], d['q'] * scale], axis=0).astype(BF16), bdv(d['k']))
        d['p'] = -jnp.where(d['strict'], kk_qk[:c] * d['decay'], 0.0)
        d['intra'] = jnp.where(d['allowed'], kk_qk[c:] * d['decay'], 0.0)
        d['t'] = eye + d['p']
    for _ in range(5):
        for d in st:
            d['p'] = _dot(d['p'].astype(BF16), bdv(d['p']))
        for d in st:
            d['t'] = d['t'] + _dot(d['t'].astype(BF16), bdv(d['p']))
    for d in st:
        t16 = d['t'].astype(BF16)
        d['u'] = _dot(t16, bdv(d['v'] * d['beta']))
        d['w'] = _dot(t16, bdv(d['kbeta'] * d['eg']))
    for d in st:
        d['s'] = d['s_ref'][...]
        d['ws_qs'] = _dot(jnp.concatenate([d['w'], d['q'] * scale * d['eg']], axis=0).astype(BF16),
                          d['s'].astype(BF16))
    for d in st:
        d['v_new'] = d['u'] - d['ws_qs'][:c]
        d['o'] = d['ws_qs'][c:] + _dot(d['intra'].astype(BF16), bdv(d['v_new']))
    for d in st:
        last = 0 if d['rev'] else c - 1
        g_last = d['gc'][last:last + 1, :]
        kg = (d['k'] * jnp.exp(g_last - d['gc'])).astype(BF16)
        upd = lax.dot_general(kg, d['v_new'].astype(BF16), (((0,), (0,)), ((), ())), preferred_element_type=F32)
        d['s_ref'][...] = d['s'] * jnp.exp(g_last) + jnp.where(blockmask, upd, 0.0)
    return [d['o'] for d in st]


GDN_SEQS = 2


def _gdn_chunk_body(qf_ref, kf_ref, vf_ref, gf_ref, qb_ref, kb_ref, vb_ref, gb_ref, of_ref, ob_ref, s_ref):
    @pl.when(pl.program_id(1) == 0)
    def _():
        s_ref[...] = jnp.zeros(s_ref.shape, F32)

    chains = []
    for j in range(qf_ref.shape[0]):
        chains.append((0, qf_ref[j], kf_ref[j], vf_ref[j], gf_ref[j], s_ref.at[j, 0]))
        chains.append((1, qb_ref[j], kb_ref[j], vb_ref[j], gb_ref[j], s_ref.at[j, 1]))
    outs = _gdn_chunk_step(chains)
    for j in range(qf_ref.shape[0]):
        of_ref[j] = outs[2 * j].astype(BF16)
        ob_ref[j] = outs[2 * j + 1].astype(BF16)


def _gdn_chunks(qn, kn, vs, gb, b, l):
    c = GDN_CHUNK
    n = l // c
    nseq = GDN_SEQS if b % GDN_SEQS == 0 else 1
    fwd = lambda bi, i: (bi, i, 0)
    bwd = lambda bi, i: (bi, n - 1 - i, 0)
    blk = lambda m: pl.BlockSpec((nseq, c, 256), m)
    gblk = lambda m: pl.BlockSpec((nseq, c, GATE_W), m)
    qn, kn, vs = (a.reshape(b, l, 256) for a in (qn, kn, vs))
    gb = gb.reshape(b, l, GATE_W)
    of, ob = pl.pallas_call(
        _gdn_chunk_body,
        grid=(b // nseq, n),
        in_specs=[blk(fwd), blk(fwd), blk(fwd), gblk(fwd), blk(bwd), blk(bwd), blk(bwd), gblk(bwd)],
        out_specs=[blk(fwd), blk(bwd)],
        out_shape=[jax.ShapeDtypeStruct((b, l, 256), BF16)] * 2,
        scratch_shapes=[pltpu.VMEM((nseq, 2, 256, 256), F32)],
        compiler_params=_cparams(("parallel", "arbitrary")),
        name="gdn_chunks",
    )(qn, kn, vs, gb, qn, kn, vs, gb)
    return of.reshape(b * l, 256), ob.reshape(b * l, 256)


def _gdn_out_body(of_ref, ob_ref, z_ref, g_ref, bd_ref, o_ref):
    o = of_ref[...].astype(F32) + ob_ref[...].astype(F32)
    z = z_ref[...].astype(F32)
    ms = _head_mean_sq(o, bd_ref[...])
    o_ref[...] = (o * lax.rsqrt(ms + NORM_EPS) * g_ref[...] * (z * jax.nn.sigmoid(z))).astype(BF16)


def _gdn_out(of, ob, main, g, bd):
    t = of.shape[0]
    tm = min(1024, t)
    row = lambda i: (i, 0)
    full = lambda i: (0, 0)
    return pl.pallas_call(
        _gdn_out_body,
        grid=(t // tm,),
        in_specs=[pl.BlockSpec((tm, 256), row), pl.BlockSpec((tm, 256), row),
                  pl.BlockSpec((tm, 256), lambda i: (i, COL_CZ // 256)),
                  pl.BlockSpec((1, 256), full), pl.BlockSpec((256, 256), full)],
        out_specs=pl.BlockSpec((tm, 256), row),
        out_shape=jax.ShapeDtypeStruct((t, 256), BF16),
        compiler_params=_cparams(("parallel",)),
        name="gdn_out",
    )(of, ob, main, g, bd)


def _gdn(main, gate, q, bd, b, l):
    qn, kn, vs, gb = _gdn_prep(main, gate, q, bd, b, l)
    of, ob = _gdn_chunks(qn, kn, vs, gb, b, l)
    return _gdn_out(of, ob, main, q['gdn_o_norm'], bd)


def _moe(x1, x1b, top_idx, gates, p):
    t = x1.shape[0]
    m = t * TOP_K
    flat_e = top_idx.reshape(-1)
    order = jnp.argsort(flat_e, stable=True).astype(jnp.int32)
    inv = jnp.argsort(order).astype(jnp.int32)
    sorted_e = flat_e[order]
    experts = jnp.arange(N_EXPERTS, dtype=jnp.int32)
    counts = jnp.sum((top_idx[None] == experts[:, None, None]).astype(jnp.int32), axis=(1, 2))
    padded = (counts + MOE_BLOCK - 1) // MOE_BLOCK * MOE_BLOCK
    group_start = jnp.cumsum(counts) - counts
    padded_end = jnp.cumsum(padded)
    padded_start = padded_end - padded
    dest = (padded_start[sorted_e] + jnp.arange(m, dtype=jnp.int32) - group_start[sorted_e]).astype(jnp.int32)
    n_blocks = -(-m // MOE_BLOCK) + N_EXPERTS
    n_slots = n_blocks * MOE_BLOCK
    block_start = jnp.arange(n_blocks, dtype=jnp.int32) * MOE_BLOCK
    block_expert = jnp.minimum(jnp.sum((padded_end[None, :] <= block_start[:, None]).astype(jnp.int32), axis=1),
                               N_EXPERTS - 1).astype(jnp.int32)
    n_used = (padded_end[-1] // MOE_BLOCK).astype(jnp.int32).reshape(1)
    slot = jnp.arange(n_slots, dtype=jnp.int32)
    slot_e = jnp.repeat(block_expert, MOE_BLOCK)
    rank = slot - padded_start[slot_e]
    src = jnp.clip(group_start[slot_e] + rank, 0, m - 1)
    slot_token = jnp.where(rank < counts[slot_e], order[src] % t, 0).astype(jnp.int32)
    n_ranges = MOE_RANGES if n_blocks % MOE_RANGES == 0 else 1
    rblocks = n_blocks // n_ranges
    yb = None
    for r in range(n_ranges):
        b0 = r * rblocks
        xb = x1b[slot_token[b0 * MOE_BLOCK:(b0 + rblocks) * MOE_BLOCK]]
        yb = _ffn(xb, block_expert[b0:b0 + rblocks], jnp.clip(n_used - b0, 0, rblocks),
                  p['moe_wg'], p['moe_wl'], p['moe_bg'], p['moe_bl'], p['moe_w2'], p['moe_b2'],
                  yb, n_slots, b0)
    dest_a = dest[inv]
    out = jnp.zeros((t, D_MODEL), F32)
    for k in range(TOP_K):
        out = out + yb[dest_a[k * t:(k + 1) * t]].astype(F32) * gates[:, k:k + 1]
    return out


def _block_diag_mean(width, group):
    idx = np.arange(width)
    return jnp.asarray((idx[:, None] // group == idx[None, :] // group) / group, BF16)


def _axial_tables(l):
    rows = l // GRID_W
    row_pos = np.repeat(np.arange(rows), GRID_W).astype(np.float64)
    col_pos = np.tile(np.arange(GRID_W), rows).astype(np.float64)
    lane = np.arange(LANES)
    d = lane % HEAD_DIM
    e = d % 32
    f = e % 16
    inv = (AXIAL_THETA ** (-(np.arange(0, 32, 2, dtype=np.float32)) / 32)).astype(np.float32)
    pos = np.where((d // 32)[None, :] == 0, row_pos[:, None], col_pos[:, None]).astype(np.float32)
    ang = pos * inv[f][None, :]
    sign = np.where(e < 16, -1.0, 1.0)[None, :]
    return jnp.asarray(np.cos(ang), F32), jnp.asarray(np.sin(ang) * sign, F32)


def _diff_tables(l):
    lane = np.arange(256)
    e = lane % 32
    f = e % 4
    inv = (ROPE_THETA ** (-(np.arange(0, PARTIAL_ROPE_DIMS, 2, dtype=np.float32)) / PARTIAL_ROPE_DIMS)).astype(np.float32)
    ang = np.arange(l, dtype=np.float32)[:, None] * inv[f][None, :]
    roped = (e < PARTIAL_ROPE_DIMS)[None, :]
    sign = np.where(e < 4, -1.0, 1.0)[None, :]
    c = np.where(roped, np.cos(ang), 1.0)
    s = np.where(roped, np.sin(ang) * sign, 0.0)
    return jnp.asarray(c, F32), jnp.asarray(s, F32)


def _prep_layer(params, layer, w1_all):
    p = {name: arr[layer] for name, arr in params.items()}
    w_in = p['w_in']
    w_in = jnp.concatenate([w_in[:, :ORIG_CA], w_in[:, ORIG_CA + 16:], w_in[:, ORIG_CA:ORIG_CA + 16],
                            jnp.zeros((D_MODEL, GATE_W - 16), F32)], axis=1)
    q = dict(p)
    q['w_in'] = w_in.astype(BF16)
    q['w_out'] = p['w_out'].astype(BF16)
    q['router_w'] = jnp.concatenate([p['router_w'], jnp.zeros((D_MODEL, LANES - N_EXPERTS), F32)], axis=1).astype(BF16)
    q['router_b'] = jnp.concatenate([p['router_b'], jnp.full((LANES - N_EXPERTS,), -1e30, F32)])[None, :]
    q['moe_wg'] = w1_all[0][layer]
    q['moe_wl'] = w1_all[1][layer]
    q['moe_bg'] = p['moe_b1'][:, None, 0::2]
    q['moe_bl'] = p['moe_b1'][:, None, 1::2]
    q['moe_w2'] = p['moe_w2'].astype(BF16)
    q['moe_b2'] = p['moe_b2'][:, None, :]
    for name in ('ln1_g', 'ln1_b', 'ln2_g', 'ln2_b'):
        q[name] = p[name][None, :]
    q['gqa_q_norm'] = jnp.tile(p['gqa_q_norm'], 4)[None, :]
    q['gqa_k_norm'] = jnp.tile(p['gqa_k_norm'], 2)[None, :]
    lambda_init = 0.8 - 0.6 * math.exp(-0.3 * layer)
    q['diff_subln'] = (jnp.tile(p['diff_subln'], 4) * (1.0 - lambda_init))[None, :]
    lam = (jnp.exp(jnp.sum(p['diff_lambda_q1'] * p['diff_lambda_k1']))
           - jnp.exp(jnp.sum(p['diff_lambda_q2'] * p['diff_lambda_k2'])) + lambda_init)
    q['diff_lam'] = lam.reshape(1).astype(F32)
    q['s5'] = _s5_prep(p)
    q['ssm_glu_w'] = p['ssm_glu_w'].astype(BF16)
    q['ssm_glu_b'] = p['ssm_glu_b'][None, :]
    q['gdn_conv_w'] = jnp.concatenate([p['gdn_conv_w'], jnp.zeros((3, 768), F32)], axis=0)
    pad8 = lambda x: jnp.concatenate([x.reshape(-1), jnp.zeros((LANES - 8,), F32)])[None, :]
    q['gdn_al'] = pad8(jnp.exp(p['gdn_a_log']))
    q['gdn_dtb'] = pad8(p['gdn_dt_bias'])
    q['gdn_o_norm'] = jnp.tile(p['gdn_o_norm'], 4)[None, :]
    return q


def _mixers(main, gate, ux, q, tabs, b, l):
    a_out = _s5(ux, q, tabs['s5sel'], b, l)
    b_out = _gqa(main, tabs['axial'], q['gqa_q_norm'], q['gqa_k_norm'], tabs['bd64'], b, l)
    c_out = _gdn(main, gate, q, tabs['bd64'], b, l)
    d_out = _diff(main, q['diff_lam'], tabs['diff'], q['diff_subln'], tabs['bd64'], b, l)
    return a_out, b_out, c_out, d_out


def _layer(x, q, tabs, b, l, alpha):
    main, gate, ux = _inproj(x, q['w_in'])
    pieces = _mixers(main, gate, ux, q, tabs, b, l)
    x1, x1b, idx, gates = _outproj(pieces, q['w_out'], x, q['ln1_g'], q['ln1_b'],
                                   q['router_w'], q['router_b'], alpha)
    ffn = _moe(x1, x1b, idx[:TOP_K], gates, q)
    return _ln2(x1, ffn, q['ln2_g'], q['ln2_b'], alpha)


def _trunk(x, layers, alpha):
    b, l, _ = x.shape
    tabs = {'axial': _axial_tables(l), 'diff': _diff_tables(l), 'bd64': _block_diag_mean(256, HEAD_DIM),
            's5sel': _s5_sel()}
    h = x.reshape(b * l, D_MODEL)
    for q in layers:
        h = _layer(h, q, tabs, b, l, alpha)
    return h.reshape(b, l, D_MODEL)


def kernel(x_prompt, x_sample, w_in, w_out, ssm_a_re, ssm_a_im, ssm_log_dt, ssm_b_re, ssm_b_im, ssm_c_re, ssm_c_im, ssm_d, ssm_glu_w, ssm_glu_b, gqa_q_norm, gqa_k_norm, gdn_conv_w, gdn_a_log, gdn_dt_bias, gdn_o_norm, diff_lambda_q1, diff_lambda_k1, diff_lambda_q2, diff_lambda_k2, diff_subln, router_w, router_b, moe_w1, moe_b1, moe_w2, moe_b2, ln1_g, ln1_b, ln2_g, ln2_b):
    params = {
        'w_in': w_in, 'w_out': w_out,
        'ssm_a_re': ssm_a_re, 'ssm_a_im': ssm_a_im, 'ssm_log_dt': ssm_log_dt,
        'ssm_b_re': ssm_b_re, 'ssm_b_im': ssm_b_im, 'ssm_c_re': ssm_c_re, 'ssm_c_im': ssm_c_im,
        'ssm_d': ssm_d, 'ssm_glu_w': ssm_glu_w, 'ssm_glu_b': ssm_glu_b,
        'gqa_q_norm': gqa_q_norm, 'gqa_k_norm': gqa_k_norm,
        'gdn_conv_w': gdn_conv_w, 'gdn_a_log': gdn_a_log, 'gdn_dt_bias': gdn_dt_bias, 'gdn_o_norm': gdn_o_norm,
        'diff_lambda_q1': diff_lambda_q1, 'diff_lambda_k1': diff_lambda_k1,
        'diff_lambda_q2': diff_lambda_q2, 'diff_lambda_k2': diff_lambda_k2, 'diff_subln': diff_subln,
        'router_w': router_w, 'router_b': router_b,
        'moe_w1': moe_w1, 'moe_b1': moe_b1, 'moe_w2': moe_w2, 'moe_b2': moe_b2,
        'ln1_g': ln1_g, 'ln1_b': ln1_b, 'ln2_g': ln2_g, 'ln2_b': ln2_b,
    }
    depth = w_in.shape[0]
    alpha = (2.0 * depth) ** 0.25
    w1_all = [w.reshape(moe_w1.shape[:3] + (-1,)) for w in _w1_prep(moe_w1.reshape((-1,) + moe_w1.shape[2:]))]
    layers = [_prep_layer(params, layer, w1_all) for layer in range(depth)]
    return (_trunk(x_prompt, layers, alpha), _trunk(x_sample, layers, alpha))
```

```python
import functools
import math

import jax
import jax.numpy as jnp
import numpy as np
from jax import lax
from jax.experimental import pallas as pl
from jax.experimental.pallas import tpu as pltpu

F32 = jnp.float32
BF16 = jnp.bfloat16

D_MODEL = 1024
GROUP_WIDTH = 256
HEAD_DIM = 64
SSM_GROUPS = 16
SSM_GROUP_CH = 16
SSM_STATE = 64
GDN_HEADS = 4
GDN_CHUNK = 64
DIFF_SUB_DIM = 32
PARTIAL_ROPE_DIMS = 8
AXIAL_THETA = 10000.0
ROPE_THETA = 500000.0
GRID_W = 64
N_EXPERTS = 32
TOP_K = 4
SWIGLU_ALPHA = 1.702
SWIGLU_LIMIT = 7.0
NORM_EPS = 1e-6
LN_EPS = 1e-5
LOG2E = math.log2(math.e)

LANES = 128
VMEM_LIMIT = 56 * 1024 * 1024

COL_U, COL_GQ, COL_GK, COL_GV = 0, 256, 512, 640
COL_CQ, COL_CK, COL_CV, COL_CZ = 768, 1024, 1280, 1536
COL_DQ, COL_DK, COL_DV = 1792, 2048, 2304
MAIN_W = 2560
GATE_W = 128
ORIG_CA = 1792


def _cparams(sem):
    return pltpu.CompilerParams(dimension_semantics=sem, vmem_limit_bytes=VMEM_LIMIT)


def _dot(a, b):
    return jnp.dot(a, b, preferred_element_type=F32)


def _dot_nt(a, b):
    return lax.dot_general(a, b, (((1,), (1,)), ((), ())), preferred_element_type=F32)


def _split_dot(x, m):
    hi = x.astype(BF16)
    lo = (x - hi.astype(F32)).astype(BF16)
    return _dot(hi, m) + _dot(lo, m)


def _chunk_rows_sel(n_chunks, t):
    r = lax.broadcasted_iota(jnp.int32, (n_chunks, S5_CHUNK * n_chunks), 0)
    c = lax.broadcasted_iota(jnp.int32, (n_chunks, S5_CHUNK * n_chunks), 1)
    return (c == S5_CHUNK * r + t).astype(BF16)


def _inproj_body(x_ref, w_ref, main_ref, gate_ref, ux_ref):
    x = x_ref[...].astype(BF16)
    step = 640
    for c in range(MAIN_W // step):
        main_ref[:, c * step:(c + 1) * step] = _dot(x, w_ref[:, c * step:(c + 1) * step]).astype(BF16)
    gate_ref[...] = _dot(x, w_ref[:, MAIN_W:])
    u = main_ref[:, COL_U:COL_U + 256]
    for t in range(S5_CHUNK):
        ux_ref[:, t * 256:(t + 1) * 256] = _dot(_chunk_rows_sel(ux_ref.shape[0], t), u).astype(BF16)


def _inproj(x, w):
    t = x.shape[0]
    tm = min(512, t)
    return pl.pallas_call(
        _inproj_body,
        grid=(t // tm,),
        in_specs=[pl.BlockSpec((tm, D_MODEL), lambda i: (i, 0)),
                  pl.BlockSpec((D_MODEL, MAIN_W + GATE_W), lambda i: (0, 0))],
        out_specs=[pl.BlockSpec((tm, MAIN_W), lambda i: (i, 0)),
                   pl.BlockSpec((tm, GATE_W), lambda i: (i, 0)),
                   pl.BlockSpec((tm // S5_CHUNK, S5_CHUNK * 256), lambda i: (i, 0))],
        out_shape=[jax.ShapeDtypeStruct((t, MAIN_W), BF16),
                   jax.ShapeDtypeStruct((t, GATE_W), F32),
                   jax.ShapeDtypeStruct((t // S5_CHUNK, S5_CHUNK * 256), BF16)],
        compiler_params=_cparams(("parallel",)),
        name="inproj",
    )(x, w)


def _head_mean_sq(xf, bd):
    return _split_dot(xf * xf, bd)


def _rope_lanes(x, c, s, half):
    n = x.shape[-1]
    lane = lax.broadcasted_iota(jnp.int32, x.shape, 1)
    first = (lane % (2 * half)) < half
    swapped = jnp.where(first, pltpu.roll(x, n - half, 1), pltpu.roll(x, half, 1))
    return x * c + swapped * s


def _gqa_body(q_ref, k_ref, v_ref, cq_ref, sq_ref, ck_ref, sk_ref, gq_ref, gk_ref, bd_ref,
              o_ref, kdup_ref, vaug_ref):
    i = pl.program_id(1)
    bd = bd_ref[...]

    @pl.when(i == 0)
    def _():
        kf = k_ref[...].astype(F32)
        ms = _head_mean_sq(kf, bd[:LANES, :LANES])
        kn = kf * lax.rsqrt(ms + NORM_EPS) * gk_ref[...]
        kn = _rope_lanes(kn, ck_ref[...], sk_ref[...], 16)
        lane = lax.broadcasted_iota(jnp.int32, kn.shape, 1)
        sw = pltpu.roll(kn, 64, 1)
        kdup_ref[0] = jnp.where(lane < 64, kn, sw).astype(BF16)
        kdup_ref[1] = jnp.where(lane < 64, sw, kn).astype(BF16)
        v = v_ref[...]
        one = jnp.ones(v.shape, BF16)
        vaug_ref[0] = jnp.where(lane < 64, v, one)
        vaug_ref[1] = jnp.where(lane < 64, one, v)

    qf = q_ref[...].astype(F32)
    ms = _head_mean_sq(qf, bd)
    qn = qf * lax.rsqrt(ms + NORM_EPS) * gq_ref[...]
    cq = jnp.concatenate([cq_ref[...], cq_ref[...]], axis=1)
    sq = jnp.concatenate([sq_ref[...], sq_ref[...]], axis=1)
    qn = (_rope_lanes(qn, cq, sq, 16) * (HEAD_DIM ** -0.5 * LOG2E)).astype(BF16)
    tq = qn.shape[0]
    lane = lax.broadcasted_iota(jnp.int32, (tq, LANES), 1)
    zero = jnp.zeros((tq, LANES), BF16)
    def scores(h):
        qh = qn[:, h * LANES:(h + 1) * LANES]
        q2 = jnp.concatenate([jnp.where(lane < 64, qh, zero), jnp.where(lane < 64, zero, qh)], axis=0)
        return _dot_nt(q2, kdup_ref[h])

    s_next = scores(0)
    for h in range(2):
        s = s_next
        if h + 1 < 2:
            s_next = scores(h + 1)
        m = jnp.max(s, axis=-1, keepdims=True)
        p = jnp.exp2(s - m).astype(BF16)
        o2 = _dot(p, vaug_ref[h])
        o2 = o2 / pltpu.roll(o2, 64, 1)
        top, bot = o2[:tq], o2[tq:]
        if h == 0:
            oh = jnp.where(lane < 64, top, pltpu.roll(bot, 64, 1))
        else:
            oh = jnp.where(lane < 64, pltpu.roll(top, 64, 1), bot)
        o_ref[:, h * LANES:(h + 1) * LANES] = oh.astype(BF16)


def _gqa(main, tabs, gq, gk, bd, b, l):
    tq = min(256, l)
    nq = l // tq
    cq, sq = tabs
    full = lambda bi, i: (0, 0)
    return pl.pallas_call(
        _gqa_body,
        grid=(b, nq),
        in_specs=[pl.BlockSpec((tq, 256), lambda bi, i: (bi * nq + i, COL_GQ // 256)),
                  pl.BlockSpec((l, LANES), lambda bi, i: (bi, COL_GK // LANES)),
                  pl.BlockSpec((l, LANES), lambda bi, i: (bi, COL_GV // LANES)),
                  pl.BlockSpec((tq, LANES), lambda bi, i: (i, 0)),
                  pl.BlockSpec((tq, LANES), lambda bi, i: (i, 0)),
                  pl.BlockSpec((l, LANES), full),
                  pl.BlockSpec((l, LANES), full),
                  pl.BlockSpec((1, 256), full),
                  pl.BlockSpec((1, LANES), full),
                  pl.BlockSpec((256, 256), full)],
        out_specs=pl.BlockSpec((tq, 256), lambda bi, i: (bi * nq + i, 0)),
        out_shape=jax.ShapeDtypeStruct((b * l, 256), BF16),
        scratch_shapes=[pltpu.VMEM((2, l, LANES), BF16), pltpu.VMEM((2, l, LANES), BF16)],
        compiler_params=_cparams(("parallel", "arbitrary")),
        name="gqa",
    )(main, main, main, cq, sq, cq, sq, gq, gk, bd)


def _diff_body(lam_ref, q_ref, k_ref, v_ref, cq_ref, sq_ref, ck_ref, sk_ref, g_ref, bd_ref,
               o_ref, kr_ref, vaug_ref):
    i = pl.program_id(1)

    @pl.when(i == 0)
    def _():
        kr_ref[...] = _rope_lanes(k_ref[...].astype(F32), ck_ref[...], sk_ref[...], 4).astype(BF16)
        v = v_ref[...]
        vlane = lax.broadcasted_iota(jnp.int32, v.shape, 1)
        one = jnp.ones(v.shape, BF16)
        for h in range(4):
            vaug_ref[h] = jnp.where(vlane // 64 == h, v, one)

    lam = lam_ref[0]
    qr = _rope_lanes(q_ref[...].astype(F32), cq_ref[...], sq_ref[...], 4)
    qr = (qr * (DIFF_SUB_DIM ** -0.5 * LOG2E)).astype(BF16)
    tq = qr.shape[0]
    lane = lax.broadcasted_iota(jnp.int32, (tq, 256), 1)
    zero = jnp.zeros((tq, 256), BF16)
    kr = kr_ref[...]
    acc = jnp.zeros((tq, 256), F32)

    def scores(h):
        q2 = jnp.concatenate([jnp.where(lane // 32 == 2 * h, qr, zero),
                              jnp.where(lane // 32 == 2 * h + 1, qr, zero)], axis=0)
        return _dot_nt(q2, kr)

    s_next = scores(0)
    for h in range(4):
        s = s_next
        if h + 1 < 4:
            s_next = scores(h + 1)
        m = jnp.max(s, axis=-1, keepdims=True)
        e = jnp.exp2(s - m).astype(BF16)
        o2 = _dot(e, vaug_ref[h])
        o2 = o2 / pltpu.roll(o2, 64, 1)
        o = o2[:tq] - lam * o2[tq:]
        acc = jnp.where(lane // 64 == h, o, acc)
    ms = _head_mean_sq(acc, bd_ref[...])
    o_ref[...] = (acc * lax.rsqrt(ms + NORM_EPS) * g_ref[...]).astype(BF16)


def _diff(main, lam, tabs, g, bd, b, l):
    tq = min(256, l)
    nq = l // tq
    c, s = tabs
    full = lambda bi, i, *_: (0, 0)
    once = pl.Buffered(1)
    grid_spec = pltpu.PrefetchScalarGridSpec(
        num_scalar_prefetch=1,
        grid=(b, nq),
        in_specs=[pl.BlockSpec((tq, 256), lambda bi, i, *_: (bi * nq + i, COL_DQ // 256)),
                  pl.BlockSpec((l, 256), lambda bi, i, *_: (bi, COL_DK // 256)),
                  pl.BlockSpec((l, 256), lambda bi, i, *_: (bi, COL_DV // 256)),
                  pl.BlockSpec((tq, 256), lambda bi, i, *_: (i, 0)),
                  pl.BlockSpec((tq, 256), lambda bi, i, *_: (i, 0)),
                  pl.BlockSpec((l, 256), full, pipeline_mode=once),
                  pl.BlockSpec((l, 256), full, pipeline_mode=once),
                  pl.BlockSpec((1, 256), full),
                  pl.BlockSpec((256, 256), full)],
        out_specs=pl.BlockSpec((tq, 256), lambda bi, i, *_: (bi * nq + i, 0)),
        scratch_shapes=[pltpu.VMEM((l, 256), BF16), pltpu.VMEM((4, l, 256), BF16)],
    )
    return pl.pallas_call(
        _diff_body,
        grid_spec=grid_spec,
        out_shape=jax.ShapeDtypeStruct((b * l, 256), BF16),
        compiler_params=_cparams(("parallel", "arbitrary")),
        name="diffattn",
    )(lam, main, main, main, c, s, c, s, g, bd)


def _layer_norm_rows(y, g, b):
    mu = jnp.mean(y, axis=-1, keepdims=True)
    d = y - mu
    var = jnp.mean(d * d, axis=-1, keepdims=True)
    return d * lax.rsqrt(var + LN_EPS) * g + b


def _outproj_body(alpha, a_ref, b_ref, c_ref, d_ref, w_ref, x_ref, g_ref, beta_ref, rw_ref, rb_ref,
                  x1_ref, x1b_ref, idx_ref, gate_ref):
    nch = a_ref.shape[0]
    a = jnp.zeros((nch * S5_CHUNK, 256), F32)
    r = lax.broadcasted_iota(jnp.int32, (nch * S5_CHUNK, nch), 0)
    n = lax.broadcasted_iota(jnp.int32, (nch * S5_CHUNK, nch), 1)
    for t in range(S5_CHUNK):
        a = a + _dot((r == S5_CHUNK * n + t).astype(BF16), a_ref[:, t * 256:(t + 1) * 256])
    mixed = (_dot(a.astype(BF16), w_ref[0:256, :]) + _dot(b_ref[...], w_ref[256:512, :])
             + _dot(c_ref[...], w_ref[512:768, :]) + _dot(d_ref[...], w_ref[768:1024, :]))
    x1 = _layer_norm_rows(alpha * x_ref[...] + mixed, g_ref[...], beta_ref[...])
    x1_ref[...] = x1
    x1b = x1.astype(BF16)
    x1b_ref[...] = x1b
    logits = _dot(x1b, rw_ref[...]) + rb_ref[...]
    lane = lax.broadcasted_iota(jnp.int32, logits.shape, 1)
    vals, idxs = [], []
    for _ in range(TOP_K):
        m = jnp.max(logits, axis=-1, keepdims=True)
        ix = jnp.min(jnp.where(logits == m, lane, LANES), axis=-1, keepdims=True)
        vals.append(m)
        idxs.append(ix)
        logits = jnp.where(lane == ix, -jnp.inf, logits)
    es = [jnp.exp(vk - vals[0]) for vk in vals]
    tot = es[0] + es[1] + es[2] + es[3]
    idx_out = jnp.zeros(logits.shape, F32)
    gate_out = jnp.zeros(logits.shape, F32)
    for k in range(TOP_K):
        idx_out = jnp.where(lane == k, idxs[k].astype(F32), idx_out)
        gate_out = jnp.where(lane == k, es[k] / tot, gate_out)
    idx_ref[...] = jnp.transpose(idx_out)[:8, :].astype(jnp.int32)
    gate_ref[...] = gate_out


def _outproj(pieces, w, x, g, beta, rw, rb, alpha):
    t = x.shape[0]
    tm = min(512, t)
    row = lambda i: (i, 0)
    full = lambda i: (0, 0)
    return pl.pallas_call(
        functools.partial(_outproj_body, alpha),
        grid=(t // tm,),
        in_specs=[pl.BlockSpec((tm // S5_CHUNK, S5_CHUNK * 256), row)] + [pl.BlockSpec((tm, 256), row)] * 3 + [
            pl.BlockSpec((D_MODEL, D_MODEL), full),
            pl.BlockSpec((tm, D_MODEL), row),
            pl.BlockSpec((1, D_MODEL), full),
            pl.BlockSpec((1, D_MODEL), full),
            pl.BlockSpec((D_MODEL, LANES), full),
            pl.BlockSpec((1, LANES), full)],
        out_specs=[pl.BlockSpec((tm, D_MODEL), row), pl.BlockSpec((tm, D_MODEL), row),
                   pl.BlockSpec((8, tm), lambda i: (0, i)), pl.BlockSpec((tm, LANES), row)],
        out_shape=[jax.ShapeDtypeStruct((t, D_MODEL), F32), jax.ShapeDtypeStruct((t, D_MODEL), BF16),
                   jax.ShapeDtypeStruct((8, t), jnp.int32), jax.ShapeDtypeStruct((t, LANES), F32)],
        compiler_params=_cparams(("parallel",)),
        name="outproj_ln_router",
    )(*pieces, w, x, g, beta, rw, rb)


MOE_BLOCK = 1024
MOE_RANGES = 4


def _ffn_body(be_ref, nb_ref, x_ref, wg_ref, wl_ref, bg_ref, bl_ref, w2_ref, b2_ref, *rest):
    o_ref = rest[-1]
    i = pl.program_id(0)

    @pl.when(i < nb_ref[0])
    def _():
        x = x_ref[...]
        glu = jnp.minimum(_dot(x, wg_ref[...]) + bg_ref[...], SWIGLU_LIMIT)
        lin = jnp.clip(_dot(x, wl_ref[...]) + bl_ref[...], -SWIGLU_LIMIT, SWIGLU_LIMIT)
        act = (glu * jax.nn.sigmoid(SWIGLU_ALPHA * glu) * (lin + 1.0)).astype(BF16)
        o_ref[...] = (_dot(act, w2_ref[...]) + b2_ref[...]).astype(o_ref.dtype)

    @pl.when(i >= nb_ref[0])
    def _():
        o_ref[...] = jnp.zeros(o_ref.shape, o_ref.dtype)


def _ffn(xb, block_expert, n_used, wg, wl, bg, bl, w2, b2, yb_prev, n_slots, block_offset):
    n_blocks = xb.shape[0] // MOE_BLOCK
    de = wg.shape[2]
    expert = lambda i, be, nb: (be[i], 0, 0)
    in_specs = [pl.BlockSpec((MOE_BLOCK, D_MODEL), lambda i, be, nb: (i, 0)),
                pl.BlockSpec((None, D_MODEL, de), expert),
                pl.BlockSpec((None, D_MODEL, de), expert),
                pl.BlockSpec((None, 1, de), expert),
                pl.BlockSpec((None, 1, de), expert),
                pl.BlockSpec((None, de, D_MODEL), expert),
                pl.BlockSpec((None, 1, D_MODEL), expert)]
    args = [block_expert, n_used, xb, wg, wl, bg, bl, w2, b2]
    aliases = {}
    if yb_prev is not None:
        in_specs.append(pl.BlockSpec(memory_space=pl.ANY))
        aliases = {len(args): 0}
        args.append(yb_prev)
    grid_spec = pltpu.PrefetchScalarGridSpec(
        num_scalar_prefetch=2,
        grid=(n_blocks,),
        in_specs=in_specs,
        out_specs=pl.BlockSpec((MOE_BLOCK, D_MODEL), lambda i, be, nb: (i + block_offset, 0)),
    )
    return pl.pallas_call(
        _ffn_body,
        grid_spec=grid_spec,
        out_shape=jax.ShapeDtypeStruct((n_slots, D_MODEL), BF16),
        input_output_aliases=aliases,
        compiler_params=_cparams(("arbitrary",)),
        name="moe_ffn",
    )(*args)


def _w1_prep_body(w_ref, pg_ref, pl_ref, g_ref, l_ref):
    w = w_ref[...].astype(BF16)
    g_ref[...] = _dot(w, pg_ref[...]).astype(BF16)
    l_ref[...] = _dot(w, pl_ref[...]).astype(BF16)


def _w1_prep(w1):
    ne, dm, de2 = w1.shape
    tn = 512
    r = np.arange(tn)[:, None]
    c = np.arange(tn // 2)[None, :]
    sel_g = jnp.asarray(r == 2 * c, BF16)
    sel_l = jnp.asarray(r == 2 * c + 1, BF16)
    full = lambda e, j: (0, 0)
    out = jax.ShapeDtypeStruct((ne, dm, de2 // 2), BF16)
    return pl.pallas_call(
        _w1_prep_body,
        grid=(ne, de2 // tn),
        in_specs=[pl.BlockSpec((None, dm, tn), lambda e, j: (e, 0, j)),
                  pl.BlockSpec((tn, tn // 2), full), pl.BlockSpec((tn, tn // 2), full)],
        out_specs=[pl.BlockSpec((None, dm, tn // 2), lambda e, j: (e, 0, j))] * 2,
        out_shape=[out, out],
        compiler_params=_cparams(("parallel", "parallel")),
        name="w1_prep",
    )(w1, sel_g, sel_l)


def _ln2_body(alpha, x_ref, f_ref, g_ref, b_ref, o_ref):
    o_ref[...] = _layer_norm_rows(alpha * x_ref[...] + f_ref[...], g_ref[...], b_ref[...])


def _ln2(x, f, g, b, alpha):
    t = x.shape[0]
    tm = min(512, t)
    row = lambda i: (i, 0)
    full = lambda i: (0, 0)
    return pl.pallas_call(
        functools.partial(_ln2_body, alpha),
        grid=(t // tm,),
        in_specs=[pl.BlockSpec((tm, D_MODEL), row), pl.BlockSpec((tm, D_MODEL), row),
                  pl.BlockSpec((1, D_MODEL), full), pl.BlockSpec((1, D_MODEL), full)],
        out_specs=pl.BlockSpec((tm, D_MODEL), row),
        out_shape=jax.ShapeDtypeStruct((t, D_MODEL), F32),
        compiler_params=_cparams(("parallel",)),
        name="ln2",
    )(x, f, g, b)


S5_CHUNK = 16


S5_SEL_PAD = (SSM_GROUPS - 1) * SSM_GROUP_CH
S5_ROWS = 256


def _s5_sel():
    c = np.arange(256)
    target = (c // 16) * 256 + c % 16
    r = np.arange(S5_SEL_PAD + 4096)[:, None] - S5_SEL_PAD
    return jnp.asarray(r == target[None, :], BF16)


def _s5_in_body(x_ref, sel_ref, t_ref, w_ref, y_ref, h_ref):
    xcat = x_ref[...]
    us = []
    for g in range(SSM_GROUPS):
        start = S5_SEL_PAD - SSM_GROUP_CH * g
        u = _dot(xcat, sel_ref[start:start + 4096, :]).astype(BF16)
        y_ref[g] = _dot(u, t_ref[g]).astype(BF16)
        us.append(u)
    for i in range(SSM_GROUPS // 2):
        hp = _dot(jnp.concatenate([us[2 * i], us[2 * i + 1]], axis=1), w_ref[i])
        for c in range(4):
            h_ref[c, :, i * LANES:(i + 1) * LANES] = hp[:, c * LANES:(c + 1) * LANES]


def _s5_in(ux, sel, sp):
    t16 = ux.shape[0]
    rb = min(S5_ROWS, t16)
    full2 = lambda i: (0, 0)
    full3 = lambda i: (0, 0, 0)
    return pl.pallas_call(
        _s5_in_body,
        grid=(t16 // rb,),
        in_specs=[pl.BlockSpec((rb, S5_CHUNK * 256), lambda i: (i, 0)),
                  pl.BlockSpec(sel.shape, full2),
                  pl.BlockSpec((SSM_GROUPS, 256, 256), full3),
                  pl.BlockSpec((SSM_GROUPS // 2, 512, 512), full3)],
        out_specs=[pl.BlockSpec((SSM_GROUPS, rb, 256), lambda i: (0, i, 0)),
                   pl.BlockSpec((4, rb, 1024), lambda i: (0, i, 0))],
        out_shape=[jax.ShapeDtypeStruct((SSM_GROUPS, t16, 256), BF16),
                   jax.ShapeDtypeStruct((4, t16, 1024), F32)],
        compiler_params=_cparams(("parallel",)),
        name="s5_in",
    )(ux, sel, sp['t'], sp['w'])


def _s5_scan_body(n1, h_ref, lam_ref, e_ref):
    lam = lam_ref[...]
    zero = jnp.zeros((1, 1024), F32)

    def sweep(base, reverse):
        lr, li = lam[base:base + 1], lam[base + 1:base + 2]

        def tile(k, carry):
            er, ei = carry
            r0 = pl.multiple_of(((n1 // 8 - 1 - k) if reverse else k) * 8, 8)
            hr = h_ref[base, pl.ds(r0, 8), :]
            hi = h_ref[base + 1, pl.ds(r0, 8), :]
            outs_r, outs_i = [None] * 8, [None] * 8
            for j in (range(7, -1, -1) if reverse else range(8)):
                outs_r[j], outs_i[j] = er, ei
                er, ei = (lr * er - li * ei + hr[j:j + 1], lr * ei + li * er + hi[j:j + 1])
            e_ref[base, pl.ds(r0, 8), :] = jnp.concatenate(outs_r, axis=0)
            e_ref[base + 1, pl.ds(r0, 8), :] = jnp.concatenate(outs_i, axis=0)
            return er, ei

        lax.fori_loop(0, n1 // 8, tile, (zero, zero))

    sweep(0, False)
    sweep(2, True)


def _s5_scan(h, lam, b, n1):
    return pl.pallas_call(
        functools.partial(_s5_scan_body, n1),
        grid=(b,),
        in_specs=[pl.BlockSpec((4, n1, 1024), lambda i: (0, i, 0)),
                  pl.BlockSpec((8, 1024), lambda i: (0, 0))],
        out_specs=pl.BlockSpec((4, n1, 1024), lambda i: (0, i, 0)),
        out_shape=jax.ShapeDtypeStruct(h.shape, F32),
        compiler_params=_cparams(("parallel",)),
        name="s5_scan",
    )(h, lam)


def _s5_fin_body(y_ref, e_ref, sel_ref, v_ref, w_ref, b_ref, o_ref):
    ys = []
    for i in range(SSM_GROUPS // 2):
        ep = jnp.concatenate([e_ref[c, :, i * LANES:(i + 1) * LANES] for c in range(4)], axis=1)
        yi = _dot(ep.astype(BF16), v_ref[i])
        ys.append((y_ref[2 * i].astype(F32) + yi[:, :256]).astype(BF16))
        ys.append((y_ref[2 * i + 1].astype(F32) + yi[:, 256:]).astype(BF16))
    ycat = jnp.concatenate(ys, axis=1)
    for t in range(S5_CHUNK):
        start = S5_SEL_PAD - SSM_GROUP_CH * t
        y = jax.nn.gelu(_dot(ycat, sel_ref[start:start + 4096, :]))
        z = _dot(y.astype(BF16), w_ref[...]) + b_ref[...]
        o_ref[:, t * 256:(t + 1) * 256] = (y * jax.nn.sigmoid(z)).astype(BF16)


def _s5_fin(y, e, sel, sp, w, bias):
    t16 = y.shape[1]
    rb = min(S5_ROWS, t16)
    full2 = lambda i: (0, 0)
    full3 = lambda i: (0, 0, 0)
    return pl.pallas_call(
        _s5_fin_body,
        grid=(t16 // rb,),
        in_specs=[pl.BlockSpec((SSM_GROUPS, rb, 256), lambda i: (0, i, 0)),
                  pl.BlockSpec((4, rb, 1024), lambda i: (0, i, 0)),
                  pl.BlockSpec(sel.shape, full2),
                  pl.BlockSpec((SSM_GROUPS // 2, 512, 512), full3),
                  pl.BlockSpec((256, 256), full2),
                  pl.BlockSpec((1, 256), full2)],
        out_specs=pl.BlockSpec((rb, S5_CHUNK * 256), lambda i: (i, 0)),
        out_shape=jax.ShapeDtypeStruct((t16, S5_CHUNK * 256), BF16),
        compiler_params=_cparams(("parallel",)),
        name="s5_fin",
    )(y, e, sel, sp['v'], w, bias)


def _s5(ux, q, sel, b, l):
    y, h = _s5_in(ux, sel, q['s5'])
    e = _s5_scan(h, q['s5']['lam'], b, l // S5_CHUNK)
    return _s5_fin(y, e, sel, q['s5'], q['ssm_glu_w'], q['ssm_glu_b'])


def _s5_prep(p):
    hp = lax.Precision.HIGHEST
    c = S5_CHUNK
    tau = jnp.arange(c + 1, dtype=F32)[:, None, None]
    ks, ws, vs, lams = [], [], [], []
    for d in (0, 1):
        lam_re = p['ssm_a_re'][d]
        lam_im = p['ssm_a_im'][d]
        dt = jnp.exp(p['ssm_log_dt'][d])[:, None]
        mag = jnp.exp(lam_re * dt)
        abar_re = mag * jnp.cos(lam_im * dt)
        abar_im = mag * jnp.sin(lam_im * dt)
        den = lam_re * lam_re + lam_im * lam_im
        coef_re = ((abar_re - 1.0) * lam_re + abar_im * lam_im) / den
        coef_im = (abar_im * lam_re - (abar_re - 1.0) * lam_im) / den
        br, bi = p['ssm_b_re'][d], p['ssm_b_im'][d]
        bbar_re = coef_re[..., None] * br - coef_im[..., None] * bi
        bbar_im = coef_re[..., None] * bi + coef_im[..., None] * br
        cr, ci = p['ssm_c_re'][d], p['ssm_c_im'][d]
        pm = jnp.exp(tau * (lam_re * dt)[None])
        pr = pm * jnp.cos(tau * (lam_im * dt)[None])
        pi = pm * jnp.sin(tau * (lam_im * dt)[None])
        m_re = cr[None] * pr[:, :, None, :] - ci[None] * pi[:, :, None, :]
        m_im = cr[None] * pi[:, :, None, :] + ci[None] * pr[:, :, None, :]
        k = (jnp.einsum('tghp,gpk->gtkh', m_re[:c], bbar_re, precision=hp)
             - jnp.einsum('tghp,gpk->gtkh', m_im[:c], bbar_im, precision=hp))
        ks.append(k)
        pw = jnp.arange(c - 1, -1, -1) if d == 0 else jnp.arange(c)
        w_re = pr[pw][:, :, None, :] * jnp.swapaxes(bbar_re, 1, 2)[None] - pi[pw][:, :, None, :] * jnp.swapaxes(bbar_im, 1, 2)[None]
        w_im = pr[pw][:, :, None, :] * jnp.swapaxes(bbar_im, 1, 2)[None] + pi[pw][:, :, None, :] * jnp.swapaxes(bbar_re, 1, 2)[None]
        ws.append((jnp.transpose(w_re, (1, 0, 2, 3)).reshape(SSM_GROUPS, 256, SSM_STATE),
                   jnp.transpose(w_im, (1, 0, 2, 3)).reshape(SSM_GROUPS, 256, SSM_STATE)))
        po = jnp.arange(1, c + 1) if d == 0 else jnp.arange(c, 0, -1)
        v_re = m_re[po]
        v_im = m_im[po]
        vs.append((jnp.transpose(v_re, (1, 3, 0, 2)).reshape(SSM_GROUPS, SSM_STATE, 256),
                   jnp.transpose(-v_im, (1, 3, 0, 2)).reshape(SSM_GROUPS, SSM_STATE, 256)))
        lams.append((pr[c], pi[c]))
    j = jnp.arange(c)[:, None]
    t = jnp.arange(c)[None, :]
    k0 = ks[0][:, jnp.clip(t - j, 0, c - 1)] * (t >= j)[None, :, :, None, None]
    k1 = ks[1][:, jnp.clip(j - t, 0, c - 1)] * (j >= t)[None, :, :, None, None]
    tm = jnp.transpose(k0 + k1, (0, 1, 3, 2, 4)).reshape(SSM_GROUPS, 256, 256)
    dd = jnp.tile(p['ssm_d'], (1, c))
    tm = tm + jnp.eye(256, dtype=F32)[None] * dd[:, None, :]
    npair = SSM_GROUPS // 2

    def pair_diag(x):
        g, r, cc = x.shape
        x = x.reshape(npair, 2, r, cc)
        z = jnp.zeros((npair, 2, r, 2, cc), F32)
        z = z.at[:, 0, :, 0, :].set(x[:, 0]).at[:, 1, :, 1, :].set(x[:, 1])
        return z.reshape(npair, 2 * r, 2 * cc)

    w = jnp.concatenate([pair_diag(ws[0][0]), pair_diag(ws[0][1]), pair_diag(ws[1][0]), pair_diag(ws[1][1])], axis=2)
    v = jnp.concatenate([pair_diag(vs[0][0]), pair_diag(vs[0][1]), pair_diag(vs[1][0]), pair_diag(vs[1][1])], axis=1)
    lam = jnp.stack([lams[0][0], lams[0][1], lams[1][0], lams[1][1]], axis=0)
    lam = lam.reshape(4, SSM_GROUPS * SSM_STATE)
    lam = jnp.concatenate([lam, jnp.zeros_like(lam)], axis=0)
    return {'t': tm.astype(BF16), 'w': w.astype(BF16), 'v': v.astype(BF16), 'lam': lam}


GDN_HALO = 16


def _gdn_prep_body(nt, q_ref, qp_ref, qn_ref, k_ref, kp_ref, kn_ref, v_ref, vp_ref, vn_ref,
                   gate_ref, cw_ref, al_ref, dtb_ref, bd_ref, qo_ref, ko_ref, vo_ref, gb_ref):
    i = pl.program_id(1)
    cw = cw_ref[...]
    tl = q_ref.shape[0]

    def conv(cur_ref, prev_ref, next_ref, col):
        prev = jnp.where(i > 0, prev_ref[...].astype(F32), 0.0)
        nxt = jnp.where(i < nt - 1, next_ref[...].astype(F32), 0.0)
        xe = jnp.concatenate([prev, cur_ref[...].astype(F32), nxt], axis=0)
        n = xe.shape[0]
        acc = jnp.zeros((tl, 256), F32)
        for tap in range(5):
            s = tap - 2
            sh = xe if s == 0 else pltpu.roll(xe, (-s) % n, 0)
            acc = acc + sh[GDN_HALO:GDN_HALO + tl] * cw[tap:tap + 1, col:col + 256]
        return acc * jax.nn.sigmoid(acc)

    bd = bd_ref[...]

    def l2n(x):
        return x * lax.rsqrt(_head_mean_sq(x, bd) * HEAD_DIM + NORM_EPS)

    qo_ref[...] = l2n(conv(q_ref, qp_ref, qn_ref, 0)).astype(BF16)
    ko_ref[...] = l2n(conv(k_ref, kp_ref, kn_ref, 256)).astype(BF16)
    vo_ref[...] = conv(v_ref, vp_ref, vn_ref, 512).astype(BF16)
    gt = gate_ref[...]
    lane = lax.broadcasted_iota(jnp.int32, gt.shape, 1)
    x = gt + dtb_ref[...]
    softplus = jnp.maximum(x, 0.0) + jnp.log(1.0 + jnp.exp(-jnp.abs(x)))
    gb_ref[...] = jnp.where(lane < 8, -al_ref[...] * softplus, jnp.where(lane < 16, jax.nn.sigmoid(gt), 0.0))


def _gdn_prep(main, gate, q, bd, b, l):
    tl = min(256, l)
    nt = l // tl
    hb = tl // GDN_HALO
    nh = l // GDN_HALO

    def cur(col):
        return pl.BlockSpec((tl, 256), lambda bi, i: (bi * nt + i, col // 256))

    def prev(col):
        return pl.BlockSpec((GDN_HALO, 256), lambda bi, i: (bi * nh + jnp.maximum(i * hb - 1, 0), col // 256))

    def nxt(col):
        return pl.BlockSpec((GDN_HALO, 256), lambda bi, i: (bi * nh + jnp.minimum((i + 1) * hb, nh - 1), col // 256))

    full = lambda bi, i: (0, 0)
    row = lambda bi, i: (bi * nt + i, 0)
    specs = []
    for col in (COL_CQ, COL_CK, COL_CV):
        specs += [cur(col), prev(col), nxt(col)]
    specs += [pl.BlockSpec((tl, GATE_W), row), pl.BlockSpec((8, 768), full), pl.BlockSpec((1, LANES), full),
              pl.BlockSpec((1, LANES), full), pl.BlockSpec((256, 256), full)]
    return pl.pallas_call(
        functools.partial(_gdn_prep_body, nt),
        grid=(b, nt),
        in_specs=specs,
        out_specs=[pl.BlockSpec((tl, 256), row)] * 3 + [pl.BlockSpec((tl, GATE_W), row)],
        out_shape=[jax.ShapeDtypeStruct((b * l, 256), BF16)] * 3 + [jax.ShapeDtypeStruct((b * l, GATE_W), F32)],
        compiler_params=_cparams(("parallel", "parallel")),
        name="gdn_prep",
    )(*([main] * 9), gate, q['gdn_conv_w'], q['gdn_al'], q['gdn_dtb'], bd)


def _gdn_chunk_step(chains):
    c = GDN_CHUNK
    r256 = lax.broadcasted_iota(jnp.int32, (256, 256), 0)
    c256 = lax.broadcasted_iota(jnp.int32, (256, 256), 1)
    blockmask = (r256 // c) == (c256 // c)
    rl = lax.broadcasted_iota(jnp.int32, (LANES, 256), 0)
    cl = lax.broadcasted_iota(jnp.int32, (LANES, 256), 1)
    ri = lax.broadcasted_iota(jnp.int32, (c, c), 0)
    ci = lax.broadcasted_iota(jnp.int32, (c, c), 1)
    i_s = lax.broadcasted_iota(jnp.int32, (c, 256), 0)
    j_s = lax.broadcasted_iota(jnp.int32, (c, 256), 1) % c
    ones = jnp.ones((c, c), BF16)
    eye = (i_s == j_s).astype(F32)
    scale = HEAD_DIM ** -0.5

    def bdv(y):
        return jnp.where(blockmask, jnp.concatenate([y, y, y, y], axis=0), 0.0).astype(BF16)

    def hilo(x):
        hi = x.astype(BF16)
        return hi, (x - hi.astype(F32)).astype(BF16)

    st = []
    for dirn, qb, kb16, vb, gb, s_ref in chains:
        rev = dirn == 1
        e_g = (rl == dirn * 4 + cl // c).astype(BF16)
        e_b = (rl == 8 + dirn * 4 + cl // c).astype(BF16)
        ghi, glo = hilo(gb)
        st.append(dict(rev=rev, q=qb.astype(F32), k=kb16.astype(F32), v=vb.astype(F32), s_ref=s_ref,
                       g=_dot(ghi, e_g) + _dot(glo, e_g), beta=_dot(ghi, e_b) + _dot(glo, e_b),
                       tri=((ci >= ri) if rev else (ci <= ri)).astype(BF16),
                       allowed=(j_s >= i_s) if rev else (j_s <= i_s),
                       strict=(j_s > i_s) if rev else (j_s < i_s)))
    for d in st:
        ghi, glo = hilo(d['g'])
        d['gc'] = _dot(d['tri'], ghi) + _dot(d['tri'], glo)
    for d in st:
        zhi, zlo = hilo(jnp.where(i_s == j_s, d['gc'], 0.0))
        d['gct'] = _dot(ones, zhi) + _dot(ones, zlo)
    for d in st:
        d['decay'] = jnp.exp(jnp.where(d['allowed'], d['gc'] - d['gct'], -jnp.inf))
        d['eg'] = jnp.exp(d['gc'])
        d['kbeta'] = d['k'] * d['beta']
    for d in st:
        kk_qk = _dot_nt(jnp.concatenate([d['kbeta'], d['q'] * scale], axis=0).astype(BF16), bdv(d['k']))
        d['p'] = -jnp.where(d['strict'], kk_qk[:c] * d['decay'], 0.0)
        d['intra'] = jnp.where(d['allowed'], kk_qk[c:] * d['decay'], 0.0)
        d['t'] = eye + d['p']
    for _ in range(5):
        for d in st:
            d['p'] = _dot(d['p'].astype(BF16), bdv(d['p']))
        for d in st:
            d['t'] = d['t'] + _dot(d['t'].astype(BF16), bdv(d['p']))
    for d in st:
        t16 = d['t'].astype(BF16)
        d['u'] = _dot(t16, bdv(d['v'] * d['beta']))
        d['w'] = _dot(t16, bdv(d['kbeta'] * d['eg']))
    for d in st:
        d['s'] = d['s_ref'][...]
        d['ws_qs'] = _dot(jnp.concatenate([d['w'], d['q'] * scale * d['eg']], axis=0).astype(BF16),
                          d['s'].astype(BF16))
    for d in st:
        d['v_new'] = d['u'] - d['ws_qs'][:c]
        d['o'] = d['ws_qs'][c:] + _dot(d['intra'].astype(BF16), bdv(d['v_new']))
    for d in st:
        last = 0 if d['rev'] else c - 1
        g_last = d['gc'][last:last + 1, :]
        kg = (d['k'] * jnp.exp(g_last - d['gc'])).astype(BF16)
        upd = lax.dot_general(kg, d['v_new'].astype(BF16), (((0,), (0,)), ((), ())), preferred_element_type=F32)
        d['s_ref'][...] = d['s'] * jnp.exp(g_last) + jnp.where(blockmask, upd, 0.0)
    return [d['o'] for d in st]


GDN_SEQS = 4


def _gdn_chunk_body(qf_ref, kf_ref, vf_ref, gf_ref, qb_ref, kb_ref, vb_ref, gb_ref, of_ref, ob_ref, s_ref):
    @pl.when(pl.program_id(1) == 0)
    def _():
        s_ref[...] = jnp.zeros(s_ref.shape, F32)

    chains = []
    for j in range(qf_ref.shape[0]):
        chains.append((0, qf_ref[j], kf_ref[j], vf_ref[j], gf_ref[j], s_ref.at[j, 0]))
        chains.append((1, qb_ref[j], kb_ref[j], vb_ref[j], gb_ref[j], s_ref.at[j, 1]))
    outs = _gdn_chunk_step(chains)
    for j in range(qf_ref.shape[0]):
        of_ref[j] = outs[2 * j].astype(BF16)
        ob_ref[j] = outs[2 * j + 1].astype(BF16)


def _gdn_chunks(qn, kn, vs, gb, b, l):
    c = GDN_CHUNK
    n = l // c
    nseq = GDN_SEQS if b % GDN_SEQS == 0 else 1
    fwd = lambda bi, i: (bi, i, 0)
    bwd = lambda bi, i: (bi, n - 1 - i, 0)
    blk = lambda m: pl.BlockSpec((nseq, c, 256), m)
    gblk = lambda m: pl.BlockSpec((nseq, c, GATE_W), m)
    qn, kn, vs = (a.reshape(b, l, 256) for a in (qn, kn, vs))
    gb = gb.reshape(b, l, GATE_W)
    of, ob = pl.pallas_call(
        _gdn_chunk_body,
        grid=(b // nseq, n),
        in_specs=[blk(fwd), blk(fwd), blk(fwd), gblk(fwd), blk(bwd), blk(bwd), blk(bwd), gblk(bwd)],
        out_specs=[blk(fwd), blk(bwd)],
        out_shape=[jax.ShapeDtypeStruct((b, l, 256), BF16)] * 2,
        scratch_shapes=[pltpu.VMEM((nseq, 2, 256, 256), F32)],
        compiler_params=_cparams(("parallel", "arbitrary")),
        name="gdn_chunks",
    )(qn, kn, vs, gb, qn, kn, vs, gb)
    return of.reshape(b * l, 256), ob.reshape(b * l, 256)


def _gdn_out_body(of_ref, ob_ref, z_ref, g_ref, bd_ref, o_ref):
    o = of_ref[...].astype(F32) + ob_ref[...].astype(F32)
    z = z_ref[...].astype(F32)
    ms = _head_mean_sq(o, bd_ref[...])
    o_ref[...] = (o * lax.rsqrt(ms + NORM_EPS) * g_ref[...] * (z * jax.nn.sigmoid(z))).astype(BF16)


def _gdn_out(of, ob, main, g, bd):
    t = of.shape[0]
    tm = min(1024, t)
    row = lambda i: (i, 0)
    full = lambda i: (0, 0)
    return pl.pallas_call(
        _gdn_out_body,
        grid=(t // tm,),
        in_specs=[pl.BlockSpec((tm, 256), row), pl.BlockSpec((tm, 256), row),
                  pl.BlockSpec((tm, 256), lambda i: (i, COL_CZ // 256)),
                  pl.BlockSpec((1, 256), full), pl.BlockSpec((256, 256), full)],
        out_specs=pl.BlockSpec((tm, 256), row),
        out_shape=jax.ShapeDtypeStruct((t, 256), BF16),
        compiler_params=_cparams(("parallel",)),
        name="gdn_out",
    )(of, ob, main, g, bd)


def _gdn(main, gate, q, bd, b, l):
    qn, kn, vs, gb = _gdn_prep(main, gate, q, bd, b, l)
    of, ob = _gdn_chunks(qn, kn, vs, gb, b, l)
    return _gdn_out(of, ob, main, q['gdn_o_norm'], bd)


def _moe(x1, x1b, top_idx, gates, p):
    t = x1.shape[0]
    m = t * TOP_K
    flat_e = top_idx.reshape(-1)
    order = jnp.argsort(flat_e, stable=True).astype(jnp.int32)
    inv = jnp.argsort(order).astype(jnp.int32)
    sorted_e = flat_e[order]
    experts = jnp.arange(N_EXPERTS, dtype=jnp.int32)
    counts = jnp.sum((top_idx[None] == experts[:, None, None]).astype(jnp.int32), axis=(1, 2))
    padded = (counts + MOE_BLOCK - 1) // MOE_BLOCK * MOE_BLOCK
    group_start = jnp.cumsum(counts) - counts
    padded_end = jnp.cumsum(padded)
    padded_start = padded_end - padded
    dest = (padded_start[sorted_e] + jnp.arange(m, dtype=jnp.int32) - group_start[sorted_e]).astype(jnp.int32)
    n_blocks = -(-m // MOE_BLOCK) + N_EXPERTS
    n_slots = n_blocks * MOE_BLOCK
    block_start = jnp.arange(n_blocks, dtype=jnp.int32) * MOE_BLOCK
    block_expert = jnp.minimum(jnp.sum((padded_end[None, :] <= block_start[:, None]).astype(jnp.int32), axis=1),
                               N_EXPERTS - 1).astype(jnp.int32)
    n_used = (padded_end[-1] // MOE_BLOCK).astype(jnp.int32).reshape(1)
    slot = jnp.arange(n_slots, dtype=jnp.int32)
    slot_e = jnp.repeat(block_expert, MOE_BLOCK)
    rank = slot - padded_start[slot_e]
    src = jnp.clip(group_start[slot_e] + rank, 0, m - 1)
    slot_token = jnp.where(rank < counts[slot_e], order[src] % t, 0).astype(jnp.int32)
    n_ranges = MOE_RANGES if n_blocks % MOE_RANGES == 0 else 1
    rblocks = n_blocks // n_ranges
    yb = None
    for r in range(n_ranges):
        b0 = r * rblocks
        xb = x1b[slot_token[b0 * MOE_BLOCK:(b0 + rblocks) * MOE_BLOCK]]
        yb = _ffn(xb, block_expert[b0:b0 + rblocks], jnp.clip(n_used - b0, 0, rblocks),
                  p['moe_wg'], p['moe_wl'], p['moe_bg'], p['moe_bl'], p['moe_w2'], p['moe_b2'],
                  yb, n_slots, b0)
    dest_a = dest[inv]
    out = jnp.zeros((t, D_MODEL), F32)
    for k in range(TOP_K):
        out = out + yb[dest_a[k * t:(k + 1) * t]].astype(F32) * gates[:, k:k + 1]
    return out


def _block_diag_mean(width, group):
    idx = np.arange(width)
    return jnp.asarray((idx[:, None] // group == idx[None, :] // group) / group, BF16)


def _axial_tables(l):
    rows = l // GRID_W
    row_pos = np.repeat(np.arange(rows), GRID_W).astype(np.float64)
    col_pos = np.tile(np.arange(GRID_W), rows).astype(np.float64)
    lane = np.arange(LANES)
    d = lane % HEAD_DIM
    e = d % 32
    f = e % 16
    inv = (AXIAL_THETA ** (-(np.arange(0, 32, 2, dtype=np.float32)) / 32)).astype(np.float32)
    pos = np.where((d // 32)[None, :] == 0, row_pos[:, None], col_pos[:, None]).astype(np.float32)
    ang = pos * inv[f][None, :]
    sign = np.where(e < 16, -1.0, 1.0)[None, :]
    return jnp.asarray(np.cos(ang), F32), jnp.asarray(np.sin(ang) * sign, F32)


def _diff_tables(l):
    lane = np.arange(256)
    e = lane % 32
    f = e % 4
    inv = (ROPE_THETA ** (-(np.arange(0, PARTIAL_ROPE_DIMS, 2, dtype=np.float32)) / PARTIAL_ROPE_DIMS)).astype(np.float32)
    ang = np.arange(l, dtype=np.float32)[:, None] * inv[f][None, :]
    roped = (e < PARTIAL_ROPE_DIMS)[None, :]
    sign = np.where(e < 4, -1.0, 1.0)[None, :]
    c = np.where(roped, np.cos(ang), 1.0)
    s = np.where(roped, np.sin(ang) * sign, 0.0)
    return jnp.asarray(c, F32), jnp.asarray(s, F32)


def _prep_layer(params, layer, w1_all):
    p = {name: arr[layer] for name, arr in params.items()}
    w_in = p['w_in']
    w_in = jnp.concatenate([w_in[:, :ORIG_CA], w_in[:, ORIG_CA + 16:], w_in[:, ORIG_CA:ORIG_CA + 16],
                            jnp.zeros((D_MODEL, GATE_W - 16), F32)], axis=1)
    q = dict(p)
    q['w_in'] = w_in.astype(BF16)
    q['w_out'] = p['w_out'].astype(BF16)
    q['router_w'] = jnp.concatenate([p['router_w'], jnp.zeros((D_MODEL, LANES - N_EXPERTS), F32)], axis=1).astype(BF16)
    q['router_b'] = jnp.concatenate([p['router_b'], jnp.full((LANES - N_EXPERTS,), -1e30, F32)])[None, :]
    q['moe_wg'] = w1_all[0][layer]
    q['moe_wl'] = w1_all[1][layer]
    q['moe_bg'] = p['moe_b1'][:, None, 0::2]
    q['moe_bl'] = p['moe_b1'][:, None, 1::2]
    q['moe_w2'] = p['moe_w2'].astype(BF16)
    q['moe_b2'] = p['moe_b2'][:, None, :]
    for name in ('ln1_g', 'ln1_b', 'ln2_g', 'ln2_b'):
        q[name] = p[name][None, :]
    q['gqa_q_norm'] = jnp.tile(p['gqa_q_norm'], 4)[None, :]
    q['gqa_k_norm'] = jnp.tile(p['gqa_k_norm'], 2)[None, :]
    lambda_init = 0.8 - 0.6 * math.exp(-0.3 * layer)
    q['diff_subln'] = (jnp.tile(p['diff_subln'], 4) * (1.0 - lambda_init))[None, :]
    lam = (jnp.exp(jnp.sum(p['diff_lambda_q1'] * p['diff_lambda_k1']))
           - jnp.exp(jnp.sum(p['diff_lambda_q2'] * p['diff_lambda_k2'])) + lambda_init)
    q['diff_lam'] = lam.reshape(1).astype(F32)
    q['s5'] = _s5_prep(p)
    q['ssm_glu_w'] = p['ssm_glu_w'].astype(BF16)
    q['ssm_glu_b'] = p['ssm_glu_b'][None, :]
    q['gdn_conv_w'] = jnp.concatenate([p['gdn_conv_w'], jnp.zeros((3, 768), F32)], axis=0)
    pad8 = lambda x: jnp.concatenate([x.reshape(-1), jnp.zeros((LANES - 8,), F32)])[None, :]
    q['gdn_al'] = pad8(jnp.exp(p['gdn_a_log']))
    q['gdn_dtb'] = pad8(p['gdn_dt_bias'])
    q['gdn_o_norm'] = jnp.tile(p['gdn_o_norm'], 4)[None, :]
    return q


def _mixers(main, gate, ux, q, tabs, b, l):
    a_out = _s5(ux, q, tabs['s5sel'], b, l)
    b_out = _gqa(main, tabs['axial'], q['gqa_q_norm'], q['gqa_k_norm'], tabs['bd64'], b, l)
    c_out = _gdn(main, gate, q, tabs['bd64'], b, l)
    d_out = _diff(main, q['diff_lam'], tabs['diff'], q['diff_subln'], tabs['bd64'], b, l)
    return a_out, b_out, c_out, d_out


def _layer(x, q, tabs, b, l, alpha):
    main, gate, ux = _inproj(x, q['w_in'])
    pieces = _mixers(main, gate, ux, q, tabs, b, l)
    x1, x1b, idx, gates = _outproj(pieces, q['w_out'], x, q['ln1_g'], q['ln1_b'],
                                   q['router_w'], q['router_b'], alpha)
    ffn = _moe(x1, x1b, idx[:TOP_K], gates, q)
    return _ln2(x1, ffn, q['ln2_g'], q['ln2_b'], alpha)


def _trunk(x, layers, alpha):
    b, l, _ = x.shape
    tabs = {'axial': _axial_tables(l), 'diff': _diff_tables(l), 'bd64': _block_diag_mean(256, HEAD_DIM),
            's5sel': _s5_sel()}
    h = x.reshape(b * l, D_MODEL)
    for q in layers:
        h = _layer(h, q, tabs, b, l, alpha)
    return h.reshape(b, l, D_MODEL)


def kernel(x_prompt, x_sample, w_in, w_out, ssm_a_re, ssm_a_im, ssm_log_dt, ssm_b_re, ssm_b_im, ssm_c_re, ssm_c_im, ssm_d, ssm_glu_w, ssm_glu_b, gqa_q_norm, gqa_k_norm, gdn_conv_w, gdn_a_log, gdn_dt_bias, gdn_o_norm, diff_lambda_q1, diff_lambda_k1, diff_lambda_q2, diff_lambda_k2, diff_subln, router_w, router_b, moe_w1, moe_b1, moe_w2, moe_b2, ln1_g, ln1_b, ln2_g, ln2_b):
    params = {
        'w_in': w_in, 'w_out': w_out,
        'ssm_a_re': ssm_a_re, 'ssm_a_im': ssm_a_im, 'ssm_log_dt': ssm_log_dt,
        'ssm_b_re': ssm_b_re, 'ssm_b_im': ssm_b_im, 'ssm_c_re': ssm_c_re, 'ssm_c_im': ssm_c_im,
        'ssm_d': ssm_d, 'ssm_glu_w': ssm_glu_w, 'ssm_glu_b': ssm_glu_b,
        'gqa_q_norm': gqa_q_norm, 'gqa_k_norm': gqa_k_norm,
        'gdn_conv_w': gdn_conv_w, 'gdn_a_log': gdn_a_log, 'gdn_dt_bias': gdn_dt_bias, 'gdn_o_norm': gdn_o_norm,
        'diff_lambda_q1': diff_lambda_q1, 'diff_lambda_k1': diff_lambda_k1,
        'diff_lambda_q2': diff_lambda_q2, 'diff_lambda_k2': diff_lambda_k2, 'diff_subln': diff_subln,
        'router_w': router_w, 'router_b': router_b,
        'moe_w1': moe_w1, 'moe_b1': moe_b1, 'moe_w2': moe_w2, 'moe_b2': moe_b2,
        'ln1_g': ln1_g, 'ln1_b': ln1_b, 'ln2_g': ln2_g, 'ln2_b': ln2_b,
    }
    depth = w_in.shape[0]
    alpha = (2.0 * depth) ** 0.25
    w1_all = [w.reshape(moe_w1.shape[:3] + (-1,)) for w in _w1_prep(moe_w1.reshape((-1,) + moe_w1.shape[2:]))]
    layers = [_prep_layer(params, layer, w1_all) for layer in range(depth)]
    return (_trunk(x_prompt, layers, alpha), _trunk(x_sample, layers, alpha))
```

```python
import functools
import math

import jax
import jax.numpy as jnp
import numpy as np
from jax import lax
from jax.experimental import pallas as pl
from jax.experimental.pallas import tpu as pltpu

F32 = jnp.float32
BF16 = jnp.bfloat16

D_MODEL = 1024
GROUP_WIDTH = 256
HEAD_DIM = 64
SSM_GROUPS = 16
SSM_GROUP_CH = 16
SSM_STATE = 64
GDN_HEADS = 4
GDN_CHUNK = 64
DIFF_SUB_DIM = 32
PARTIAL_ROPE_DIMS = 8
AXIAL_THETA = 10000.0
ROPE_THETA = 500000.0
GRID_W = 64
N_EXPERTS = 32
TOP_K = 4
SWIGLU_ALPHA = 1.702
SWIGLU_LIMIT = 7.0
NORM_EPS = 1e-6
LN_EPS = 1e-5
LOG2E = math.log2(math.e)

LANES = 128
VMEM_LIMIT = 56 * 1024 * 1024

COL_U, COL_GQ, COL_GK, COL_GV = 0, 256, 512, 640
COL_CQ, COL_CK, COL_CV, COL_CZ = 768, 1024, 1280, 1536
COL_DQ, COL_DK, COL_DV = 1792, 2048, 2304
MAIN_W = 2560
GATE_W = 128
ORIG_CA = 1792


def _cparams(sem):
    return pltpu.CompilerParams(dimension_semantics=sem, vmem_limit_bytes=VMEM_LIMIT)


def _dot(a, b):
    return jnp.dot(a, b, preferred_element_type=F32)


def _dot_nt(a, b):
    return lax.dot_general(a, b, (((1,), (1,)), ((), ())), preferred_element_type=F32)


def _split_dot(x, m):
    hi = x.astype(BF16)
    lo = (x - hi.astype(F32)).astype(BF16)
    return _dot(hi, m) + _dot(lo, m)


def _chunk_rows_sel(n_chunks, t):
    r = lax.broadcasted_iota(jnp.int32, (n_chunks, S5_CHUNK * n_chunks), 0)
    c = lax.broadcasted_iota(jnp.int32, (n_chunks, S5_CHUNK * n_chunks), 1)
    return (c == S5_CHUNK * r + t).astype(BF16)


def _inproj_body(x_ref, w_ref, main_ref, gate_ref, ux_ref):
    x = x_ref[...].astype(BF16)
    step = 640
    for c in range(MAIN_W // step):
        main_ref[:, c * step:(c + 1) * step] = _dot(x, w_ref[:, c * step:(c + 1) * step]).astype(BF16)
    gate_ref[...] = _dot(x, w_ref[:, MAIN_W:])
    u = main_ref[:, COL_U:COL_U + 256]
    for t in range(S5_CHUNK):
        ux_ref[:, t * 256:(t + 1) * 256] = _dot(_chunk_rows_sel(ux_ref.shape[0], t), u).astype(BF16)


def _inproj(x, w):
    t = x.shape[0]
    tm = min(512, t)
    return pl.pallas_call(
        _inproj_body,
        grid=(t // tm,),
        in_specs=[pl.BlockSpec((tm, D_MODEL), lambda i: (i, 0)),
                  pl.BlockSpec((D_MODEL, MAIN_W + GATE_W), lambda i: (0, 0))],
        out_specs=[pl.BlockSpec((tm, MAIN_W), lambda i: (i, 0)),
                   pl.BlockSpec((tm, GATE_W), lambda i: (i, 0)),
                   pl.BlockSpec((tm // S5_CHUNK, S5_CHUNK * 256), lambda i: (i, 0))],
        out_shape=[jax.ShapeDtypeStruct((t, MAIN_W), BF16),
                   jax.ShapeDtypeStruct((t, GATE_W), F32),
                   jax.ShapeDtypeStruct((t // S5_CHUNK, S5_CHUNK * 256), BF16)],
        compiler_params=_cparams(("parallel",)),
        name="inproj",
    )(x, w)


def _head_mean_sq(xf, bd):
    return _split_dot(xf * xf, bd)


def _rope_lanes(x, c, s, half):
    n = x.shape[-1]
    lane = lax.broadcasted_iota(jnp.int32, x.shape, 1)
    first = (lane % (2 * half)) < half
    swapped = jnp.where(first, pltpu.roll(x, n - half, 1), pltpu.roll(x, half, 1))
    return x * c + swapped * s


def _gqa_body(q_ref, k_ref, v_ref, cq_ref, sq_ref, ck_ref, sk_ref, gq_ref, gk_ref, bd_ref,
              o_ref, kdup_ref, vaug_ref):
    i = pl.program_id(1)
    bd = bd_ref[...]

    @pl.when(i == 0)
    def _():
        kf = k_ref[...].astype(F32)
        ms = _head_mean_sq(kf, bd[:LANES, :LANES])
        kn = kf * lax.rsqrt(ms + NORM_EPS) * gk_ref[...]
        kn = _rope_lanes(kn, ck_ref[...], sk_ref[...], 16)
        lane = lax.broadcasted_iota(jnp.int32, kn.shape, 1)
        sw = pltpu.roll(kn, 64, 1)
        kdup_ref[0] = jnp.where(lane < 64, kn, sw).astype(BF16)
        kdup_ref[1] = jnp.where(lane < 64, sw, kn).astype(BF16)
        v = v_ref[...]
        one = jnp.ones(v.shape, BF16)
        vaug_ref[0] = jnp.where(lane < 64, v, one)
        vaug_ref[1] = jnp.where(lane < 64, one, v)

    qf = q_ref[...].astype(F32)
    ms = _head_mean_sq(qf, bd)
    qn = qf * lax.rsqrt(ms + NORM_EPS) * gq_ref[...]
    cq = jnp.concatenate([cq_ref[...], cq_ref[...]], axis=1)
    sq = jnp.concatenate([sq_ref[...], sq_ref[...]], axis=1)
    qn = (_rope_lanes(qn, cq, sq, 16) * (HEAD_DIM ** -0.5 * LOG2E)).astype(BF16)
    tq = qn.shape[0]
    lane = lax.broadcasted_iota(jnp.int32, (tq, LANES), 1)
    zero = jnp.zeros((tq, LANES), BF16)
    def scores(h):
        qh = qn[:, h * LANES:(h + 1) * LANES]
        q2 = jnp.concatenate([jnp.where(lane < 64, qh, zero), jnp.where(lane < 64, zero, qh)], axis=0)
        return _dot_nt(q2, kdup_ref[h])

    s_next = scores(0)
    for h in range(2):
        s = s_next
        if h + 1 < 2:
            s_next = scores(h + 1)
        m = jnp.max(s, axis=-1, keepdims=True)
        p = jnp.exp2(s - m).astype(BF16)
        o2 = _dot(p, vaug_ref[h])
        o2 = o2 / pltpu.roll(o2, 64, 1)
        top, bot = o2[:tq], o2[tq:]
        if h == 0:
            oh = jnp.where(lane < 64, top, pltpu.roll(bot, 64, 1))
        else:
            oh = jnp.where(lane < 64, pltpu.roll(top, 64, 1), bot)
        o_ref[:, h * LANES:(h + 1) * LANES] = oh.astype(BF16)


def _gqa(main, tabs, gq, gk, bd, b, l):
    tq = min(256, l)
    nq = l // tq
    cq, sq = tabs
    full = lambda bi, i: (0, 0)
    return pl.pallas_call(
        _gqa_body,
        grid=(b, nq),
        in_specs=[pl.BlockSpec((tq, 256), lambda bi, i: (bi * nq + i, COL_GQ // 256)),
                  pl.BlockSpec((l, LANES), lambda bi, i: (bi, COL_GK // LANES)),
                  pl.BlockSpec((l, LANES), lambda bi, i: (bi, COL_GV // LANES)),
                  pl.BlockSpec((tq, LANES), lambda bi, i: (i, 0)),
                  pl.BlockSpec((tq, LANES), lambda bi, i: (i, 0)),
                  pl.BlockSpec((l, LANES), full),
                  pl.BlockSpec((l, LANES), full),
                  pl.BlockSpec((1, 256), full),
                  pl.BlockSpec((1, LANES), full),
                  pl.BlockSpec((256, 256), full)],
        out_specs=pl.BlockSpec((tq, 256), lambda bi, i: (bi * nq + i, 0)),
        out_shape=jax.ShapeDtypeStruct((b * l, 256), BF16),
        scratch_shapes=[pltpu.VMEM((2, l, LANES), BF16), pltpu.VMEM((2, l, LANES), BF16)],
        compiler_params=_cparams(("parallel", "arbitrary")),
        name="gqa",
    )(main, main, main, cq, sq, cq, sq, gq, gk, bd)


def _diff_body(lam_ref, q_ref, k_ref, v_ref, cq_ref, sq_ref, ck_ref, sk_ref, g_ref, bd_ref,
               o_ref, kr_ref, vaug_ref):
    i = pl.program_id(1)

    @pl.when(i == 0)
    def _():
        kr_ref[...] = _rope_lanes(k_ref[...].astype(F32), ck_ref[...], sk_ref[...], 4).astype(BF16)
        v = v_ref[...]
        vlane = lax.broadcasted_iota(jnp.int32, v.shape, 1)
        one = jnp.ones(v.shape, BF16)
        for h in range(4):
            vaug_ref[h] = jnp.where(vlane // 64 == h, v, one)

    lam = lam_ref[0]
    qr = _rope_lanes(q_ref[...].astype(F32), cq_ref[...], sq_ref[...], 4)
    qr = (qr * (DIFF_SUB_DIM ** -0.5 * LOG2E)).astype(BF16)
    tq = qr.shape[0]
    lane = lax.broadcasted_iota(jnp.int32, (tq, 256), 1)
    zero = jnp.zeros((tq, 256), BF16)
    kr = kr_ref[...]
    acc = jnp.zeros((tq, 256), F32)

    def scores(h):
        q2 = jnp.concatenate([jnp.where(lane // 32 == 2 * h, qr, zero),
                              jnp.where(lane // 32 == 2 * h + 1, qr, zero)], axis=0)
        return _dot_nt(q2, kr)

    s_next = scores(0)
    for h in range(4):
        s = s_next
        if h + 1 < 4:
            s_next = scores(h + 1)
        m = jnp.max(s, axis=-1, keepdims=True)
        e = jnp.exp2(s - m).astype(BF16)
        o2 = _dot(e, vaug_ref[h])
        o2 = o2 / pltpu.roll(o2, 64, 1)
        o = o2[:tq] - lam * o2[tq:]
        acc = jnp.where(lane // 64 == h, o, acc)
    ms = _head_mean_sq(acc, bd_ref[...])
    o_ref[...] = (acc * lax.rsqrt(ms + NORM_EPS) * g_ref[...]).astype(BF16)


def _diff(main, lam, tabs, g, bd, b, l):
    tq = min(256, l)
    nq = l // tq
    c, s = tabs
    full = lambda bi, i, *_: (0, 0)
    once = pl.Buffered(1)
    grid_spec = pltpu.PrefetchScalarGridSpec(
        num_scalar_prefetch=1,
        grid=(b, nq),
        in_specs=[pl.BlockSpec((tq, 256), lambda bi, i, *_: (bi * nq + i, COL_DQ // 256)),
                  pl.BlockSpec((l, 256), lambda bi, i, *_: (bi, COL_DK // 256)),
                  pl.BlockSpec((l, 256), lambda bi, i, *_: (bi, COL_DV // 256)),
                  pl.BlockSpec((tq, 256), lambda bi, i, *_: (i, 0)),
                  pl.BlockSpec((tq, 256), lambda bi, i, *_: (i, 0)),
                  pl.BlockSpec((l, 256), full, pipeline_mode=once),
                  pl.BlockSpec((l, 256), full, pipeline_mode=once),
                  pl.BlockSpec((1, 256), full),
                  pl.BlockSpec((256, 256), full)],
        out_specs=pl.BlockSpec((tq, 256), lambda bi, i, *_: (bi * nq + i, 0)),
        scratch_shapes=[pltpu.VMEM((l, 256), BF16), pltpu.VMEM((4, l, 256), BF16)],
    )
    return pl.pallas_call(
        _diff_body,
        grid_spec=grid_spec,
        out_shape=jax.ShapeDtypeStruct((b * l, 256), BF16),
        compiler_params=_cparams(("parallel", "arbitrary")),
        name="diffattn",
    )(lam, main, main, main, c, s, c, s, g, bd)


def _layer_norm_rows(y, g, b):
    mu = jnp.mean(y, axis=-1, keepdims=True)
    d = y - mu
    var = jnp.mean(d * d, axis=-1, keepdims=True)
    return d * lax.rsqrt(var + LN_EPS) * g + b


def _outproj_body(alpha, a_ref, b_ref, c_ref, d_ref, w_ref, x_ref, g_ref, beta_ref, rw_ref, rb_ref,
                  x1_ref, x1b_ref, idx_ref, gate_ref):
    nch = a_ref.shape[0]
    a = jnp.zeros((nch * S5_CHUNK, 256), F32)
    r = lax.broadcasted_iota(jnp.int32, (nch * S5_CHUNK, nch), 0)
    n = lax.broadcasted_iota(jnp.int32, (nch * S5_CHUNK, nch), 1)
    for t in range(S5_CHUNK):
        a = a + _dot((r == S5_CHUNK * n + t).astype(BF16), a_ref[:, t * 256:(t + 1) * 256])
    mixed = (_dot(a.astype(BF16), w_ref[0:256, :]) + _dot(b_ref[...], w_ref[256:512, :])
             + _dot(c_ref[...], w_ref[512:768, :]) + _dot(d_ref[...], w_ref[768:1024, :]))
    x1 = _layer_norm_rows(alpha * x_ref[...] + mixed, g_ref[...], beta_ref[...])
    x1_ref[...] = x1
    x1b = x1.astype(BF16)
    x1b_ref[...] = x1b
    logits = _dot(x1b, rw_ref[...]) + rb_ref[...]
    lane = lax.broadcasted_iota(jnp.int32, logits.shape, 1)
    vals, idxs = [], []
    for _ in range(TOP_K):
        m = jnp.max(logits, axis=-1, keepdims=True)
        ix = jnp.min(jnp.where(logits == m, lane, LANES), axis=-1, keepdims=True)
        vals.append(m)
        idxs.append(ix)
        logits = jnp.where(lane == ix, -jnp.inf, logits)
    es = [jnp.exp(vk - vals[0]) for vk in vals]
    tot = es[0] + es[1] + es[2] + es[3]
    idx_out = jnp.zeros(logits.shape, F32)
    gate_out = jnp.zeros(logits.shape, F32)
    for k in range(TOP_K):
        idx_out = jnp.where(lane == k, idxs[k].astype(F32), idx_out)
        gate_out = jnp.where(lane == k, es[k] / tot, gate_out)
    idx_ref[...] = jnp.transpose(idx_out)[:8, :].astype(jnp.int32)
    gate_ref[...] = gate_out


def _outproj(pieces, w, x, g, beta, rw, rb, alpha):
    t = x.shape[0]
    tm = min(512, t)
    row = lambda i: (i, 0)
    full = lambda i: (0, 0)
    return pl.pallas_call(
        functools.partial(_outproj_body, alpha),
        grid=(t // tm,),
        in_specs=[pl.BlockSpec((tm // S5_CHUNK, S5_CHUNK * 256), row)] + [pl.BlockSpec((tm, 256), row)] * 3 + [
            pl.BlockSpec((D_MODEL, D_MODEL), full),
            pl.BlockSpec((tm, D_MODEL), row),
            pl.BlockSpec((1, D_MODEL), full),
            pl.BlockSpec((1, D_MODEL), full),
            pl.BlockSpec((D_MODEL, LANES), full),
            pl.BlockSpec((1, LANES), full)],
        out_specs=[pl.BlockSpec((tm, D_MODEL), row), pl.BlockSpec((tm, D_MODEL), row),
                   pl.BlockSpec((8, tm), lambda i: (0, i)), pl.BlockSpec((tm, LANES), row)],
        out_shape=[jax.ShapeDtypeStruct((t, D_MODEL), F32), jax.ShapeDtypeStruct((t, D_MODEL), BF16),
                   jax.ShapeDtypeStruct((8, t), jnp.int32), jax.ShapeDtypeStruct((t, LANES), F32)],
        compiler_params=_cparams(("parallel",)),
        name="outproj_ln_router",
    )(*pieces, w, x, g, beta, rw, rb)


MOE_BLOCK = 512
MOE_RANGES = 8


def _ffn_body(be_ref, nb_ref, x_ref, wg_ref, wl_ref, bg_ref, bl_ref, w2_ref, b2_ref, *rest):
    o_ref = rest[-1]
    i = pl.program_id(0)

    @pl.when(i < nb_ref[0])
    def _():
        x = x_ref[...]
        glu = jnp.minimum(_dot(x, wg_ref[...]) + bg_ref[...], SWIGLU_LIMIT)
        lin = jnp.clip(_dot(x, wl_ref[...]) + bl_ref[...], -SWIGLU_LIMIT, SWIGLU_LIMIT)
        act = (glu * jax.nn.sigmoid(SWIGLU_ALPHA * glu) * (lin + 1.0)).astype(BF16)
        o_ref[...] = (_dot(act, w2_ref[...]) + b2_ref[...]).astype(o_ref.dtype)

    @pl.when(i >= nb_ref[0])
    def _():
        o_ref[...] = jnp.zeros(o_ref.shape, o_ref.dtype)


def _ffn(xb, block_expert, n_used, wg, wl, bg, bl, w2, b2, yb_prev, n_slots, block_offset):
    n_blocks = xb.shape[0] // MOE_BLOCK
    de = wg.shape[2]
    expert = lambda i, be, nb: (be[i], 0, 0)
    in_specs = [pl.BlockSpec((MOE_BLOCK, D_MODEL), lambda i, be, nb: (i, 0)),
                pl.BlockSpec((None, D_MODEL, de), expert),
                pl.BlockSpec((None, D_MODEL, de), expert),
                pl.BlockSpec((None, 1, de), expert),
                pl.BlockSpec((None, 1, de), expert),
                pl.BlockSpec((None, de, D_MODEL), expert),
                pl.BlockSpec((None, 1, D_MODEL), expert)]
    args = [block_expert, n_used, xb, wg, wl, bg, bl, w2, b2]
    aliases = {}
    if yb_prev is not None:
        in_specs.append(pl.BlockSpec(memory_space=pl.ANY))
        aliases = {len(args): 0}
        args.append(yb_prev)
    grid_spec = pltpu.PrefetchScalarGridSpec(
        num_scalar_prefetch=2,
        grid=(n_blocks,),
        in_specs=in_specs,
        out_specs=pl.BlockSpec((MOE_BLOCK, D_MODEL), lambda i, be, nb: (i + block_offset, 0)),
    )
    return pl.pallas_call(
        _ffn_body,
        grid_spec=grid_spec,
        out_shape=jax.ShapeDtypeStruct((n_slots, D_MODEL), BF16),
        input_output_aliases=aliases,
        compiler_params=_cparams(("arbitrary",)),
        name="moe_ffn",
    )(*args)


def _w1_prep_body(w_ref, pg_ref, pl_ref, g_ref, l_ref):
    w = w_ref[...].astype(BF16)
    g_ref[...] = _dot(w, pg_ref[...]).astype(BF16)
    l_ref[...] = _dot(w, pl_ref[...]).astype(BF16)


def _w1_prep(w1):
    ne, dm, de2 = w1.shape
    tn = 512
    r = np.arange(tn)[:, None]
    c = np.arange(tn // 2)[None, :]
    sel_g = jnp.asarray(r == 2 * c, BF16)
    sel_l = jnp.asarray(r == 2 * c + 1, BF16)
    full = lambda e, j: (0, 0)
    out = jax.ShapeDtypeStruct((ne, dm, de2 // 2), BF16)
    return pl.pallas_call(
        _w1_prep_body,
        grid=(ne, de2 // tn),
        in_specs=[pl.BlockSpec((None, dm, tn), lambda e, j: (e, 0, j)),
                  pl.BlockSpec((tn, tn // 2), full), pl.BlockSpec((tn, tn // 2), full)],
        out_specs=[pl.BlockSpec((None, dm, tn // 2), lambda e, j: (e, 0, j))] * 2,
        out_shape=[out, out],
        compiler_params=_cparams(("parallel", "parallel")),
        name="w1_prep",
    )(w1, sel_g, sel_l)


def _ln2_body(alpha, x_ref, f_ref, g_ref, b_ref, o_ref):
    o_ref[...] = _layer_norm_rows(alpha * x_ref[...] + f_ref[...], g_ref[...], b_ref[...])


def _ln2(x, f, g, b, alpha):
    t = x.shape[0]
    tm = min(512, t)
    row = lambda i: (i, 0)
    full = lambda i: (0, 0)
    return pl.pallas_call(
        functools.partial(_ln2_body, alpha),
        grid=(t // tm,),
        in_specs=[pl.BlockSpec((tm, D_MODEL), row), pl.BlockSpec((tm, D_MODEL), row),
                  pl.BlockSpec((1, D_MODEL), full), pl.BlockSpec((1, D_MODEL), full)],
        out_specs=pl.BlockSpec((tm, D_MODEL), row),
        out_shape=jax.ShapeDtypeStruct((t, D_MODEL), F32),
        compiler_params=_cparams(("parallel",)),
        name="ln2",
    )(x, f, g, b)


S5_CHUNK = 16


S5_SEL_PAD = (SSM_GROUPS - 1) * SSM_GROUP_CH
S5_ROWS = 256


def _s5_sel():
    c = np.arange(256)
    target = (c // 16) * 256 + c % 16
    r = np.arange(S5_SEL_PAD + 4096)[:, None] - S5_SEL_PAD
    return jnp.asarray(r == target[None, :], BF16)


def _s5_in_body(x_ref, sel_ref, t_ref, w_ref, y_ref, h_ref):
    xcat = x_ref[...]
    us = []
    for g in range(SSM_GROUPS):
        start = S5_SEL_PAD - SSM_GROUP_CH * g
        u = _dot(xcat, sel_ref[start:start + 4096, :]).astype(BF16)
        y_ref[g] = _dot(u, t_ref[g]).astype(BF16)
        us.append(u)
    for i in range(SSM_GROUPS // 2):
        hp = _dot(jnp.concatenate([us[2 * i], us[2 * i + 1]], axis=1), w_ref[i])
        for c in range(4):
            h_ref[c, :, i * LANES:(i + 1) * LANES] = hp[:, c * LANES:(c + 1) * LANES]


def _s5_in(ux, sel, sp):
    t16 = ux.shape[0]
    rb = min(S5_ROWS, t16)
    full2 = lambda i: (0, 0)
    full3 = lambda i: (0, 0, 0)
    return pl.pallas_call(
        _s5_in_body,
        grid=(t16 // rb,),
        in_specs=[pl.BlockSpec((rb, S5_CHUNK * 256), lambda i: (i, 0)),
                  pl.BlockSpec(sel.shape, full2),
                  pl.BlockSpec((SSM_GROUPS, 256, 256), full3),
                  pl.BlockSpec((SSM_GROUPS // 2, 512, 512), full3)],
        out_specs=[pl.BlockSpec((SSM_GROUPS, rb, 256), lambda i: (0, i, 0)),
                   pl.BlockSpec((4, rb, 1024), lambda i: (0, i, 0))],
        out_shape=[jax.ShapeDtypeStruct((SSM_GROUPS, t16, 256), BF16),
                   jax.ShapeDtypeStruct((4, t16, 1024), F32)],
        compiler_params=_cparams(("parallel",)),
        name="s5_in",
    )(ux, sel, sp['t'], sp['w'])


def _s5_scan_body(n1, h_ref, lam_ref, e_ref):
    lam = lam_ref[...]
    zero = jnp.zeros((1, 1024), F32)

    def sweep(base, reverse):
        lr, li = lam[base:base + 1], lam[base + 1:base + 2]

        def tile(k, carry):
            er, ei = carry
            r0 = pl.multiple_of(((n1 // 8 - 1 - k) if reverse else k) * 8, 8)
            hr = h_ref[base, pl.ds(r0, 8), :]
            hi = h_ref[base + 1, pl.ds(r0, 8), :]
            outs_r, outs_i = [None] * 8, [None] * 8
            for j in (range(7, -1, -1) if reverse else range(8)):
                outs_r[j], outs_i[j] = er, ei
                er, ei = (lr * er - li * ei + hr[j:j + 1], lr * ei + li * er + hi[j:j + 1])
            e_ref[base, pl.ds(r0, 8), :] = jnp.concatenate(outs_r, axis=0)
            e_ref[base + 1, pl.ds(r0, 8), :] = jnp.concatenate(outs_i, axis=0)
            return er, ei

        lax.fori_loop(0, n1 // 8, tile, (zero, zero))

    sweep(0, False)
    sweep(2, True)


def _s5_scan(h, lam, b, n1):
    return pl.pallas_call(
        functools.partial(_s5_scan_body, n1),
        grid=(b,),
        in_specs=[pl.BlockSpec((4, n1, 1024), lambda i: (0, i, 0)),
                  pl.BlockSpec((8, 1024), lambda i: (0, 0))],
        out_specs=pl.BlockSpec((4, n1, 1024), lambda i: (0, i, 0)),
        out_shape=jax.ShapeDtypeStruct(h.shape, F32),
        compiler_params=_cparams(("parallel",)),
        name="s5_scan",
    )(h, lam)


def _s5_fin_body(y_ref, e_ref, sel_ref, v_ref, w_ref, b_ref, o_ref):
    ys = []
    for i in range(SSM_GROUPS // 2):
        ep = jnp.concatenate([e_ref[c, :, i * LANES:(i + 1) * LANES] for c in range(4)], axis=1)
        yi = _dot(ep.astype(BF16), v_ref[i])
        ys.append((y_ref[2 * i].astype(F32) + yi[:, :256]).astype(BF16))
        ys.append((y_ref[2 * i + 1].astype(F32) + yi[:, 256:]).astype(BF16))
    ycat = jnp.concatenate(ys, axis=1)
    for t in range(S5_CHUNK):
        start = S5_SEL_PAD - SSM_GROUP_CH * t
        y = jax.nn.gelu(_dot(ycat, sel_ref[start:start + 4096, :]))
        z = _dot(y.astype(BF16), w_ref[...]) + b_ref[...]
        o_ref[:, t * 256:(t + 1) * 256] = (y * jax.nn.sigmoid(z)).astype(BF16)


def _s5_fin(y, e, sel, sp, w, bias):
    t16 = y.shape[1]
    rb = min(S5_ROWS, t16)
    full2 = lambda i: (0, 0)
    full3 = lambda i: (0, 0, 0)
    return pl.pallas_call(
        _s5_fin_body,
        grid=(t16 // rb,),
        in_specs=[pl.BlockSpec((SSM_GROUPS, rb, 256), lambda i: (0, i, 0)),
                  pl.BlockSpec((4, rb, 1024), lambda i: (0, i, 0)),
                  pl.BlockSpec(sel.shape, full2),
                  pl.BlockSpec((SSM_GROUPS // 2, 512, 512), full3),
                  pl.BlockSpec((256, 256), full2),
                  pl.BlockSpec((1, 256), full2)],
        out_specs=pl.BlockSpec((rb, S5_CHUNK * 256), lambda i: (i, 0)),
        out_shape=jax.ShapeDtypeStruct((t16, S5_CHUNK * 256), BF16),
        compiler_params=_cparams(("parallel",)),
        name="s5_fin",
    )(y, e, sel, sp['v'], w, bias)


def _s5(ux, q, sel, b, l):
    y, h = _s5_in(ux, sel, q['s5'])
    e = _s5_scan(h, q['s5']['lam'], b, l // S5_CHUNK)
    return _s5_fin(y, e, sel, q['s5'], q['ssm_glu_w'], q['ssm_glu_b'])


def _s5_prep(p):
    hp = lax.Precision.HIGHEST
    c = S5_CHUNK
    tau = jnp.arange(c + 1, dtype=F32)[:, None, None]
    ks, ws, vs, lams = [], [], [], []
    for d in (0, 1):
        lam_re = p['ssm_a_re'][d]
        lam_im = p['ssm_a_im'][d]
        dt = jnp.exp(p['ssm_log_dt'][d])[:, None]
        mag = jnp.exp(lam_re * dt)
        abar_re = mag * jnp.cos(lam_im * dt)
        abar_im = mag * jnp.sin(lam_im * dt)
        den = lam_re * lam_re + lam_im * lam_im
        coef_re = ((abar_re - 1.0) * lam_re + abar_im * lam_im) / den
        coef_im = (abar_im * lam_re - (abar_re - 1.0) * lam_im) / den
        br, bi = p['ssm_b_re'][d], p['ssm_b_im'][d]
        bbar_re = coef_re[..., None] * br - coef_im[..., None] * bi
        bbar_im = coef_re[..., None] * bi + coef_im[..., None] * br
        cr, ci = p['ssm_c_re'][d], p['ssm_c_im'][d]
        pm = jnp.exp(tau * (lam_re * dt)[None])
        pr = pm * jnp.cos(tau * (lam_im * dt)[None])
        pi = pm * jnp.sin(tau * (lam_im * dt)[None])
        m_re = cr[None] * pr[:, :, None, :] - ci[None] * pi[:, :, None, :]
        m_im = cr[None] * pi[:, :, None, :] + ci[None] * pr[:, :, None, :]
        k = (jnp.einsum('tghp,gpk->gtkh', m_re[:c], bbar_re, precision=hp)
             - jnp.einsum('tghp,gpk->gtkh', m_im[:c], bbar_im, precision=hp))
        ks.append(k)
        pw = jnp.arange(c - 1, -1, -1) if d == 0 else jnp.arange(c)
        w_re = pr[pw][:, :, None, :] * jnp.swapaxes(bbar_re, 1, 2)[None] - pi[pw][:, :, None, :] * jnp.swapaxes(bbar_im, 1, 2)[None]
        w_im = pr[pw][:, :, None, :] * jnp.swapaxes(bbar_im, 1, 2)[None] + pi[pw][:, :, None, :] * jnp.swapaxes(bbar_re, 1, 2)[None]
        ws.append((jnp.transpose(w_re, (1, 0, 2, 3)).reshape(SSM_GROUPS, 256, SSM_STATE),
                   jnp.transpose(w_im, (1, 0, 2, 3)).reshape(SSM_GROUPS, 256, SSM_STATE)))
        po = jnp.arange(1, c + 1) if d == 0 else jnp.arange(c, 0, -1)
        v_re = m_re[po]
        v_im = m_im[po]
        vs.append((jnp.transpose(v_re, (1, 3, 0, 2)).reshape(SSM_GROUPS, SSM_STATE, 256),
                   jnp.transpose(-v_im, (1, 3, 0, 2)).reshape(SSM_GROUPS, SSM_STATE, 256)))
        lams.append((pr[c], pi[c]))
    j = jnp.arange(c)[:, None]
    t = jnp.arange(c)[None, :]
    k0 = ks[0][:, jnp.clip(t - j, 0, c - 1)] * (t >= j)[None, :, :, None, None]
    k1 = ks[1][:, jnp.clip(j - t, 0, c - 1)] * (j >= t)[None, :, :, None, None]
    tm = jnp.transpose(k0 + k1, (0, 1, 3, 2, 4)).reshape(SSM_GROUPS, 256, 256)
    dd = jnp.tile(p['ssm_d'], (1, c))
    tm = tm + jnp.eye(256, dtype=F32)[None] * dd[:, None, :]
    npair = SSM_GROUPS // 2

    def pair_diag(x):
        g, r, cc = x.shape
        x = x.reshape(npair, 2, r, cc)
        z = jnp.zeros((npair, 2, r, 2, cc), F32)
        z = z.at[:, 0, :, 0, :].set(x[:, 0]).at[:, 1, :, 1, :].set(x[:, 1])
        return z.reshape(npair, 2 * r, 2 * cc)

    w = jnp.concatenate([pair_diag(ws[0][0]), pair_diag(ws[0][1]), pair_diag(ws[1][0]), pair_diag(ws[1][1])], axis=2)
    v = jnp.concatenate([pair_diag(vs[0][0]), pair_diag(vs[0][1]), pair_diag(vs[1][0]), pair_diag(vs[1][1])], axis=1)
    lam = jnp.stack([lams[0][0], lams[0][1], lams[1][0], lams[1][1]], axis=0)
    lam = lam.reshape(4, SSM_GROUPS * SSM_STATE)
    lam = jnp.concatenate([lam, jnp.zeros_like(lam)], axis=0)
    return {'t': tm.astype(BF16), 'w': w.astype(BF16), 'v': v.astype(BF16), 'lam': lam}


GDN_HALO = 16


def _gdn_prep_body(nt, q_ref, qp_ref, qn_ref, k_ref, kp_ref, kn_ref, v_ref, vp_ref, vn_ref,
                   gate_ref, cw_ref, al_ref, dtb_ref, bd_ref, qo_ref, ko_ref, vo_ref, gb_ref):
    i = pl.program_id(1)
    cw = cw_ref[...]
    tl = q_ref.shape[0]

    def conv(cur_ref, prev_ref, next_ref, col):
        prev = jnp.where(i > 0, prev_ref[...].astype(F32), 0.0)
        nxt = jnp.where(i < nt - 1, next_ref[...].astype(F32), 0.0)
        xe = jnp.concatenate([prev, cur_ref[...].astype(F32), nxt], axis=0)
        n = xe.shape[0]
        acc = jnp.zeros((tl, 256), F32)
        for tap in range(5):
            s = tap - 2
            sh = xe if s == 0 else pltpu.roll(xe, (-s) % n, 0)
            acc = acc + sh[GDN_HALO:GDN_HALO + tl] * cw[tap:tap + 1, col:col + 256]
        return acc * jax.nn.sigmoid(acc)

    bd = bd_ref[...]

    def l2n(x):
        return x * lax.rsqrt(_head_mean_sq(x, bd) * HEAD_DIM + NORM_EPS)

    qo_ref[...] = l2n(conv(q_ref, qp_ref, qn_ref, 0)).astype(BF16)
    ko_ref[...] = l2n(conv(k_ref, kp_ref, kn_ref, 256)).astype(BF16)
    vo_ref[...] = conv(v_ref, vp_ref, vn_ref, 512).astype(BF16)
    gt = gate_ref[...]
    lane = lax.broadcasted_iota(jnp.int32, gt.shape, 1)
    x = gt + dtb_ref[...]
    softplus = jnp.maximum(x, 0.0) + jnp.log(1.0 + jnp.exp(-jnp.abs(x)))
    gb_ref[...] = jnp.where(lane < 8, -al_ref[...] * softplus, jnp.where(lane < 16, jax.nn.sigmoid(gt), 0.0))


def _gdn_prep(main, gate, q, bd, b, l):
    tl = min(256, l)
    nt = l // tl
    hb = tl // GDN_HALO
    nh = l // GDN_HALO

    def cur(col):
        return pl.BlockSpec((tl, 256), lambda bi, i: (bi * nt + i, col // 256))

    def prev(col):
        return pl.BlockSpec((GDN_HALO, 256), lambda bi, i: (bi * nh + jnp.maximum(i * hb - 1, 0), col // 256))

    def nxt(col):
        return pl.BlockSpec((GDN_HALO, 256), lambda bi, i: (bi * nh + jnp.minimum((i + 1) * hb, nh - 1), col // 256))

    full = lambda bi, i: (0, 0)
    row = lambda bi, i: (bi * nt + i, 0)
    specs = []
    for col in (COL_CQ, COL_CK, COL_CV):
        specs += [cur(col), prev(col), nxt(col)]
    specs += [pl.BlockSpec((tl, GATE_W), row), pl.BlockSpec((8, 768), full), pl.BlockSpec((1, LANES), full),
              pl.BlockSpec((1, LANES), full), pl.BlockSpec((256, 256), full)]
    return pl.pallas_call(
        functools.partial(_gdn_prep_body, nt),
        grid=(b, nt),
        in_specs=specs,
        out_specs=[pl.BlockSpec((tl, 256), row)] * 3 + [pl.BlockSpec((tl, GATE_W), row)],
        out_shape=[jax.ShapeDtypeStruct((b * l, 256), BF16)] * 3 + [jax.ShapeDtypeStruct((b * l, GATE_W), F32)],
        compiler_params=_cparams(("parallel", "parallel")),
        name="gdn_prep",
    )(*([main] * 9), gate, q['gdn_conv_w'], q['gdn_al'], q['gdn_dtb'], bd)


def _gdn_chunk_step(chains):
    c = GDN_CHUNK
    r256 = lax.broadcasted_iota(jnp.int32, (256, 256), 0)
    c256 = lax.broadcasted_iota(jnp.int32, (256, 256), 1)
    blockmask = (r256 // c) == (c256 // c)
    rl = lax.broadcasted_iota(jnp.int32, (LANES, 256), 0)
    cl = lax.broadcasted_iota(jnp.int32, (LANES, 256), 1)
    ri = lax.broadcasted_iota(jnp.int32, (c, c), 0)
    ci = lax.broadcasted_iota(jnp.int32, (c, c), 1)
    i_s = lax.broadcasted_iota(jnp.int32, (c, 256), 0)
    j_s = lax.broadcasted_iota(jnp.int32, (c, 256), 1) % c
    ones = jnp.ones((c, c), BF16)
    eye = (i_s == j_s).astype(F32)
    scale = HEAD_DIM ** -0.5

    def bdv(y):
        return jnp.where(blockmask, jnp.concatenate([y, y, y, y], axis=0), 0.0).astype(BF16)

    def hilo(x):
        hi = x.astype(BF16)
        return hi, (x - hi.astype(F32)).astype(BF16)

    st = []
    for dirn, qb, kb16, vb, gb, s_ref in chains:
        rev = dirn == 1
        e_g = (rl == dirn * 4 + cl // c).astype(BF16)
        e_b = (rl == 8 + dirn * 4 + cl // c).astype(BF16)
        ghi, glo = hilo(gb)
        st.append(dict(rev=rev, q=qb.astype(F32), k=kb16.astype(F32), v=vb.astype(F32), s_ref=s_ref,
                       g=_dot(ghi, e_g) + _dot(glo, e_g), beta=_dot(ghi, e_b) + _dot(glo, e_b),
                       tri=((ci >= ri) if rev else (ci <= ri)).astype(BF16),
                       allowed=(j_s >= i_s) if rev else (j_s <= i_s),
                       strict=(j_s > i_s) if rev else (j_s < i_s)))
    for d in st:
        ghi, glo = hilo(d['g'])
        d['gc'] = _dot(d['tri'], ghi) + _dot(d['tri'], glo)
    for d in st:
        zhi, zlo = hilo(jnp.where(i_s == j_s, d['gc'], 0.0))
        d['gct'] = _dot(ones, zhi) + _dot(ones, zlo)
    for d in st:
        d['decay'] = jnp.exp(jnp.where(d['allowed'], d['gc'] - d['gct'], -jnp.inf))
        d['eg'] = jnp.exp(d['gc'])
        d['kbeta'] = d['k'] * d['beta']
    for d in st:
        kk_qk = _dot_nt(jnp.concatenate([d['kbeta'], d['q'] * scale], axis=0).astype(BF16), bdv(d['k']))
        d['p'] = -jnp.where(d['strict'], kk_qk[:c] * d['decay'], 0.0)
        d['intra'] = jnp.where(d['allowed'], kk_qk[c:] * d['decay'], 0.0)
        d['t'] = eye + d['p']
    for _ in range(5):
        for d in st:
            d['p'] = _dot(d['p'].astype(BF16), bdv(d['p']))
        for d in st:
            d['t'] = d['t'] + _dot(d['t'].astype(BF16), bdv(d['p']))
    for d in st:
        t16 = d['t'].astype(BF16)
        d['u'] = _dot(t16, bdv(d['v'] * d['beta']))
        d['w'] = _dot(t16, bdv(d['kbeta'] * d['eg']))
    for d in st:
        d['s'] = d['s_ref'][...]
        d['ws_qs'] = _dot(jnp.concatenate([d['w'], d['q'] * scale * d['eg']], axis=0).astype(BF16),
                          d['s'].astype(BF16))
    for d in st:
        d['v_new'] = d['u'] - d['ws_qs'][:c]
        d['o'] = d['ws_qs'][c:] + _dot(d['intra'].astype(BF16), bdv(d['v_new']))
    for d in st:
        last = 0 if d['rev'] else c - 1
        g_last = d['gc'][last:last + 1, :]
        kg = (d['k'] * jnp.exp(g_last - d['gc'])).astype(BF16)
        upd = lax.dot_general(kg, d['v_new'].astype(BF16), (((0,), (0,)), ((), ())), preferred_element_type=F32)
        d['s_ref'][...] = d['s'] * jnp.exp(g_last) + jnp.where(blockmask, upd, 0.0)
    return [d['o'] for d in st]


GDN_SEQS = 4


def _gdn_chunk_body(qf_ref, kf_ref, vf_ref, gf_ref, qb_ref, kb_ref, vb_ref, gb_ref, of_ref, ob_ref, s_ref):
    @pl.when(pl.program_id(1) == 0)
    def _():
        s_ref[...] = jnp.zeros(s_ref.shape, F32)

    chains = []
    for j in range(qf_ref.shape[0]):
        chains.append((0, qf_ref[j], kf_ref[j], vf_ref[j], gf_ref[j], s_ref.at[j, 0]))
        chains.append((1, qb_ref[j], kb_ref[j], vb_ref[j], gb_ref[j], s_ref.at[j, 1]))
    outs = _gdn_chunk_step(chains)
    for j in range(qf_ref.shape[0]):
        of_ref[j] = outs[2 * j].astype(BF16)
        ob_ref[j] = outs[2 * j + 1].astype(BF16)


def _gdn_chunks(qn, kn, vs, gb, b, l):
    c = GDN_CHUNK
    n = l // c
    nseq = GDN_SEQS if b % GDN_SEQS == 0 else 1
    fwd = lambda bi, i: (bi, i, 0)
    bwd = lambda bi, i: (bi, n - 1 - i, 0)
    blk = lambda m: pl.BlockSpec((nseq, c, 256), m)
    gblk = lambda m: pl.BlockSpec((nseq, c, GATE_W), m)
    qn, kn, vs = (a.reshape(b, l, 256) for a in (qn, kn, vs))
    gb = gb.reshape(b, l, GATE_W)
    of, ob = pl.pallas_call(
        _gdn_chunk_body,
        grid=(b // nseq, n),
        in_specs=[blk(fwd), blk(fwd), blk(fwd), gblk(fwd), blk(bwd), blk(bwd), blk(bwd), gblk(bwd)],
        out_specs=[blk(fwd), blk(bwd)],
        out_shape=[jax.ShapeDtypeStruct((b, l, 256), BF16)] * 2,
        scratch_shapes=[pltpu.VMEM((nseq, 2, 256, 256), F32)],
        compiler_params=_cparams(("parallel", "arbitrary")),
        name="gdn_chunks",
    )(qn, kn, vs, gb, qn, kn, vs, gb)
    return of.reshape(b * l, 256), ob.reshape(b * l, 256)


def _gdn_out_body(of_ref, ob_ref, z_ref, g_ref, bd_ref, o_ref):
    o = of_ref[...].astype(F32) + ob_ref[...].astype(F32)
    z = z_ref[...].astype(F32)
    ms = _head_mean_sq(o, bd_ref[...])
    o_ref[...] = (o * lax.rsqrt(ms + NORM_EPS) * g_ref[...] * (z * jax.nn.sigmoid(z))).astype(BF16)


def _gdn_out(of, ob, main, g, bd):
    t = of.shape[0]
    tm = min(1024, t)
    row = lambda i: (i, 0)
    full = lambda i: (0, 0)
    return pl.pallas_call(
        _gdn_out_body,
        grid=(t // tm,),
        in_specs=[pl.BlockSpec((tm, 256), row), pl.BlockSpec((tm, 256), row),
                  pl.BlockSpec((tm, 256), lambda i: (i, COL_CZ // 256)),
                  pl.BlockSpec((1, 256), full), pl.BlockSpec((256, 256), full)],
        out_specs=pl.BlockSpec((tm, 256), row),
        out_shape=jax.ShapeDtypeStruct((t, 256), BF16),
        compiler_params=_cparams(("parallel",)),
        name="gdn_out",
    )(of, ob, main, g, bd)


def _gdn(main, gate, q, bd, b, l):
    qn, kn, vs, gb = _gdn_prep(main, gate, q, bd, b, l)
    of, ob = _gdn_chunks(qn, kn, vs, gb, b, l)
    return _gdn_out(of, ob, main, q['gdn_o_norm'], bd)


def _moe(x1, x1b, top_idx, gates, p):
    t = x1.shape[0]
    m = t * TOP_K
    flat_e = top_idx.reshape(-1)
    order = jnp.argsort(flat_e, stable=True).astype(jnp.int32)
    inv = jnp.argsort(order).astype(jnp.int32)
    sorted_e = flat_e[order]
    experts = jnp.arange(N_EXPERTS, dtype=jnp.int32)
    counts = jnp.sum((top_idx[None] == experts[:, None, None]).astype(jnp.int32), axis=(1, 2))
    padded = (counts + MOE_BLOCK - 1) // MOE_BLOCK * MOE_BLOCK
    group_start = jnp.cumsum(counts) - counts
    padded_end = jnp.cumsum(padded)
    padded_start = padded_end - padded
    dest = (padded_start[sorted_e] + jnp.arange(m, dtype=jnp.int32) - group_start[sorted_e]).astype(jnp.int32)
    n_blocks = -(-m // MOE_BLOCK) + N_EXPERTS
    n_slots = n_blocks * MOE_BLOCK
    block_start = jnp.arange(n_blocks, dtype=jnp.int32) * MOE_BLOCK
    block_expert = jnp.minimum(jnp.sum((padded_end[None, :] <= block_start[:, None]).astype(jnp.int32), axis=1),
                               N_EXPERTS - 1).astype(jnp.int32)
    n_used = (padded_end[-1] // MOE_BLOCK).astype(jnp.int32).reshape(1)
    slot = jnp.arange(n_slots, dtype=jnp.int32)
    slot_e = jnp.repeat(block_expert, MOE_BLOCK)
    rank = slot - padded_start[slot_e]
    src = jnp.clip(group_start[slot_e] + rank, 0, m - 1)
    slot_token = jnp.where(rank < counts[slot_e], order[src] % t, 0).astype(jnp.int32)
    n_ranges = MOE_RANGES if n_blocks % MOE_RANGES == 0 else 1
    rblocks = n_blocks // n_ranges
    yb = None
    for r in range(n_ranges):
        b0 = r * rblocks
        xb = x1b[slot_token[b0 * MOE_BLOCK:(b0 + rblocks) * MOE_BLOCK]]
        yb = _ffn(xb, block_expert[b0:b0 + rblocks], jnp.clip(n_used - b0, 0, rblocks),
                  p['moe_wg'], p['moe_wl'], p['moe_bg'], p['moe_bl'], p['moe_w2'], p['moe_b2'],
                  yb, n_slots, b0)
    dest_a = dest[inv]
    out = jnp.zeros((t, D_MODEL), F32)
    for k in range(TOP_K):
        out = out + yb[dest_a[k * t:(k + 1) * t]].astype(F32) * gates[:, k:k + 1]
    return out


def _block_diag_mean(width, group):
    idx = np.arange(width)
    return jnp.asarray((idx[:, None] // group == idx[None, :] // group) / group, BF16)


def _axial_tables(l):
    rows = l // GRID_W
    row_pos = np.repeat(np.arange(rows), GRID_W).astype(np.float64)
    col_pos = np.tile(np.arange(GRID_W), rows).astype(np.float64)
    lane = np.arange(LANES)
    d = lane % HEAD_DIM
    e = d % 32
    f = e % 16
    inv = (AXIAL_THETA ** (-(np.arange(0, 32, 2, dtype=np.float32)) / 32)).astype(np.float32)
    pos = np.where((d // 32)[None, :] == 0, row_pos[:, None], col_pos[:, None]).astype(np.float32)
    ang = pos * inv[f][None, :]
    sign = np.where(e < 16, -1.0, 1.0)[None, :]
    return jnp.asarray(np.cos(ang), F32), jnp.asarray(np.sin(ang) * sign, F32)


def _diff_tables(l):
    lane = np.arange(256)
    e = lane % 32
    f = e % 4
    inv = (ROPE_THETA ** (-(np.arange(0, PARTIAL_ROPE_DIMS, 2, dtype=np.float32)) / PARTIAL_ROPE_DIMS)).astype(np.float32)
    ang = np.arange(l, dtype=np.float32)[:, None] * inv[f][None, :]
    roped = (e < PARTIAL_ROPE_DIMS)[None, :]
    sign = np.where(e < 4, -1.0, 1.0)[None, :]
    c = np.where(roped, np.cos(ang), 1.0)
    s = np.where(roped, np.sin(ang) * sign, 0.0)
    return jnp.asarray(c, F32), jnp.asarray(s, F32)


def _prep_layer(params, layer, w1_all):
    p = {name: arr[layer] for name, arr in params.items()}
    w_in = p['w_in']
    w_in = jnp.concatenate([w_in[:, :ORIG_CA], w_in[:, ORIG_CA + 16:], w_in[:, ORIG_CA:ORIG_CA + 16],
                            jnp.zeros((D_MODEL, GATE_W - 16), F32)], axis=1)
    q = dict(p)
    q['w_in'] = w_in.astype(BF16)
    q['w_out'] = p['w_out'].astype(BF16)
    q['router_w'] = jnp.concatenate([p['router_w'], jnp.zeros((D_MODEL, LANES - N_EXPERTS), F32)], axis=1).astype(BF16)
    q['router_b'] = jnp.concatenate([p['router_b'], jnp.full((LANES - N_EXPERTS,), -1e30, F32)])[None, :]
    q['moe_wg'] = w1_all[0][layer]
    q['moe_wl'] = w1_all[1][layer]
    q['moe_bg'] = p['moe_b1'][:, None, 0::2]
    q['moe_bl'] = p['moe_b1'][:, None, 1::2]
    q['moe_w2'] = p['moe_w2'].astype(BF16)
    q['moe_b2'] = p['moe_b2'][:, None, :]
    for name in ('ln1_g', 'ln1_b', 'ln2_g', 'ln2_b'):
        q[name] = p[name][None, :]
    q['gqa_q_norm'] = jnp.tile(p['gqa_q_norm'], 4)[None, :]
    q['gqa_k_norm'] = jnp.tile(p['gqa_k_norm'], 2)[None, :]
    lambda_init = 0.8 - 0.6 * math.exp(-0.3 * layer)
    q['diff_subln'] = (jnp.tile(p['diff_subln'], 4) * (1.0 - lambda_init))[None, :]
    lam = (jnp.exp(jnp.sum(p['diff_lambda_q1'] * p['diff_lambda_k1']))
           - jnp.exp(jnp.sum(p['diff_lambda_q2'] * p['diff_lambda_k2'])) + lambda_init)
    q['diff_lam'] = lam.reshape(1).astype(F32)
    q['s5'] = _s5_prep(p)
    q['ssm_glu_w'] = p['ssm_glu_w'].astype(BF16)
    q['ssm_glu_b'] = p['ssm_glu_b'][None, :]
    q['gdn_conv_w'] = jnp.concatenate([p['gdn_conv_w'], jnp.zeros((3, 768), F32)], axis=0)
    pad8 = lambda x: jnp.concatenate([x.reshape(-1), jnp.zeros((LANES - 8,), F32)])[None, :]
    q['gdn_al'] = pad8(jnp.exp(p['gdn_a_log']))
    q['gdn_dtb'] = pad8(p['gdn_dt_bias'])
    q['gdn_o_norm'] = jnp.tile(p['gdn_o_norm'], 4)[None, :]
    return q


def _mixers(main, gate, ux, q, tabs, b, l):
    a_out = _s5(ux, q, tabs['s5sel'], b, l)
    b_out = _gqa(main, tabs['axial'], q['gqa_q_norm'], q['gqa_k_norm'], tabs['bd64'], b, l)
    c_out = _gdn(main, gate, q, tabs['bd64'], b, l)
    d_out = _diff(main, q['diff_lam'], tabs['diff'], q['diff_subln'], tabs['bd64'], b, l)
    return a_out, b_out, c_out, d_out


def _layer(x, q, tabs, b, l, alpha):
    main, gate, ux = _inproj(x, q['w_in'])
    pieces = _mixers(main, gate, ux, q, tabs, b, l)
    x1, x1b, idx, gates = _outproj(pieces, q['w_out'], x, q['ln1_g'], q['ln1_b'],
                                   q['router_w'], q['router_b'], alpha)
    ffn = _moe(x1, x1b, idx[:TOP_K], gates, q)
    return _ln2(x1, ffn, q['ln2_g'], q['ln2_b'], alpha)


def _trunk(x, layers, alpha):
    b, l, _ = x.shape
    tabs = {'axial': _axial_tables(l), 'diff': _diff_tables(l), 'bd64': _block_diag_mean(256, HEAD_DIM),
            's5sel': _s5_sel()}
    h = x.reshape(b * l, D_MODEL)
    for q in layers:
        h = _layer(h, q, tabs, b, l, alpha)
    return h.reshape(b, l, D_MODEL)


def kernel(x_prompt, x_sample, w_in, w_out, ssm_a_re, ssm_a_im, ssm_log_dt, ssm_b_re, ssm_b_im, ssm_c_re, ssm_c_im, ssm_d, ssm_glu_w, ssm_glu_b, gqa_q_norm, gqa_k_norm, gdn_conv_w, gdn_a_log, gdn_dt_bias, gdn_o_norm, diff_lambda_q1, diff_lambda_k1, diff_lambda_q2, diff_lambda_k2, diff_subln, router_w, router_b, moe_w1, moe_b1, moe_w2, moe_b2, ln1_g, ln1_b, ln2_g, ln2_b):
    params = {
        'w_in': w_in, 'w_out': w_out,
        'ssm_a_re': ssm_a_re, 'ssm_a_im': ssm_a_im, 'ssm_log_dt': ssm_log_dt,
        'ssm_b_re': ssm_b_re, 'ssm_b_im': ssm_b_im, 'ssm_c_re': ssm_c_re, 'ssm_c_im': ssm_c_im,
        'ssm_d': ssm_d, 'ssm_glu_w': ssm_glu_w, 'ssm_glu_b': ssm_glu_b,
        'gqa_q_norm': gqa_q_norm, 'gqa_k_norm': gqa_k_norm,
        'gdn_conv_w': gdn_conv_w, 'gdn_a_log': gdn_a_log, 'gdn_dt_bias': gdn_dt_bias, 'gdn_o_norm': gdn_o_norm,
        'diff_lambda_q1': diff_lambda_q1, 'diff_lambda_k1': diff_lambda_k1,
        'diff_lambda_q2': diff_lambda_q2, 'diff_lambda_k2': diff_lambda_k2, 'diff_subln': diff_subln,
        'router_w': router_w, 'router_b': router_b,
        'moe_w1': moe_w1, 'moe_b1': moe_b1, 'moe_w2': moe_w2, 'moe_b2': moe_b2,
        'ln1_g': ln1_g, 'ln1_b': ln1_b, 'ln2_g': ln2_g, 'ln2_b': ln2_b,
    }
    depth = w_in.shape[0]
    alpha = (2.0 * depth) ** 0.25
    w1_all = [w.reshape(moe_w1.shape[:3] + (-1,)) for w in _w1_prep(moe_w1.reshape((-1,) + moe_w1.shape[2:]))]
    layers = [_prep_layer(params, layer, w1_all) for layer in range(depth)]
    return (_trunk(x_prompt, layers, alpha), _trunk(x_sample, layers, alpha))
```
